```python
import math
import jax, jax.numpy as jnp
from jax import lax
import numpy as np

D_MODEL = 4096
BATCH = 4
SEQ = 4096
DEPTH = 1

CONV_DIM = D_MODEL // 2
CONV_GROUPS = 16
CONV_WIDTH = 3
N_HEADS = 16
QK_NOPE_DIM = 128
QK_ROPE_DIM = 64
V_HEAD_DIM = 128
QK_HEAD_DIM = QK_NOPE_DIM + QK_ROPE_DIM
ATTN_DIM = N_HEADS * V_HEAD_DIM
Q_LORA_RANK = 1024
KV_LORA_RANK = 512
N_BRANCHES = 2
D_FF = ((8 * D_MODEL // 3 + 255) // 256) * 256
ROPE_THETA = 10000.0
RMS_EPS = 1e-6
Q_BLOCK = 128
SOFTMAX_SCALE = 1.0 / math.sqrt(QK_HEAD_DIM)

IN_SPLITS = (CONV_DIM, CONV_DIM, CONV_DIM, Q_LORA_RANK, KV_LORA_RANK, QK_ROPE_DIM,
             N_BRANCHES * D_MODEL)
IN_COLS = CONV_DIM * 3 + Q_LORA_RANK + KV_LORA_RANK + QK_ROPE_DIM + N_BRANCHES * D_MODEL

kernel_name = "hybrid_shortconv_mla_gated_encoder"


def split_columns(z, widths):
    parts = []
    start = 0
    for wdt in widths:
        parts.append(z[..., start:start + wdt])
        start += wdt
    return parts


def rms_norm(x, g):
    xf = x.astype(jnp.float32)
    inv = lax.rsqrt(jnp.mean(xf * xf, axis=-1, keepdims=True) + RMS_EPS)
    return (xf * inv * g.astype(jnp.float32)).astype(x.dtype)


def centred_short_conv(u, w):
    up = jnp.pad(u, ((0, 0), (1, 1), (0, 0)))
    return up[:, :-2] * w[0] + up[:, 1:-1] * w[1] + up[:, 2:] * w[2]


def rotary(t, cos, sin):
    half = t.shape[-1] // 2
    t1, t2 = t[..., :half], t[..., half:]
    return jnp.concatenate([t1 * cos - t2 * sin, t1 * sin + t2 * cos], axis=-1)


def mla_attention(q_nope, q_rope, k_nope, k_rope, v):
    b, s, h, _ = q_nope.shape
    nblk = s // Q_BLOCK

    def to_blocks(t):
        return jnp.swapaxes(t.reshape((b, nblk, Q_BLOCK) + t.shape[2:]), 0, 1)

    def block(qs):
        qn, qr = qs
        sc = jnp.einsum('bqhd,bkhd->bhqk', qn, k_nope).astype(jnp.float32)
        sc = sc + jnp.einsum('bqhr,bkr->bhqk', qr, k_rope).astype(jnp.float32)
        p = jax.nn.softmax(sc * SOFTMAX_SCALE, axis=-1).astype(v.dtype)
        return jnp.einsum('bhqk,bkhd->bqhd', p, v)

    out = lax.map(block, (to_blocks(q_nope), to_blocks(q_rope)))
    return jnp.swapaxes(out, 0, 1).reshape(b, s, h * V_HEAD_DIM)


def setup_inputs(seed: int = 0) -> dict:
    key = jax.random.key(seed)
    ks = jax.random.split(key, 20)

    def w(k, shape, fan_in):
        return jax.random.normal(k, shape, jnp.float32) * (fan_in ** -0.5)

    def gain(k, shape):
        return 1.0 + 0.02 * jax.random.normal(k, shape, jnp.float32)

    x = jax.random.normal(ks[0], (BATCH, SEQ, D_MODEL), jnp.float32)
    positions = (jnp.arange(SEQ, dtype=jnp.int32)[None, :]
                 + jax.random.randint(ks[1], (BATCH, 1), 0, 1024, dtype=jnp.int32))
    return {
        "x": x,
        "positions": positions,
        "g_mix": gain(ks[2], (DEPTH, D_MODEL)),
        "w_in": w(ks[3], (DEPTH, D_MODEL, IN_COLS), D_MODEL),
        "b_gate": 0.01 * jax.random.normal(ks[4], (DEPTH, N_BRANCHES * D_MODEL), jnp.float32),
        "conv_w": w(ks[5], (DEPTH, CONV_WIDTH, CONV_DIM), CONV_WIDTH),
        "g_q_a": gain(ks[6], (DEPTH, Q_LORA_RANK)),
        "w_q_b": w(ks[7], (DEPTH, Q_LORA_RANK, N_HEADS * QK_HEAD_DIM), Q_LORA_RANK),
        "g_kv_a": gain(ks[8], (DEPTH, KV_LORA_RANK)),
        "w_kv_b": w(ks[9], (DEPTH, KV_LORA_RANK, N_HEADS * (QK_NOPE_DIM + V_HEAD_DIM)), KV_LORA_RANK),
        "w_branch": w(ks[10], (DEPTH, N_BRANCHES, CONV_DIM, D_MODEL), CONV_DIM),
        "w_out": w(ks[11], (DEPTH, D_MODEL, D_MODEL), D_MODEL),
        "g_ffn": gain(ks[12], (DEPTH, D_MODEL)),
        "w_ffn_gate": w(ks[13], (DEPTH, D_MODEL, D_FF), D_MODEL),
        "w_ffn_up": w(ks[14], (DEPTH, D_MODEL, D_FF), D_MODEL),
        "w_ffn_down": w(ks[15], (DEPTH, D_FF, D_MODEL), D_FF),
        "g_final": gain(ks[16], (D_MODEL,)),
    }


def reference(x, positions, g_mix, w_in, b_gate, conv_w, g_q_a, w_q_b, g_kv_a, w_kv_b,
              w_branch, w_out, g_ffn, w_ffn_gate, w_ffn_up, w_ffn_down, g_final):
    b, s, d = x.shape
    dt = x.dtype
    inv_freq = ROPE_THETA ** (-jnp.arange(0, QK_ROPE_DIM, 2, dtype=jnp.float32) / QK_ROPE_DIM)
    ang = positions.astype(jnp.float32)[..., None] * inv_freq[None, None, :]
    cos, sin = jnp.cos(ang).astype(dt), jnp.sin(ang).astype(dt)

    for l in range(DEPTH):
        h = rms_norm(x, g_mix[l])
        z = h @ w_in[l]
        c_b, c_c, c_h, q_a, kv_a, k_rope, z_gate = split_columns(z, IN_SPLITS)

        y_a = c_b * centred_short_conv(c_c * c_h, conv_w[l])

        q = (rms_norm(q_a, g_q_a[l]) @ w_q_b[l]).reshape(b, s, N_HEADS, QK_HEAD_DIM)
        q_nope, q_rope = q[..., :QK_NOPE_DIM], q[..., QK_NOPE_DIM:]
        q_rope = rotary(q_rope, cos[:, :, None, :], sin[:, :, None, :])
        kv = (rms_norm(kv_a, g_kv_a[l]) @ w_kv_b[l]).reshape(b, s, N_HEADS, QK_NOPE_DIM + V_HEAD_DIM)
        k_nope, v = kv[..., :QK_NOPE_DIM], kv[..., QK_NOPE_DIM:]
        k_rope = rotary(k_rope, cos, sin)
        y_b = mla_attention(q_nope, q_rope, k_nope, k_rope, v)

        y_br = jnp.einsum('nbsc,ncd->bsnd', jnp.stack([y_a, y_b], axis=0), w_branch[l])
        gates = jax.nn.sigmoid((z_gate + b_gate[l]).astype(jnp.float32)).astype(dt)
        merged = jnp.sum(gates.reshape(b, s, N_BRANCHES, d) * y_br, axis=2)
        x = x + merged @ w_out[l]

        h2 = rms_norm(x, g_ffn[l])
        x = x + (jax.nn.silu(h2 @ w_ffn_gate[l]) * (h2 @ w_ffn_up[l])) @ w_ffn_down[l]

    return rms_norm(x, g_final)
```

```python
import functools
import math

import jax
import jax.numpy as jnp
from jax import lax
from jax.experimental import pallas as pl
from jax.experimental.pallas import tpu as pltpu

F32 = jnp.float32
BF16 = jnp.bfloat16

RMS_EPS = 1e-6
ROPE_THETA = 10000.0
QK_NOPE_DIM = 128
QK_ROPE_DIM = 64
V_HEAD_DIM = 128
QK_HEAD_DIM = QK_NOPE_DIM + QK_ROPE_DIM
HEAD_PAD = 256
LANE = 128
HALO_ROWS = 16
VMEM_LIMIT = 60 * 1024 * 1024


def _cparams(*sem):
    return pltpu.CompilerParams(dimension_semantics=sem, vmem_limit_bytes=VMEM_LIMIT)


def _resident(block_shape, index_map):
    return pl.BlockSpec(block_shape, index_map, pipeline_mode=pl.Buffered(1))


def _dot(a, b):
    return jnp.dot(a, b, preferred_element_type=F32)


def _rms(x, g):
    inv = lax.rsqrt(jnp.mean(x * x, axis=-1, keepdims=True) + RMS_EPS)
    return x * inv * g


def _norm_kernel(x_ref, g_ref, o_ref):
    o_ref[...] = _rms(x_ref[...], g_ref[...]).astype(o_ref.dtype)


def _norm(x, g, tm):
    t, d = x.shape
    return pl.pallas_call(
        _norm_kernel,
        grid=(t // tm,),
        in_specs=[pl.BlockSpec((tm, d), lambda i: (i, 0)),
                  pl.BlockSpec((1, d), lambda i: (0, 0))],
        out_specs=pl.BlockSpec((tm, d), lambda i: (i, 0)),
        out_shape=jax.ShapeDtypeStruct((t, d), BF16),
        compiler_params=_cparams("parallel"),
        name="norm",
    )(x, g)


def _conv_proj_kernel(h_ref, wb_ref, wc_ref, wh_ref, cb_ref, u_ref):
    h = h_ref[...]
    cb_ref[...] = _dot(h, wb_ref[...]).astype(cb_ref.dtype)
    u_ref[...] = (_dot(h, wc_ref[...]) * _dot(h, wh_ref[...])).astype(u_ref.dtype)


def _conv_proj(h, w_main, conv_dim, tm, tn):
    t, d = h.shape
    nb = conv_dim // tn
    w_spec = lambda off: pl.BlockSpec((d, tn), lambda i, j: (0, j + off))
    out = jax.ShapeDtypeStruct((t, conv_dim), BF16)
    return pl.pallas_call(
        _conv_proj_kernel,
        grid=(t // tm, nb),
        in_specs=[pl.BlockSpec((tm, d), lambda i, j: (i, 0)),
                  w_spec(0), w_spec(nb), w_spec(2 * nb)],
        out_specs=[pl.BlockSpec((tm, tn), lambda i, j: (i, j))] * 2,
        out_shape=[out, out],
        compiler_params=_cparams("parallel", "arbitrary"),
        name="conv_proj",
    )(h, w_main, w_main, w_main)


def _gate_proj_kernel(h_ref, w_ref, b_ref, o_ref):
    z = _dot(h_ref[...], w_ref[...]) + b_ref[...]
    o_ref[...] = (1.0 / (1.0 + jnp.exp(-z))).astype(o_ref.dtype)


def _gate_proj(h, w_gate, b_gate, tm, tn):
    t, d = h.shape
    n = w_gate.shape[1]
    return pl.pallas_call(
        _gate_proj_kernel,
        grid=(t // tm, n // tn),
        in_specs=[pl.BlockSpec((tm, d), lambda i, j: (i, 0)),
                  pl.BlockSpec((d, tn), lambda i, j: (0, j)),
                  pl.BlockSpec((1, tn), lambda i, j: (0, j))],
        out_specs=pl.BlockSpec((tm, tn), lambda i, j: (i, j)),
        out_shape=jax.ShapeDtypeStruct((t, n), BF16),
        compiler_params=_cparams("parallel", "arbitrary"),
        name="gate_proj",
    )(h, w_gate, b_gate)


def _rope_tables(pos_ref, invf_ref, sgn_ref):
    ang = pos_ref[...].astype(F32) * invf_ref[...]
    return jnp.cos(ang), jnp.sin(ang) * sgn_ref[...]


def _q_proj_kernel(n_heads, h_ref, pos_ref, invf_ref, sgn_ref, wqa_ref, gq_ref,
                   wq_ref, wqs_ref, q_ref):
    qn = _rms(_dot(h_ref[...], wqa_ref[...]), gq_ref[...]).astype(BF16)
    q = _dot(qn, wq_ref[...])
    qs = _dot(qn, wqs_ref[...])
    cos, sin = _rope_tables(pos_ref, invf_ref, sgn_ref)
    for hd in range(n_heads):
        a = hd * HEAD_PAD
        q_ref[:, a:a + LANE] = q[:, a:a + LANE].astype(q_ref.dtype)
        rot = q[:, a + LANE:a + HEAD_PAD] * cos + qs[:, hd * LANE:(hd + 1) * LANE] * sin
        q_ref[:, a + LANE:a + HEAD_PAD] = rot.astype(q_ref.dtype)


def _q_proj(h, pos, invf, sgn, w_main, qa_col, q_rank, g_q, wq, wqs, n_heads, tm):
    t, d = h.shape
    const = lambda i: (0, 0)
    return pl.pallas_call(
        functools.partial(_q_proj_kernel, n_heads),
        grid=(t // tm,),
        in_specs=[pl.BlockSpec((tm, d), lambda i: (i, 0)),
                  pl.BlockSpec((tm, 1), lambda i: (i, 0)),
                  _resident((1, LANE), const),
                  _resident((1, LANE), const),
                  _resident((d, q_rank), lambda i: (0, qa_col // q_rank)),
                  _resident((1, q_rank), const),
                  _resident(wq.shape, const),
                  _resident(wqs.shape, const)],
        out_specs=pl.BlockSpec((tm, n_heads * HEAD_PAD), lambda i: (i, 0)),
        out_shape=jax.ShapeDtypeStruct((t, n_heads * HEAD_PAD), BF16),
        compiler_params=_cparams("parallel"),
        name="q_proj",
    )(h, pos, invf, sgn, w_main, g_q, wq, wqs)


def _kv_proj_kernel(n_heads, h_ref, pos_ref, invf_ref, sgn_ref, wkva_ref, gkv_ref,
                    wkr_ref, wkrs_ref, wkt_ref, wv_ref, kt_ref, v_ref):
    h = h_ref[...]
    kvn = _rms(_dot(h, wkva_ref[...]), gkv_ref[...]).astype(BF16)
    v_ref[...] = _dot(kvn, wv_ref[...]).astype(v_ref.dtype)
    knt = lax.dot_general(wkt_ref[...], kvn, (((1,), (1,)), ((), ())),
                          preferred_element_type=F32)
    cos, sin = _rope_tables(pos_ref, invf_ref, sgn_ref)
    krot = _dot(h, wkr_ref[...]) * cos + _dot(h, wkrs_ref[...]) * sin
    krt = krot.T.astype(kt_ref.dtype)
    for hd in range(n_heads):
        kt_ref[0, hd, 0:LANE, :] = knt[hd * LANE:(hd + 1) * LANE, :].astype(kt_ref.dtype)
        kt_ref[0, hd, LANE:HEAD_PAD, :] = krt


def _kv_proj(h, pos, invf, sgn, w_main, kva_col, kv_rank, g_kv, wkr, wkrs, wkt, wv,
             n_heads, batch, seq, tm):
    t, d = h.shape
    spt = seq // tm
    const = lambda i: (0, 0)
    return pl.pallas_call(
        functools.partial(_kv_proj_kernel, n_heads),
        grid=(t // tm,),
        in_specs=[pl.BlockSpec((tm, d), lambda i: (i, 0)),
                  pl.BlockSpec((tm, 1), lambda i: (i, 0)),
                  _resident((1, LANE), const),
                  _resident((1, LANE), const),
                  _resident((d, kv_rank), lambda i: (0, kva_col // kv_rank)),
                  _resident((1, kv_rank), const),
                  _resident(wkr.shape, const),
                  _resident(wkrs.shape, const),
                  _resident(wkt.shape, const),
                  _resident(wv.shape, const)],
        out_specs=[pl.BlockSpec((1, n_heads, HEAD_PAD, tm), lambda i: (i // spt, 0, 0, i % spt)),
                   pl.BlockSpec((tm, n_heads * V_HEAD_DIM), lambda i: (i, 0))],
        out_shape=[jax.ShapeDtypeStruct((batch, n_heads, HEAD_PAD, seq), BF16),
                   jax.ShapeDtypeStruct((t, n_heads * V_HEAD_DIM), BF16)],
        compiler_params=_cparams("parallel"),
        name="kv_proj",
    )(h, pos, invf, sgn, w_main, g_kv, wkr, wkrs, wkt, wv)


def _attn_kernel(scale, q_ref, kt_ref, v_ref, o_ref):
    s = _dot(q_ref[...], kt_ref[0, 0])
    m = jnp.max(s, axis=-1, keepdims=True)
    p = jnp.exp((s - m) * scale)
    l = jnp.sum(p, axis=-1, keepdims=True)
    o = _dot(p.astype(BF16), v_ref[...])
    o_ref[...] = (o / l).astype(o_ref.dtype)


def _attention(q, kt, v, n_heads, batch, seq, tq):
    t = q.shape[0]
    qpt = seq // tq
    scale = 1.0 / math.sqrt(QK_HEAD_DIM)
    return pl.pallas_call(
        functools.partial(_attn_kernel, scale),
        grid=(batch, n_heads, qpt),
        in_specs=[pl.BlockSpec((tq, HEAD_PAD), lambda b, hd, i: (b * qpt + i, hd)),
                  pl.BlockSpec((1, 1, HEAD_PAD, seq), lambda b, hd, i: (b, hd, 0, 0)),
                  pl.BlockSpec((seq, V_HEAD_DIM), lambda b, hd, i: (b, hd))],
        out_specs=pl.BlockSpec((tq, V_HEAD_DIM), lambda b, hd, i: (b * qpt + i, hd)),
        out_shape=jax.ShapeDtypeStruct((t, n_heads * V_HEAD_DIM), BF16),
        compiler_params=_cparams("parallel", "parallel", "arbitrary"),
        name="attention",
    )(q, kt, v)


def _branch_kernel(tiles_per_seq, cchunk, cb_ref, u_ref, up_ref, un_ref, yb_ref, cw_ref,
                   ga_ref, gb_ref, w0_ref, w1_ref, o_ref, ya_ref):
    i = pl.program_id(0)
    tm, c = u_ref.shape

    @pl.when(pl.program_id(1) == 0)
    def _():
        keep_prev = (i % tiles_per_seq != 0).astype(F32)
        keep_next = (i % tiles_per_seq != tiles_per_seq - 1).astype(F32)
        row = lax.broadcasted_iota(jnp.int32, (tm, 1), 0)
        for c0 in range(0, c, cchunk):
            cs = slice(c0, c0 + cchunk)
            u = u_ref[:, cs].astype(F32)
            prev_row = up_ref[HALO_ROWS - 1:HALO_ROWS, cs].astype(F32) * keep_prev
            next_row = un_ref[0:1, cs].astype(F32) * keep_next
            u_dn = jnp.where(row == 0, prev_row, pltpu.roll(u, 1, axis=0))
            u_up = jnp.where(row == tm - 1, next_row, pltpu.roll(u, tm - 1, axis=0))
            conv = u_dn * cw_ref[0:1, cs] + u * cw_ref[1:2, cs] + u_up * cw_ref[2:3, cs]
            ya_ref[:, cs] = (cb_ref[:, cs].astype(F32) * conv).astype(ya_ref.dtype)

    pa = _dot(ya_ref[...], w0_ref[...])
    pb = _dot(yb_ref[...], w1_ref[...])
    o_ref[...] = (ga_ref[...].astype(F32) * pa + gb_ref[...].astype(F32) * pb).astype(o_ref.dtype)


def _branch(cb, u, yb, conv_w, gates, w_br0, w_br1, seq, tm, tn):
    t, c = cb.shape
    d = w_br0.shape[1]
    nb = d // tn
    hb = tm // HALO_ROWS
    last_hb = t // HALO_ROWS - 1
    row = lambda i, j: (i, 0)
    return pl.pallas_call(
        functools.partial(_branch_kernel, seq // tm, min(c, 512)),
        grid=(t // tm, nb),
        in_specs=[pl.BlockSpec((tm, c), row),
                  pl.BlockSpec((tm, c), row),
                  pl.BlockSpec((HALO_ROWS, c), lambda i, j: (jnp.maximum(i * hb - 1, 0), 0)),
                  pl.BlockSpec((HALO_ROWS, c), lambda i, j: (jnp.minimum((i + 1) * hb, last_hb), 0)),
                  pl.BlockSpec((tm, c), row),
                  pl.BlockSpec(conv_w.shape, lambda i, j: (0, 0)),
                  pl.BlockSpec((tm, tn), lambda i, j: (i, j)),
                  pl.BlockSpec((tm, tn), lambda i, j: (i, j + nb)),
                  pl.BlockSpec((c, tn), lambda i, j: (0, j)),
                  pl.BlockSpec((c, tn), lambda i, j: (0, j))],
        out_specs=pl.BlockSpec((tm, tn), lambda i, j: (i, j)),
        out_shape=jax.ShapeDtypeStruct((t, d), BF16),
        scratch_shapes=[pltpu.VMEM((tm, c), BF16)],
        compiler_params=_cparams("parallel", "arbitrary"),
        name="branch",
    )(cb, u, u, u, yb, conv_w, gates, gates, w_br0, w_br1)


def _out_proj_kernel(m_ref, w_ref, x_ref, o_ref):
    o_ref[...] = x_ref[...] + _dot(m_ref[...], w_ref[...])


def _out_proj(m, w_out, x, tm, tn):
    t, d = m.shape
    n = w_out.shape[1]
    return pl.pallas_call(
        _out_proj_kernel,
        grid=(t // tm, n // tn),
        in_specs=[pl.BlockSpec((tm, d), lambda i, j: (i, 0)),
                  pl.BlockSpec((d, tn), lambda i, j: (0, j)),
                  pl.BlockSpec((tm, tn), lambda i, j: (i, j))],
        out_specs=pl.BlockSpec((tm, tn), lambda i, j: (i, j)),
        out_shape=jax.ShapeDtypeStruct((t, n), F32),
        compiler_params=_cparams("parallel", "arbitrary"),
        name="out_proj",
    )(m, w_out, x)


def _ffn_kernel(final_norm, x_ref, gf_ref, wg_ref, wu_ref, wd_ref, gl_ref, o_ref, h2_ref):
    j = pl.program_id(1)

    @pl.when(j == 0)
    def _():
        x = x_ref[...]
        h2_ref[...] = _rms(x, gf_ref[...]).astype(h2_ref.dtype)
        o_ref[...] = x

    h2 = h2_ref[...]
    g = _dot(h2, wg_ref[...])
    a = (g * (1.0 / (1.0 + jnp.exp(-g)))) * _dot(h2, wu_ref[...])
    o_ref[...] += _dot(a.astype(BF16), wd_ref[...])

    if final_norm:
        @pl.when(j == pl.num_programs(1) - 1)
        def _():
            o_ref[...] = _rms(o_ref[...], gl_ref[...])


def _ffn(x1, g_ffn, w_g, w_u, w_d, g_final, final_norm, tm, tf):
    t, d = x1.shape
    f = w_g.shape[1]
    return pl.pallas_call(
        functools.partial(_ffn_kernel, final_norm),
        grid=(t // tm, f // tf),
        in_specs=[_resident((tm, d), lambda i, j: (i, 0)),
                  pl.BlockSpec((1, d), lambda i, j: (0, 0)),
                  pl.BlockSpec((d, tf), lambda i, j: (0, j)),
                  pl.BlockSpec((d, tf), lambda i, j: (0, j)),
                  pl.BlockSpec((tf, d), lambda i, j: (j, 0)),
                  pl.BlockSpec((1, d), lambda i, j: (0, 0))],
        out_specs=pl.BlockSpec((tm, d), lambda i, j: (i, 0)),
        out_shape=jax.ShapeDtypeStruct((t, d), F32),
        scratch_shapes=[pltpu.VMEM((tm, d), BF16)],
        compiler_params=_cparams("parallel", "arbitrary"),
        name="ffn",
    )(x1, g_ffn, w_g, w_u, w_d, g_final)


def _pad_cols(w, width):
    return jnp.pad(w, ((0, 0), (0, width - w.shape[1])))


def _swap_halves(w):
    half = w.shape[-1] // 2
    return jnp.concatenate([w[..., half:], w[..., :half]], axis=-1)


def _q_weights(w_q_b, n_heads):
    r = w_q_b.shape[0]
    w = w_q_b.reshape(r, n_heads, QK_HEAD_DIM)
    nope, rope = w[..., :QK_NOPE_DIM], w[..., QK_NOPE_DIM:]
    zeros = jnp.zeros((r, n_heads, HEAD_PAD - QK_HEAD_DIM), w.dtype)
    main = jnp.concatenate([nope, rope, zeros], axis=-1).reshape(r, n_heads * HEAD_PAD)
    swapped = jnp.concatenate([_swap_halves(rope), zeros], axis=-1).reshape(r, n_heads * LANE)
    return main.astype(BF16), swapped.astype(BF16)


def _pick(pref, n):
    if n <= pref:
        return n
    t = pref
    while n % t:
        t //= 2
    return t


def kernel(x, positions, g_mix, w_in, b_gate, conv_w, g_q_a, w_q_b, g_kv_a, w_kv_b, w_branch,
           w_out, g_ffn, w_ffn_gate, w_ffn_up, w_ffn_down, g_final):
    batch, seq, d = x.shape
    depth = w_in.shape[0]
    t = batch * seq
    conv_dim = conv_w.shape[-1]
    q_rank = g_q_a.shape[-1]
    kv_rank = g_kv_a.shape[-1]
    n_heads = w_q_b.shape[-1] // QK_HEAD_DIM
    qa_col = 3 * conv_dim
    kva_col = qa_col + q_rank
    kr_col = kva_col + kv_rank
    gate_col = kr_col + QK_ROPE_DIM

    xf = x.reshape(t, d)
    pos = positions.reshape(t, 1)
    inv_freq = ROPE_THETA ** (-jnp.arange(0, QK_ROPE_DIM, 2, dtype=F32) / QK_ROPE_DIM)
    zeros = jnp.zeros((LANE - QK_ROPE_DIM,), F32)
    invf = jnp.concatenate([inv_freq, inv_freq, zeros])[None, :]
    half = QK_ROPE_DIM // 2
    sgn = jnp.concatenate([-jnp.ones((half,), F32), jnp.ones((half,), F32), zeros])[None, :]

    for l in range(depth):
        w_in_l = w_in[l]
        w_main = w_in_l[:, :kr_col].astype(BF16)
        w_kr = w_in_l[:, kr_col:gate_col]
        wkr = _pad_cols(w_kr, LANE).astype(BF16)
        wkrs = _pad_cols(_swap_halves(w_kr), LANE).astype(BF16)
        w_gate = w_in_l[:, gate_col:].astype(BF16)
        wq, wqs = _q_weights(w_q_b[l], n_heads)
        w_kv = w_kv_b[l].reshape(kv_rank, n_heads, QK_NOPE_DIM + V_HEAD_DIM)
        wkt = w_kv[..., :QK_NOPE_DIM].reshape(kv_rank, n_heads * QK_NOPE_DIM).T.astype(BF16)
        wv = w_kv[..., QK_NOPE_DIM:].reshape(kv_rank, n_heads * V_HEAD_DIM).astype(BF16)
        w_br = w_branch[l].astype(BF16)
        w_o = w_out[l].astype(BF16)
        w_fg = w_ffn_gate[l].astype(BF16)
        w_fu = w_ffn_up[l].astype(BF16)
        w_fd = w_ffn_down[l].astype(BF16)

        h = _norm(xf, g_mix[l][None, :], _pick(256, t))
        cb, u = _conv_proj(h, w_main, conv_dim, _pick(1024, t), _pick(256, conv_dim))
        gates = _gate_proj(h, w_gate, b_gate[l][None, :], _pick(1024, t), _pick(512, d))
        q = _q_proj(h, pos, invf, sgn, w_main, qa_col, q_rank, g_q_a[l][None, :], wq, wqs,
                    n_heads, _pick(512, seq))
        kt, v = _kv_proj(h, pos, invf, sgn, w_main, kva_col, kv_rank, g_kv_a[l][None, :],
                         wkr, wkrs, wkt, wv, n_heads, batch, seq, _pick(512, seq))
        yb = _attention(q, kt, v, n_heads, batch, seq, _pick(512, seq))
        m = _branch(cb, u, yb, conv_w[l], gates, w_br[0], w_br[1], seq,
                    _pick(1024, seq), _pick(512, d))
        xf = _out_proj(m, w_o, xf, _pick(1024, t), _pick(512, d))
        xf = _ffn(xf, g_ffn[l][None, :], w_fg, w_fu, w_fd, g_final[None, :], l == depth - 1,
                  _pick(512, t), _pick(256, w_fg.shape[1]))
    return xf.reshape(batch, seq, d)
```

```python
import functools
import math

import jax
import jax.numpy as jnp
from jax import lax
from jax.experimental import pallas as pl
from jax.experimental.pallas import tpu as pltpu

F32 = jnp.float32
BF16 = jnp.bfloat16

RMS_EPS = 1e-6
ROPE_THETA = 10000.0
QK_NOPE_DIM = 128
QK_ROPE_DIM = 64
V_HEAD_DIM = 128
QK_HEAD_DIM = QK_NOPE_DIM + QK_ROPE_DIM
HEAD_PAD = 256
LANE = 128
HALO_ROWS = 16
VMEM_LIMIT = 60 * 1024 * 1024


def _cparams(*sem):
    return pltpu.CompilerParams(dimension_semantics=sem, vmem_limit_bytes=VMEM_LIMIT)


def _resident(block_shape, index_map):
    return pl.BlockSpec(block_shape, index_map, pipeline_mode=pl.Buffered(1))


def _dot(a, b):
    return jnp.dot(a, b, preferred_element_type=F32)


def _rms(x, g):
    inv = lax.rsqrt(jnp.mean(x * x, axis=-1, keepdims=True) + RMS_EPS)
    return x * inv * g


def _norm_kernel(x_ref, g_ref, o_ref):
    o_ref[...] = _rms(x_ref[...], g_ref[...]).astype(o_ref.dtype)


def _norm(x, g, tm):
    t, d = x.shape
    return pl.pallas_call(
        _norm_kernel,
        grid=(t // tm,),
        in_specs=[pl.BlockSpec((tm, d), lambda i: (i, 0)),
                  pl.BlockSpec((1, d), lambda i: (0, 0))],
        out_specs=pl.BlockSpec((tm, d), lambda i: (i, 0)),
        out_shape=jax.ShapeDtypeStruct((t, d), BF16),
        compiler_params=_cparams("parallel"),
        name="norm",
    )(x, g)


def _conv_proj_kernel(h_ref, w_ref, cb_ref, u_ref):
    tn = cb_ref.shape[1]
    z = _dot(h_ref[...], w_ref[0])
    cb_ref[...] = z[:, :tn].astype(cb_ref.dtype)
    u_ref[...] = (z[:, tn:2 * tn] * z[:, 2 * tn:]).astype(u_ref.dtype)


def _conv_proj(h, w_conv, tm):
    t, d = h.shape
    nb, _, tn3 = w_conv.shape
    tn = tn3 // 3
    out = jax.ShapeDtypeStruct((t, nb * tn), BF16)
    return pl.pallas_call(
        _conv_proj_kernel,
        grid=(t // tm, nb),
        in_specs=[pl.BlockSpec((tm, d), lambda i, j: (i, 0)),
                  pl.BlockSpec((1, d, tn3), lambda i, j: (j, 0, 0))],
        out_specs=[pl.BlockSpec((tm, tn), lambda i, j: (i, j))] * 2,
        out_shape=[out, out],
        compiler_params=_cparams("parallel", "arbitrary"),
        name="conv_proj",
    )(h, w_conv)


def _gate_proj_kernel(h_ref, w_ref, b_ref, o_ref):
    z = _dot(h_ref[...], w_ref[0]) + b_ref[...]
    o_ref[...] = (1.0 / (1.0 + jnp.exp(-z))).astype(o_ref.dtype)


def _gate_proj(h, w_gate, b_gate, tm):
    t, d = h.shape
    nb, _, tn = w_gate.shape
    return pl.pallas_call(
        _gate_proj_kernel,
        grid=(t // tm, nb),
        in_specs=[pl.BlockSpec((tm, d), lambda i, j: (i, 0)),
                  pl.BlockSpec((1, d, tn), lambda i, j: (j, 0, 0)),
                  pl.BlockSpec((1, tn), lambda i, j: (0, j))],
        out_specs=pl.BlockSpec((tm, tn), lambda i, j: (i, j)),
        out_shape=jax.ShapeDtypeStruct((t, nb * tn), BF16),
        compiler_params=_cparams("parallel", "arbitrary"),
        name="gate_proj",
    )(h, w_gate, b_gate)


def _rope_tables(pos_ref, invf_ref, sgn_ref):
    ang = pos_ref[...].astype(F32) * invf_ref[...]
    return jnp.cos(ang), jnp.sin(ang) * sgn_ref[...]


def _q_proj_kernel(n_heads, h_ref, pos_ref, invf_ref, sgn_ref, wqa_ref, gq_ref,
                   wq_ref, wqs_ref, q_ref):
    qn = _rms(_dot(h_ref[...], wqa_ref[...]), gq_ref[...]).astype(BF16)
    q = _dot(qn, wq_ref[...])
    qs = _dot(qn, wqs_ref[...])
    cos, sin = _rope_tables(pos_ref, invf_ref, sgn_ref)
    for hd in range(n_heads):
        a = hd * HEAD_PAD
        q_ref[:, a:a + LANE] = q[:, a:a + LANE].astype(q_ref.dtype)
        rot = q[:, a + LANE:a + HEAD_PAD] * cos + qs[:, hd * LANE:(hd + 1) * LANE] * sin
        q_ref[:, a + LANE:a + HEAD_PAD] = rot.astype(q_ref.dtype)


def _q_proj(h, pos, invf, sgn, w_qa, g_q, wq, wqs, n_heads, tm):
    t, d = h.shape
    const = lambda i: (0, 0)
    return pl.pallas_call(
        functools.partial(_q_proj_kernel, n_heads),
        grid=(t // tm,),
        in_specs=[pl.BlockSpec((tm, d), lambda i: (i, 0)),
                  pl.BlockSpec((tm, 1), lambda i: (i, 0)),
                  _resident((1, LANE), const),
                  _resident((1, LANE), const),
                  _resident(w_qa.shape, const),
                  _resident(g_q.shape, const),
                  _resident(wq.shape, const),
                  _resident(wqs.shape, const)],
        out_specs=pl.BlockSpec((tm, n_heads * HEAD_PAD), lambda i: (i, 0)),
        out_shape=jax.ShapeDtypeStruct((t, n_heads * HEAD_PAD), BF16),
        compiler_params=_cparams("parallel"),
        name="q_proj",
    )(h, pos, invf, sgn, w_qa, g_q, wq, wqs)


def _kv_proj_kernel(n_heads, h_ref, pos_ref, invf_ref, sgn_ref, wkva_ref, gkv_ref,
                    wkr_ref, wkrs_ref, wkt_ref, wv_ref, kt_ref, v_ref):
    h = h_ref[...]
    kvn = _rms(_dot(h, wkva_ref[...]), gkv_ref[...]).astype(BF16)
    v_ref[...] = _dot(kvn, wv_ref[...]).astype(v_ref.dtype)
    knt = lax.dot_general(wkt_ref[...], kvn, (((1,), (1,)), ((), ())),
                          preferred_element_type=F32)
    cos, sin = _rope_tables(pos_ref, invf_ref, sgn_ref)
    krot = _dot(h, wkr_ref[...]) * cos + _dot(h, wkrs_ref[...]) * sin
    krt = krot.T.astype(kt_ref.dtype)
    for hd in range(n_heads):
        kt_ref[0, hd, 0:LANE, :] = knt[hd * LANE:(hd + 1) * LANE, :].astype(kt_ref.dtype)
        kt_ref[0, hd, LANE:HEAD_PAD, :] = krt


def _kv_proj(h, pos, invf, sgn, w_kva, g_kv, wkr, wkrs, wkt, wv, n_heads, batch, seq, tm):
    t, d = h.shape
    spt = seq // tm
    const = lambda i: (0, 0)
    return pl.pallas_call(
        functools.partial(_kv_proj_kernel, n_heads),
        grid=(t // tm,),
        in_specs=[pl.BlockSpec((tm, d), lambda i: (i, 0)),
                  pl.BlockSpec((tm, 1), lambda i: (i, 0)),
                  _resident((1, LANE), const),
                  _resident((1, LANE), const),
                  _resident(w_kva.shape, const),
                  _resident(g_kv.shape, const),
                  _resident(wkr.shape, const),
                  _resident(wkrs.shape, const),
                  _resident(wkt.shape, const),
                  _resident(wv.shape, const)],
        out_specs=[pl.BlockSpec((1, n_heads, HEAD_PAD, tm), lambda i: (i // spt, 0, 0, i % spt)),
                   pl.BlockSpec((tm, n_heads * V_HEAD_DIM), lambda i: (i, 0))],
        out_shape=[jax.ShapeDtypeStruct((batch, n_heads, HEAD_PAD, seq), BF16),
                   jax.ShapeDtypeStruct((t, n_heads * V_HEAD_DIM), BF16)],
        compiler_params=_cparams("parallel"),
        name="kv_proj",
    )(h, pos, invf, sgn, w_kva, g_kv, wkr, wkrs, wkt, wv)


def _attn_kernel(scale_log2e, kc, q_ref, kt_ref, v_ref, o_ref):
    q = q_ref[...]
    tq = q.shape[0]
    seq = kt_ref.shape[-1]
    m = jnp.full((tq, 1), -jnp.inf, F32)
    l = jnp.zeros((tq, LANE), F32)
    acc = jnp.zeros((tq, v_ref.shape[-1]), F32)
    for c0 in range(0, seq, kc):
        s = _dot(q, kt_ref[0, 0, :, c0:c0 + kc]) * scale_log2e
        m_new = jnp.maximum(m, jnp.max(s, axis=-1, keepdims=True))
        alpha = jnp.exp2(m - m_new)
        p = jnp.exp2(s - m_new)
        part = p[:, 0:LANE]
        for k0 in range(LANE, kc, LANE):
            part = part + p[:, k0:k0 + LANE]
        l = alpha * l + part
        acc = alpha * acc + _dot(p.astype(BF16), v_ref[c0:c0 + kc, :])
        m = m_new
    o_ref[...] = (acc / jnp.sum(l, axis=-1, keepdims=True)).astype(o_ref.dtype)


def _attention(q, kt, v, n_heads, batch, seq, tq, kc):
    t = q.shape[0]
    qpt = seq // tq
    scale_log2e = math.log2(math.e) / math.sqrt(QK_HEAD_DIM)
    return pl.pallas_call(
        functools.partial(_attn_kernel, scale_log2e, kc),
        grid=(batch, n_heads, qpt),
        in_specs=[pl.BlockSpec((tq, HEAD_PAD), lambda b, hd, i: (b * qpt + i, hd)),
                  pl.BlockSpec((1, 1, HEAD_PAD, seq), lambda b, hd, i: (b, hd, 0, 0)),
                  pl.BlockSpec((seq, V_HEAD_DIM), lambda b, hd, i: (b, hd))],
        out_specs=pl.BlockSpec((tq, V_HEAD_DIM), lambda b, hd, i: (b * qpt + i, hd)),
        out_shape=jax.ShapeDtypeStruct((t, n_heads * V_HEAD_DIM), BF16),
        compiler_params=_cparams("parallel", "parallel", "arbitrary"),
        name="attention",
    )(q, kt, v)


def _branch_kernel(tiles_per_seq, cchunk, cb_ref, u_ref, up_ref, un_ref, yb_ref, cw_ref,
                   ga_ref, gb_ref, w0_ref, w1_ref, o_ref, ya_ref):
    i = pl.program_id(0)
    tm, c = u_ref.shape

    @pl.when(pl.program_id(1) == 0)
    def _():
        keep_prev = (i % tiles_per_seq != 0).astype(F32)
        keep_next = (i % tiles_per_seq != tiles_per_seq - 1).astype(F32)
        row = lax.broadcasted_iota(jnp.int32, (tm, 1), 0)
        for c0 in range(0, c, cchunk):
            cs = slice(c0, c0 + cchunk)
            u = u_ref[:, cs].astype(F32)
            prev_row = up_ref[HALO_ROWS - 1:HALO_ROWS, cs].astype(F32) * keep_prev
            next_row = un_ref[0:1, cs].astype(F32) * keep_next
            u_dn = jnp.where(row == 0, prev_row, pltpu.roll(u, 1, axis=0))
            u_up = jnp.where(row == tm - 1, next_row, pltpu.roll(u, tm - 1, axis=0))
            conv = u_dn * cw_ref[0:1, cs] + u * cw_ref[1:2, cs] + u_up * cw_ref[2:3, cs]
            ya_ref[:, cs] = (cb_ref[:, cs].astype(F32) * conv).astype(ya_ref.dtype)

    pa = _dot(ya_ref[...], w0_ref[0])
    pb = _dot(yb_ref[...], w1_ref[0])
    o_ref[...] = (ga_ref[...].astype(F32) * pa + gb_ref[...].astype(F32) * pb).astype(o_ref.dtype)


def _branch(cb, u, yb, conv_w, gates, w_br0, w_br1, seq, tm):
    t, c = cb.shape
    nb, _, tn = w_br0.shape
    d = nb * tn
    hb = tm // HALO_ROWS
    last_hb = t // HALO_ROWS - 1
    row = lambda i, j: (i, 0)
    return pl.pallas_call(
        functools.partial(_branch_kernel, seq // tm, min(c, 512)),
        grid=(t // tm, nb),
        in_specs=[pl.BlockSpec((tm, c), row),
                  pl.BlockSpec((tm, c), row),
                  pl.BlockSpec((HALO_ROWS, c), lambda i, j: (jnp.maximum(i * hb - 1, 0), 0)),
                  pl.BlockSpec((HALO_ROWS, c), lambda i, j: (jnp.minimum((i + 1) * hb, last_hb), 0)),
                  pl.BlockSpec((tm, c), row),
                  pl.BlockSpec(conv_w.shape, lambda i, j: (0, 0)),
                  pl.BlockSpec((tm, tn), lambda i, j: (i, j)),
                  pl.BlockSpec((tm, tn), lambda i, j: (i, j + nb)),
                  pl.BlockSpec((1, c, tn), lambda i, j: (j, 0, 0)),
                  pl.BlockSpec((1, c, tn), lambda i, j: (j, 0, 0))],
        out_specs=pl.BlockSpec((tm, tn), lambda i, j: (i, j)),
        out_shape=jax.ShapeDtypeStruct((t, d), BF16),
        scratch_shapes=[pltpu.VMEM((tm, c), BF16)],
        compiler_params=_cparams("parallel", "arbitrary"),
        name="branch",
    )(cb, u, u, u, yb, conv_w, gates, gates, w_br0, w_br1)


def _out_proj_kernel(m_ref, w_ref, x_ref, o_ref):
    o_ref[...] = x_ref[...] + _dot(m_ref[...], w_ref[0])


def _out_proj(m, w_out, x, tm):
    t, d = m.shape
    nb, _, tn = w_out.shape
    n = nb * tn
    return pl.pallas_call(
        _out_proj_kernel,
        grid=(t // tm, nb),
        in_specs=[pl.BlockSpec((tm, d), lambda i, j: (i, 0)),
                  pl.BlockSpec((1, d, tn), lambda i, j: (j, 0, 0)),
                  pl.BlockSpec((tm, tn), lambda i, j: (i, j))],
        out_specs=pl.BlockSpec((tm, tn), lambda i, j: (i, j)),
        out_shape=jax.ShapeDtypeStruct((t, n), F32),
        compiler_params=_cparams("parallel", "arbitrary"),
        name="out_proj",
    )(m, w_out, x)


def _ffn_kernel(final_norm, x_ref, gf_ref, wgu_ref, wd_ref, gl_ref, o_ref, h2_ref):
    j = pl.program_id(1)

    @pl.when(j == 0)
    def _():
        x = x_ref[...]
        h2_ref[...] = _rms(x, gf_ref[...]).astype(h2_ref.dtype)
        o_ref[...] = x

    tf = wd_ref.shape[0]
    z = _dot(h2_ref[...], wgu_ref[0])
    g = z[:, :tf]
    a = (g * (1.0 / (1.0 + jnp.exp(-g)))) * z[:, tf:]
    o_ref[...] += _dot(a.astype(BF16), wd_ref[...])

    if final_norm:
        @pl.when(j == pl.num_programs(1) - 1)
        def _():
            o_ref[...] = _rms(o_ref[...], gl_ref[...])


def _ffn_weights(w_g, w_u, w_d, tf):
    d, f = w_g.shape
    fp = -(-f // tf) * tf
    w_g = _pad_cols(w_g, fp).astype(BF16).reshape(d, fp // tf, tf)
    w_u = _pad_cols(w_u, fp).astype(BF16).reshape(d, fp // tf, tf)
    wgu = jnp.concatenate([w_g, w_u], axis=-1).transpose(1, 0, 2)
    wd = jnp.pad(w_d, ((0, fp - f), (0, 0))).astype(BF16)
    return wgu, wd


def _ffn(x1, g_ffn, wgu, wd, g_final, final_norm, tm):
    t, d = x1.shape
    nf, _, tf2 = wgu.shape
    tf = tf2 // 2
    return pl.pallas_call(
        functools.partial(_ffn_kernel, final_norm),
        grid=(t // tm, nf),
        in_specs=[_resident((tm, d), lambda i, j: (i, 0)),
                  pl.BlockSpec((1, d), lambda i, j: (0, 0)),
                  pl.BlockSpec((1, d, tf2), lambda i, j: (j, 0, 0)),
                  pl.BlockSpec((tf, d), lambda i, j: (j, 0)),
                  pl.BlockSpec((1, d), lambda i, j: (0, 0))],
        out_specs=_resident((tm, d), lambda i, j: (i, 0)),
        out_shape=jax.ShapeDtypeStruct((t, d), F32),
        scratch_shapes=[pltpu.VMEM((tm, d), BF16)],
        compiler_params=_cparams("parallel", "arbitrary"),
        name="ffn",
    )(x1, g_ffn, wgu, wd, g_final)


def _pad_cols(w, width):
    return jnp.pad(w, ((0, 0), (0, width - w.shape[1])))


def _tile_major(w, tn):
    k, n = w.shape
    return w.reshape(k, n // tn, tn).transpose(1, 0, 2)


def _swap_halves(w):
    half = w.shape[-1] // 2
    return jnp.concatenate([w[..., half:], w[..., :half]], axis=-1)


def _q_weights(w_q_b, n_heads):
    r = w_q_b.shape[0]
    w = w_q_b.reshape(r, n_heads, QK_HEAD_DIM)
    nope, rope = w[..., :QK_NOPE_DIM], w[..., QK_NOPE_DIM:]
    zeros = jnp.zeros((r, n_heads, HEAD_PAD - QK_HEAD_DIM), w.dtype)
    main = jnp.concatenate([nope, rope, zeros], axis=-1).reshape(r, n_heads * HEAD_PAD)
    swapped = jnp.concatenate([_swap_halves(rope), zeros], axis=-1).reshape(r, n_heads * LANE)
    return main.astype(BF16), swapped.astype(BF16)


def _pick(pref, n):
    if n <= pref:
        return n
    t = pref
    while n % t:
        t //= 2
    return t


def kernel(x, positions, g_mix, w_in, b_gate, conv_w, g_q_a, w_q_b, g_kv_a, w_kv_b, w_branch,
           w_out, g_ffn, w_ffn_gate, w_ffn_up, w_ffn_down, g_final):
    batch, seq, d = x.shape
    depth = w_in.shape[0]
    t = batch * seq
    conv_dim = conv_w.shape[-1]
    q_rank = g_q_a.shape[-1]
    kv_rank = g_kv_a.shape[-1]
    n_heads = w_q_b.shape[-1] // QK_HEAD_DIM
    qa_col = 3 * conv_dim
    kva_col = qa_col + q_rank
    kr_col = kva_col + kv_rank
    gate_col = kr_col + QK_ROPE_DIM

    xf = x.reshape(t, d)
    pos = positions.reshape(t, 1)
    inv_freq = ROPE_THETA ** (-jnp.arange(0, QK_ROPE_DIM, 2, dtype=F32) / QK_ROPE_DIM)
    zeros = jnp.zeros((LANE - QK_ROPE_DIM,), F32)
    invf = jnp.concatenate([inv_freq, inv_freq, zeros])[None, :]
    half = QK_ROPE_DIM // 2
    sgn = jnp.concatenate([-jnp.ones((half,), F32), jnp.ones((half,), F32), zeros])[None, :]

    for l in range(depth):
        w_in_l = w_in[l]
        tc = _pick(512, conv_dim)
        w_conv = w_in_l[:, :qa_col].astype(BF16).reshape(d, 3, conv_dim // tc, tc)
        w_conv = w_conv.transpose(2, 0, 1, 3).reshape(conv_dim // tc, d, 3 * tc)
        w_qa = w_in_l[:, qa_col:kva_col].astype(BF16)
        w_kva = w_in_l[:, kva_col:kr_col].astype(BF16)
        w_kr = w_in_l[:, kr_col:gate_col]
        wkr = _pad_cols(w_kr, LANE).astype(BF16)
        wkrs = _pad_cols(_swap_halves(w_kr), LANE).astype(BF16)
        w_gate = _tile_major(w_in_l[:, gate_col:].astype(BF16), _pick(1024, d))
        wq, wqs = _q_weights(w_q_b[l], n_heads)
        w_kv = w_kv_b[l].reshape(kv_rank, n_heads, QK_NOPE_DIM + V_HEAD_DIM)
        wkt = w_kv[..., :QK_NOPE_DIM].reshape(kv_rank, n_heads * QK_NOPE_DIM).T.astype(BF16)
        wv = w_kv[..., QK_NOPE_DIM:].reshape(kv_rank, n_heads * V_HEAD_DIM).astype(BF16)
        w_br0 = _tile_major(w_branch[l, 0].astype(BF16), _pick(512, d))
        w_br1 = _tile_major(w_branch[l, 1].astype(BF16), _pick(512, d))
        w_o = _tile_major(w_out[l].astype(BF16), _pick(1024, d))
        wgu, wd = _ffn_weights(w_ffn_gate[l], w_ffn_up[l], w_ffn_down[l], 512)

        h = _norm(xf, g_mix[l][None, :], _pick(256, t))
        cb, u = _conv_proj(h, w_conv, _pick(1024, t))
        gates = _gate_proj(h, w_gate, b_gate[l][None, :], _pick(1024, t))
        q = _q_proj(h, pos, invf, sgn, w_qa, g_q_a[l][None, :], wq, wqs, n_heads, _pick(512, seq))
        kt, v = _kv_proj(h, pos, invf, sgn, w_kva, g_kv_a[l][None, :], wkr, wkrs, wkt, wv,
                         n_heads, batch, seq, _pick(512, seq))
        yb = _attention(q, kt, v, n_heads, batch, seq, _pick(512, seq), _pick(512, seq))
        m = _branch(cb, u, yb, conv_w[l], gates, w_br0, w_br1, seq, _pick(1024, seq))
        xf = _out_proj(m, w_o, xf, _pick(1024, t))
        xf = _ffn(xf, g_ffn[l][None, :], wgu, wd, g_final[None, :], l == depth - 1, _pick(512, t))
    return xf.reshape(batch, seq, d)
```

```python
import functools
import math

import jax
import jax.numpy as jnp
from jax import lax
from jax.experimental import pallas as pl
from jax.experimental.pallas import tpu as pltpu

F32 = jnp.float32
BF16 = jnp.bfloat16

RMS_EPS = 1e-6
ROPE_THETA = 10000.0
QK_NOPE_DIM = 128
QK_ROPE_DIM = 64
V_HEAD_DIM = 128
QK_HEAD_DIM = QK_NOPE_DIM + QK_ROPE_DIM
Q_SCALE = math.log2(math.e) / math.sqrt(QK_HEAD_DIM)
HEAD_PAD = 256
LANE = 128
HALO_ROWS = 16
VMEM_LIMIT = 60 * 1024 * 1024


def _cparams(*sem):
    return pltpu.CompilerParams(dimension_semantics=sem, vmem_limit_bytes=VMEM_LIMIT)


def _resident(block_shape, index_map):
    return pl.BlockSpec(block_shape, index_map, pipeline_mode=pl.Buffered(1))


def _dot(a, b):
    return jnp.dot(a, b, preferred_element_type=F32)


def _rms(x, g):
    inv = lax.rsqrt(jnp.mean(x * x, axis=-1, keepdims=True) + RMS_EPS)
    return x * inv * g


def _norm_kernel(x_ref, g_ref, o_ref):
    o_ref[...] = _rms(x_ref[...], g_ref[...]).astype(o_ref.dtype)


def _norm(x, g, tm):
    t, d = x.shape
    return pl.pallas_call(
        _norm_kernel,
        grid=(t // tm,),
        in_specs=[pl.BlockSpec((tm, d), lambda i: (i, 0)),
                  pl.BlockSpec((1, d), lambda i: (0, 0))],
        out_specs=pl.BlockSpec((tm, d), lambda i: (i, 0)),
        out_shape=jax.ShapeDtypeStruct((t, d), BF16),
        compiler_params=_cparams("parallel"),
        name="norm",
    )(x, g)


def _conv_proj_kernel(h_ref, wb_ref, wc_ref, wh_ref, cb_ref, u_ref):
    h = h_ref[...]
    cb_ref[...] = _dot(h, wb_ref[...]).astype(cb_ref.dtype)
    u_ref[...] = (_dot(h, wc_ref[...]) * _dot(h, wh_ref[...])).astype(u_ref.dtype)


def _conv_proj(h, w_in, conv_dim, tm, tn):
    t, d = h.shape
    nb = conv_dim // tn
    w_spec = lambda off: pl.BlockSpec((d, tn), lambda i, j: (0, j + off))
    out = jax.ShapeDtypeStruct((t, conv_dim), BF16)
    return pl.pallas_call(
        _conv_proj_kernel,
        grid=(t // tm, nb),
        in_specs=[pl.BlockSpec((tm, d), lambda i, j: (i, 0)),
                  w_spec(0), w_spec(nb), w_spec(2 * nb)],
        out_specs=[pl.BlockSpec((tm, tn), lambda i, j: (i, j))] * 2,
        out_shape=[out, out],
        compiler_params=_cparams("parallel", "arbitrary"),
        name="conv_proj",
    )(h, w_in, w_in, w_in)


def _gate_proj_kernel(h_ref, w_ref, b_ref, o_ref):
    z = _dot(h_ref[...], w_ref[...]) + b_ref[...]
    o_ref[...] = (1.0 / (1.0 + jnp.exp(-z))).astype(o_ref.dtype)


def _gate_proj(h, w_gate, b_gate, tm, tn):
    t, d = h.shape
    n = w_gate.shape[1]
    return pl.pallas_call(
        _gate_proj_kernel,
        grid=(t // tm, n // tn),
        in_specs=[pl.BlockSpec((tm, d), lambda i, j: (i, 0)),
                  pl.BlockSpec((d, tn), lambda i, j: (0, j)),
                  pl.BlockSpec((1, tn), lambda i, j: (0, j))],
        out_specs=pl.BlockSpec((tm, tn), lambda i, j: (i, j)),
        out_shape=jax.ShapeDtypeStruct((t, n), BF16),
        compiler_params=_cparams("parallel", "arbitrary"),
        name="gate_proj",
    )(h, w_gate, b_gate)


def _rope_tables(pos_ref, invf_ref, sgn_ref):
    ang = pos_ref[...].astype(F32) * invf_ref[...]
    return jnp.cos(ang), jnp.sin(ang) * sgn_ref[...]


def _q_proj_kernel(n_heads, h_ref, pos_ref, invf_ref, sgn_ref, wqa_ref, gq_ref,
                   wq_ref, wqs_ref, q_ref):
    qn = _rms(_dot(h_ref[...], wqa_ref[...]), gq_ref[...]).astype(BF16)
    q = _dot(qn, wq_ref[...])
    qs = _dot(qn, wqs_ref[...])
    cos, sin = _rope_tables(pos_ref, invf_ref, sgn_ref)
    cos, sin = cos * Q_SCALE, sin * Q_SCALE
    for hd in range(n_heads):
        a = hd * HEAD_PAD
        q_ref[:, a:a + LANE] = (q[:, a:a + LANE] * Q_SCALE).astype(q_ref.dtype)
        rot = q[:, a + LANE:a + HEAD_PAD] * cos + qs[:, hd * LANE:(hd + 1) * LANE] * sin
        q_ref[:, a + LANE:a + HEAD_PAD] = rot.astype(q_ref.dtype)


def _q_proj(h, pos, invf, sgn, w_in, qa_col, g_q, wq, wqs, n_heads, tm):
    t, d = h.shape
    q_rank = g_q.shape[1]
    assert qa_col % q_rank == 0
    const = lambda i: (0, 0)
    return pl.pallas_call(
        functools.partial(_q_proj_kernel, n_heads),
        grid=(t // tm,),
        in_specs=[pl.BlockSpec((tm, d), lambda i: (i, 0)),
                  pl.BlockSpec((tm, 1), lambda i: (i, 0)),
                  _resident((1, LANE), const),
                  _resident((1, LANE), const),
                  _resident((d, q_rank), lambda i: (0, qa_col // q_rank)),
                  _resident(g_q.shape, const),
                  _resident(wq.shape, const),
                  _resident(wqs.shape, const)],
        out_specs=pl.BlockSpec((tm, n_heads * HEAD_PAD), lambda i: (i, 0)),
        out_shape=jax.ShapeDtypeStruct((t, n_heads * HEAD_PAD), BF16),
        compiler_params=_cparams("parallel"),
        name="q_proj",
    )(h, pos, invf, sgn, w_in, g_q, wq, wqs)


def _kv_proj_kernel(n_heads, h_ref, pos_ref, invf_ref, sgn_ref, wkva_ref, gkv_ref,
                    wkr_ref, wkrs_ref, wkt_ref, wv_ref, kt_ref, v_ref):
    h = h_ref[...]
    kvn = _rms(_dot(h, wkva_ref[...]), gkv_ref[...]).astype(BF16)
    v = _dot(kvn, wv_ref[...])
    lane = lax.broadcasted_iota(jnp.int32, (h.shape[0], LANE), 1)
    ones_col = jnp.where(lane == 0, 1.0, 0.0).astype(v_ref.dtype)
    for hd in range(n_heads):
        a = hd * HEAD_PAD
        v_ref[:, a:a + V_HEAD_DIM] = v[:, hd * V_HEAD_DIM:(hd + 1) * V_HEAD_DIM].astype(v_ref.dtype)
        v_ref[:, a + V_HEAD_DIM:a + HEAD_PAD] = ones_col
    knt = lax.dot_general(wkt_ref[...], kvn, (((1,), (1,)), ((), ())),
                          preferred_element_type=F32)
    cos, sin = _rope_tables(pos_ref, invf_ref, sgn_ref)
    krot = _dot(h, wkr_ref[...]) * cos + _dot(h, wkrs_ref[...]) * sin
    krt = krot.T.astype(kt_ref.dtype)
    for hd in range(n_heads):
        kt_ref[0, hd, 0:LANE, :] = knt[hd * LANE:(hd + 1) * LANE, :].astype(kt_ref.dtype)
        kt_ref[0, hd, LANE:HEAD_PAD, :] = krt


def _kv_proj(h, pos, invf, sgn, w_in, kva_col, g_kv, wkr, wkrs, wkt, wv, n_heads, batch, seq, tm):
    t, d = h.shape
    kv_rank = g_kv.shape[1]
    assert kva_col % kv_rank == 0
    spt = seq // tm
    const = lambda i: (0, 0)
    return pl.pallas_call(
        functools.partial(_kv_proj_kernel, n_heads),
        grid=(t // tm,),
        in_specs=[pl.BlockSpec((tm, d), lambda i: (i, 0)),
                  pl.BlockSpec((tm, 1), lambda i: (i, 0)),
                  _resident((1, LANE), const),
                  _resident((1, LANE), const),
                  _resident((d, kv_rank), lambda i: (0, kva_col // kv_rank)),
                  _resident(g_kv.shape, const),
                  _resident(wkr.shape, const),
                  _resident(wkrs.shape, const),
                  _resident(wkt.shape, const),
                  _resident(wv.shape, const)],
        out_specs=[pl.BlockSpec((1, n_heads, HEAD_PAD, tm), lambda i: (i // spt, 0, 0, i % spt)),
                   pl.BlockSpec((tm, n_heads * HEAD_PAD), lambda i: (i, 0))],
        out_shape=[jax.ShapeDtypeStruct((batch, n_heads, HEAD_PAD, seq), BF16),
                   jax.ShapeDtypeStruct((t, n_heads * HEAD_PAD), BF16)],
        compiler_params=_cparams("parallel"),
        name="kv_proj",
    )(h, pos, invf, sgn, w_in, g_kv, wkr, wkrs, wkt, wv)


def _attn_kernel(kc, q_ref, kt_ref, v_ref, o_ref):
    q = q_ref[...]
    tq = q.shape[0]
    seq = kt_ref.shape[-1]
    dv = o_ref.shape[-1]
    m = jnp.full((tq, 1), -jnp.inf, F32)
    acc = jnp.zeros((tq, v_ref.shape[-1]), F32)
    for c0 in range(0, seq, kc):
        s = _dot(q, kt_ref[0, 0, :, c0:c0 + kc])
        m_new = jnp.maximum(m, jnp.max(s, axis=-1, keepdims=True))
        alpha = jnp.exp2(m - m_new)
        p = jnp.exp2(s - m_new).astype(BF16)
        acc = alpha * acc + _dot(p, v_ref[c0:c0 + kc, :])
        m = m_new
    o_ref[...] = (acc[:, :dv] / acc[:, dv:dv + 1]).astype(o_ref.dtype)


def _attention(q, kt, v, n_heads, batch, seq, tq, kc):
    t = q.shape[0]
    qpt = seq // tq
    return pl.pallas_call(
        functools.partial(_attn_kernel, kc),
        grid=(batch, n_heads, qpt),
        in_specs=[pl.BlockSpec((tq, HEAD_PAD), lambda b, hd, i: (b * qpt + i, hd)),
                  pl.BlockSpec((1, 1, HEAD_PAD, seq), lambda b, hd, i: (b, hd, 0, 0)),
                  pl.BlockSpec((seq, HEAD_PAD), lambda b, hd, i: (b, hd))],
        out_specs=pl.BlockSpec((tq, V_HEAD_DIM), lambda b, hd, i: (b * qpt + i, hd)),
        out_shape=jax.ShapeDtypeStruct((t, n_heads * V_HEAD_DIM), BF16),
        compiler_params=_cparams("parallel", "parallel", "arbitrary"),
        name="attention",
    )(q, kt, v)


def _branch_kernel(tiles_per_seq, cchunk, cb_ref, u_ref, up_ref, un_ref, yb_ref, cw_ref,
                   ga_ref, gb_ref, w0_ref, w1_ref, o_ref, ya_ref):
    i = pl.program_id(0)
    tm, c = u_ref.shape

    @pl.when(pl.program_id(1) == 0)
    def _():
        keep_prev = (i % tiles_per_seq != 0).astype(F32)
        keep_next = (i % tiles_per_seq != tiles_per_seq - 1).astype(F32)
        row = lax.broadcasted_iota(jnp.int32, (tm, 1), 0)
        for c0 in range(0, c, cchunk):
            cs = slice(c0, c0 + cchunk)
            u = u_ref[:, cs].astype(F32)
            prev_row = up_ref[HALO_ROWS - 1:HALO_ROWS, cs].astype(F32) * keep_prev
            next_row = un_ref[0:1, cs].astype(F32) * keep_next
            u_dn = jnp.where(row == 0, prev_row, pltpu.roll(u, 1, axis=0))
            u_up = jnp.where(row == tm - 1, next_row, pltpu.roll(u, tm - 1, axis=0))
            conv = u_dn * cw_ref[0:1, cs] + u * cw_ref[1:2, cs] + u_up * cw_ref[2:3, cs]
            ya_ref[:, cs] = (cb_ref[:, cs].astype(F32) * conv).astype(ya_ref.dtype)

    pa = _dot(ya_ref[...], w0_ref[0])
    pb = _dot(yb_ref[...], w1_ref[0])
    o_ref[...] = (ga_ref[...].astype(F32) * pa + gb_ref[...].astype(F32) * pb).astype(o_ref.dtype)


def _branch(cb, u, yb, conv_w, gates, w_br, seq, tm, tn):
    t, c = cb.shape
    d = w_br.shape[2]
    nb = d // tn
    hb = tm // HALO_ROWS
    last_hb = t // HALO_ROWS - 1
    row = lambda i, j: (i, 0)
    return pl.pallas_call(
        functools.partial(_branch_kernel, seq // tm, min(c, 512)),
        grid=(t // tm, nb),
        in_specs=[pl.BlockSpec((tm, c), row),
                  pl.BlockSpec((tm, c), row),
                  pl.BlockSpec((HALO_ROWS, c), lambda i, j: (jnp.maximum(i * hb - 1, 0), 0)),
                  pl.BlockSpec((HALO_ROWS, c), lambda i, j: (jnp.minimum((i + 1) * hb, last_hb), 0)),
                  pl.BlockSpec((tm, c), row),
                  pl.BlockSpec(conv_w.shape, lambda i, j: (0, 0)),
                  pl.BlockSpec((tm, tn), lambda i, j: (i, j)),
                  pl.BlockSpec((tm, tn), lambda i, j: (i, j + nb)),
                  pl.BlockSpec((1, c, tn), lambda i, j: (0, 0, j)),
                  pl.BlockSpec((1, c, tn), lambda i, j: (1, 0, j))],
        out_specs=pl.BlockSpec((tm, tn), lambda i, j: (i, j)),
        out_shape=jax.ShapeDtypeStruct((t, d), BF16),
        scratch_shapes=[pltpu.VMEM((tm, c), BF16)],
        compiler_params=_cparams("parallel", "arbitrary"),
        name="branch",
    )(cb, u, u, u, yb, conv_w, gates, gates, w_br, w_br)


def _out_proj_kernel(m_ref, w_ref, x_ref, o_ref):
    o_ref[...] = x_ref[...] + _dot(m_ref[...], w_ref[...])


def _out_proj(m, w_out, x, tm, tn):
    t, d = m.shape
    n = w_out.shape[1]
    return pl.pallas_call(
        _out_proj_kernel,
        grid=(t // tm, n // tn),
        in_specs=[pl.BlockSpec((tm, d), lambda i, j: (i, 0)),
                  pl.BlockSpec((d, tn), lambda i, j: (0, j)),
                  pl.BlockSpec((tm, tn), lambda i, j: (i, j))],
        out_specs=pl.BlockSpec((tm, tn), lambda i, j: (i, j)),
        out_shape=jax.ShapeDtypeStruct((t, n), F32),
        compiler_params=_cparams("parallel", "arbitrary"),
        name="out_proj",
    )(m, w_out, x)


def _ffn_kernel(final_norm, x_ref, gf_ref, wg_ref, wu_ref, wd_ref, gl_ref, o_ref, h2_ref):
    j = pl.program_id(1)

    @pl.when(j == 0)
    def _():
        x = x_ref[...]
        h2_ref[...] = _rms(x, gf_ref[...]).astype(h2_ref.dtype)
        o_ref[...] = x

    h2 = h2_ref[...]
    g = _dot(h2, wg_ref[...])
    a = (g * (1.0 / (1.0 + jnp.exp(-g)))) * _dot(h2, wu_ref[...])
    o_ref[...] += _dot(a.astype(BF16), wd_ref[...])

    if final_norm:
        @pl.when(j == pl.num_programs(1) - 1)
        def _():
            o_ref[...] = _rms(o_ref[...], gl_ref[...])


def _ffn(x1, g_ffn, w_g, w_u, w_d, g_final, final_norm, tm, tf):
    t, d = x1.shape
    f = w_g.shape[1]
    assert f % tf == 0
    return pl.pallas_call(
        functools.partial(_ffn_kernel, final_norm),
        grid=(t // tm, f // tf),
        in_specs=[_resident((tm, d), lambda i, j: (i, 0)),
                  pl.BlockSpec((1, d), lambda i, j: (0, 0)),
                  pl.BlockSpec((d, tf), lambda i, j: (0, j)),
                  pl.BlockSpec((d, tf), lambda i, j: (0, j)),
                  pl.BlockSpec((tf, d), lambda i, j: (j, 0)),
                  pl.BlockSpec((1, d), lambda i, j: (0, 0))],
        out_specs=pl.BlockSpec((tm, d), lambda i, j: (i, 0)),
        out_shape=jax.ShapeDtypeStruct((t, d), F32),
        scratch_shapes=[pltpu.VMEM((tm, d), BF16)],
        compiler_params=_cparams("parallel", "arbitrary"),
        name="ffn",
    )(x1, g_ffn, w_g, w_u, w_d, g_final)


def _pad_cols(w, width):
    return jnp.pad(w, ((0, 0), (0, width - w.shape[1])))


def _swap_halves(w):
    half = w.shape[-1] // 2
    return jnp.concatenate([w[..., half:], w[..., :half]], axis=-1)


def _q_weights(w_q_b, n_heads):
    r = w_q_b.shape[0]
    w = w_q_b.reshape(r, n_heads, QK_HEAD_DIM)
    nope, rope = w[..., :QK_NOPE_DIM], w[..., QK_NOPE_DIM:]
    zeros = jnp.zeros((r, n_heads, HEAD_PAD - QK_HEAD_DIM), w.dtype)
    main = jnp.concatenate([nope, rope, zeros], axis=-1).reshape(r, n_heads * HEAD_PAD)
    swapped = jnp.concatenate([_swap_halves(rope), zeros], axis=-1).reshape(r, n_heads * LANE)
    return main.astype(BF16), swapped.astype(BF16)


def _pick(pref, n):
    if n <= pref:
        return n
    t = pref
    while n % t:
        t //= 2
    return t


def kernel(x, positions, g_mix, w_in, b_gate, conv_w, g_q_a, w_q_b, g_kv_a, w_kv_b, w_branch,
           w_out, g_ffn, w_ffn_gate, w_ffn_up, w_ffn_down, g_final):
    batch, seq, d = x.shape
    depth = w_in.shape[0]
    t = batch * seq
    conv_dim = conv_w.shape[-1]
    q_rank = g_q_a.shape[-1]
    kv_rank = g_kv_a.shape[-1]
    n_heads = w_q_b.shape[-1] // QK_HEAD_DIM
    qa_col = 3 * conv_dim
    kva_col = qa_col + q_rank
    kr_col = kva_col + kv_rank
    gate_col = kr_col + QK_ROPE_DIM

    xf = x.reshape(t, d)
    pos = positions.reshape(t, 1)
    inv_freq = ROPE_THETA ** (-jnp.arange(0, QK_ROPE_DIM, 2, dtype=F32) / QK_ROPE_DIM)
    zeros = jnp.zeros((LANE - QK_ROPE_DIM,), F32)
    invf = jnp.concatenate([inv_freq, inv_freq, zeros])[None, :]
    half = QK_ROPE_DIM // 2
    sgn = jnp.concatenate([-jnp.ones((half,), F32), jnp.ones((half,), F32), zeros])[None, :]

    for l in range(depth):
        w_in_l = w_in[l]
        w_lat = w_in_l[:, :kr_col].astype(BF16)
        w_kr = w_in_l[:, kr_col:gate_col]
        wkr = _pad_cols(w_kr, LANE).astype(BF16)
        wkrs = _pad_cols(_swap_halves(w_kr), LANE).astype(BF16)
        w_gate = w_in_l[:, gate_col:].astype(BF16)
        wq, wqs = _q_weights(w_q_b[l], n_heads)
        w_kv = w_kv_b[l].reshape(kv_rank, n_heads, QK_NOPE_DIM + V_HEAD_DIM)
        wkt = w_kv[..., :QK_NOPE_DIM].reshape(kv_rank, n_heads * QK_NOPE_DIM).T.astype(BF16)
        wv = w_kv[..., QK_NOPE_DIM:].reshape(kv_rank, n_heads * V_HEAD_DIM).astype(BF16)
        w_br = w_branch[l].astype(BF16)
        w_o = w_out[l].astype(BF16)
        w_fg = w_ffn_gate[l].astype(BF16)
        w_fu = w_ffn_up[l].astype(BF16)
        w_fd = w_ffn_down[l].astype(BF16)

        h = _norm(xf, g_mix[l][None, :], _pick(256, t))
        cb, u = _conv_proj(h, w_lat, conv_dim, _pick(1024, t), _pick(512, conv_dim))
        gates = _gate_proj(h, w_gate, b_gate[l][None, :], _pick(1024, t), _pick(1024, d))
        q = _q_proj(h, pos, invf, sgn, w_lat, qa_col, g_q_a[l][None, :], wq, wqs, n_heads,
                    _pick(512, seq))
        kt, v = _kv_proj(h, pos, invf, sgn, w_lat, kva_col, g_kv_a[l][None, :], wkr, wkrs, wkt, wv,
                         n_heads, batch, seq, _pick(512, seq))
        yb = _attention(q, kt, v, n_heads, batch, seq, _pick(1024, seq), _pick(512, seq))
        m = _branch(cb, u, yb, conv_w[l], gates, w_br, seq, _pick(1024, seq), _pick(512, d))
        xf = _out_proj(m, w_o, xf, _pick(1024, t), _pick(1024, d))
        xf = _ffn(xf, g_ffn[l][None, :], w_fg, w_fu, w_fd, g_final[None, :], l == depth - 1,
                  _pick(512, t), _pick(256, w_fg.shape[1]))
    return xf.reshape(batch, seq, d)
```

```python
import functools
import math

import jax
import jax.numpy as jnp
from jax import lax
from jax.experimental import pallas as pl
from jax.experimental.pallas import tpu as pltpu

F32 = jnp.float32
BF16 = jnp.bfloat16

RMS_EPS = 1e-6
ROPE_THETA = 10000.0
QK_NOPE_DIM = 128
QK_ROPE_DIM = 64
V_HEAD_DIM = 128
QK_HEAD_DIM = QK_NOPE_DIM + QK_ROPE_DIM
Q_SCALE = math.log2(math.e) / math.sqrt(QK_HEAD_DIM)
HEAD_PAD = 256
LANE = 128
HALO_ROWS = 16
VMEM_LIMIT = 60 * 1024 * 1024


def _cparams(*sem):
    return pltpu.CompilerParams(dimension_semantics=sem, vmem_limit_bytes=VMEM_LIMIT)


def _resident(block_shape, index_map):
    return pl.BlockSpec(block_shape, index_map, pipeline_mode=pl.Buffered(1))


def _dot(a, b):
    return jnp.dot(a, b, preferred_element_type=F32)


def _rms(x, g):
    inv = lax.rsqrt(jnp.mean(x * x, axis=-1, keepdims=True) + RMS_EPS)
    return x * inv * g


def _conv_proj_kernel(h_ref, wb_ref, wc_ref, wh_ref, cb_ref, u_ref):
    h = h_ref[...]
    cb_ref[...] = _dot(h, wb_ref[...]).astype(cb_ref.dtype)
    u_ref[...] = (_dot(h, wc_ref[...]) * _dot(h, wh_ref[...])).astype(u_ref.dtype)


def _conv_proj(h, w_in, conv_dim, tm, tn):
    t, d = h.shape
    nb = conv_dim // tn
    w_spec = lambda off: pl.BlockSpec((d, tn), lambda i, j: (0, j + off))
    out = jax.ShapeDtypeStruct((t, conv_dim), BF16)
    return pl.pallas_call(
        _conv_proj_kernel,
        grid=(t // tm, nb),
        in_specs=[pl.BlockSpec((tm, d), lambda i, j: (i, 0)),
                  w_spec(0), w_spec(nb), w_spec(2 * nb)],
        out_specs=[pl.BlockSpec((tm, tn), lambda i, j: (i, j))] * 2,
        out_shape=[out, out],
        compiler_params=_cparams("parallel", "arbitrary"),
        name="conv_proj",
    )(h, w_in, w_in, w_in)


def _gate_proj_kernel(h_ref, w_ref, b_ref, o_ref):
    z = _dot(h_ref[...], w_ref[...]) + b_ref[...]
    o_ref[...] = (1.0 / (1.0 + jnp.exp(-z))).astype(o_ref.dtype)


def _gate_proj(h, w_gate, b_gate, tm, tn):
    t, d = h.shape
    n = w_gate.shape[1]
    return pl.pallas_call(
        _gate_proj_kernel,
        grid=(t // tm, n // tn),
        in_specs=[pl.BlockSpec((tm, d), lambda i, j: (i, 0)),
                  pl.BlockSpec((d, tn), lambda i, j: (0, j)),
                  pl.BlockSpec((1, tn), lambda i, j: (0, j))],
        out_specs=pl.BlockSpec((tm, tn), lambda i, j: (i, j)),
        out_shape=jax.ShapeDtypeStruct((t, n), BF16),
        compiler_params=_cparams("parallel", "arbitrary"),
        name="gate_proj",
    )(h, w_gate, b_gate)


def _rope_tables(pos_ref, invf_ref, sgn_ref):
    ang = pos_ref[...].astype(F32) * invf_ref[...]
    return jnp.cos(ang), jnp.sin(ang) * sgn_ref[...]


def _q_proj_kernel(n_heads, h_ref, pos_ref, invf_ref, sgn_ref, wqa_ref, gq_ref,
                   wq_ref, wqs_ref, q_ref):
    qn = _rms(_dot(h_ref[...], wqa_ref[...]), gq_ref[...]).astype(BF16)
    q = _dot(qn, wq_ref[...])
    qs = _dot(qn, wqs_ref[...])
    cos, sin = _rope_tables(pos_ref, invf_ref, sgn_ref)
    cos, sin = cos * Q_SCALE, sin * Q_SCALE
    for hd in range(n_heads):
        a = hd * HEAD_PAD
        q_ref[:, a:a + LANE] = (q[:, a:a + LANE] * Q_SCALE).astype(q_ref.dtype)
        rot = q[:, a + LANE:a + HEAD_PAD] * cos + qs[:, hd * LANE:(hd + 1) * LANE] * sin
        q_ref[:, a + LANE:a + HEAD_PAD] = rot.astype(q_ref.dtype)


def _q_proj(h, pos, invf, sgn, w_in, qa_col, g_q, wq, wqs, n_heads, tm):
    t, d = h.shape
    q_rank = g_q.shape[1]
    assert qa_col % q_rank == 0
    const = lambda i: (0, 0)
    return pl.pallas_call(
        functools.partial(_q_proj_kernel, n_heads),
        grid=(t // tm,),
        in_specs=[pl.BlockSpec((tm, d), lambda i: (i, 0)),
                  pl.BlockSpec((tm, 1), lambda i: (i, 0)),
                  _resident((1, LANE), const),
                  _resident((1, LANE), const),
                  _resident((d, q_rank), lambda i: (0, qa_col // q_rank)),
                  _resident(g_q.shape, const),
                  _resident(wq.shape, const),
                  _resident(wqs.shape, const)],
        out_specs=pl.BlockSpec((tm, n_heads * HEAD_PAD), lambda i: (i, 0)),
        out_shape=jax.ShapeDtypeStruct((t, n_heads * HEAD_PAD), BF16),
        compiler_params=_cparams("parallel"),
        name="q_proj",
    )(h, pos, invf, sgn, w_in, g_q, wq, wqs)


def _kv_proj_kernel(n_heads, x_ref, gm_ref, pos_ref, invf_ref, sgn_ref, wkvx_ref, gkv_ref,
                    wkt_ref, wv_ref, h_ref, kt_ref, v_ref):
    h = _rms(x_ref[...], gm_ref[...]).astype(h_ref.dtype)
    h_ref[...] = h
    kv_rank = gkv_ref.shape[1]
    z = _dot(h, wkvx_ref[...])
    kvn = _rms(z[:, :kv_rank], gkv_ref[...]).astype(BF16)
    v = _dot(kvn, wv_ref[...])
    lane = lax.broadcasted_iota(jnp.int32, (h.shape[0], LANE), 1)
    ones_col = jnp.where(lane == 0, 1.0, 0.0).astype(v_ref.dtype)
    for hd in range(n_heads):
        a = hd * HEAD_PAD
        v_ref[:, a:a + V_HEAD_DIM] = v[:, hd * V_HEAD_DIM:(hd + 1) * V_HEAD_DIM].astype(v_ref.dtype)
        v_ref[:, a + V_HEAD_DIM:a + HEAD_PAD] = ones_col
    knt = lax.dot_general(wkt_ref[...], kvn, (((1,), (1,)), ((), ())),
                          preferred_element_type=F32)
    cos, sin = _rope_tables(pos_ref, invf_ref, sgn_ref)
    krot = z[:, kv_rank:kv_rank + LANE] * cos + z[:, kv_rank + LANE:] * sin
    krt = krot.T.astype(kt_ref.dtype)
    for hd in range(n_heads):
        kt_ref[0, hd, 0:LANE, :] = knt[hd * LANE:(hd + 1) * LANE, :].astype(kt_ref.dtype)
        kt_ref[0, hd, LANE:HEAD_PAD, :] = krt


def _kv_proj(x, g_mix, pos, invf, sgn, wkvx, g_kv, wkt, wv, n_heads, batch, seq, tm):
    t, d = x.shape
    spt = seq // tm
    const = lambda i: (0, 0)
    return pl.pallas_call(
        functools.partial(_kv_proj_kernel, n_heads),
        grid=(t // tm,),
        in_specs=[pl.BlockSpec((tm, d), lambda i: (i, 0)),
                  _resident(g_mix.shape, const),
                  pl.BlockSpec((tm, 1), lambda i: (i, 0)),
                  _resident((1, LANE), const),
                  _resident((1, LANE), const),
                  _resident(wkvx.shape, const),
                  _resident(g_kv.shape, const),
                  _resident(wkt.shape, const),
                  _resident(wv.shape, const)],
        out_specs=[pl.BlockSpec((tm, d), lambda i: (i, 0)),
                   pl.BlockSpec((1, n_heads, HEAD_PAD, tm), lambda i: (i // spt, 0, 0, i % spt)),
                   pl.BlockSpec((tm, n_heads * HEAD_PAD), lambda i: (i, 0))],
        out_shape=[jax.ShapeDtypeStruct((t, d), BF16),
                   jax.ShapeDtypeStruct((batch, n_heads, HEAD_PAD, seq), BF16),
                   jax.ShapeDtypeStruct((t, n_heads * HEAD_PAD), BF16)],
        compiler_params=_cparams("parallel"),
        name="kv_proj",
    )(x, g_mix, pos, invf, sgn, wkvx, g_kv, wkt, wv)


def _attn_kernel(kc, q_ref, kt_ref, v_ref, o_ref):
    q = q_ref[...]
    tq = q.shape[0]
    seq = kt_ref.shape[-1]
    dv = o_ref.shape[-1]
    m = jnp.full((tq, 1), -jnp.inf, F32)
    acc = jnp.zeros((tq, v_ref.shape[-1]), F32)
    for c0 in range(0, seq, kc):
        s = _dot(q, kt_ref[0, 0, :, c0:c0 + kc])
        m_new = jnp.maximum(m, jnp.max(s, axis=-1, keepdims=True))
        alpha = jnp.exp2(m - m_new)
        p = jnp.exp2(s - m_new).astype(BF16)
        acc = alpha * acc + _dot(p, v_ref[c0:c0 + kc, :])
        m = m_new
    o_ref[...] = (acc[:, :dv] / acc[:, dv:dv + 1]).astype(o_ref.dtype)


def _attention(q, kt, v, n_heads, batch, seq, tq, kc):
    t = q.shape[0]
    qpt = seq // tq
    return pl.pallas_call(
        functools.partial(_attn_kernel, kc),
        grid=(batch, n_heads, qpt),
        in_specs=[pl.BlockSpec((tq, HEAD_PAD), lambda b, hd, i: (b * qpt + i, hd)),
                  pl.BlockSpec((1, 1, HEAD_PAD, seq), lambda b, hd, i: (b, hd, 0, 0)),
                  pl.BlockSpec((seq, HEAD_PAD), lambda b, hd, i: (b, hd))],
        out_specs=pl.BlockSpec((tq, V_HEAD_DIM), lambda b, hd, i: (b * qpt + i, hd)),
        out_shape=jax.ShapeDtypeStruct((t, n_heads * V_HEAD_DIM), BF16),
        compiler_params=_cparams("parallel", "parallel", "arbitrary"),
        name="attention",
    )(q, kt, v)


def _branch_kernel(tiles_per_seq, cchunk, cb_ref, u_ref, up_ref, un_ref, yb_ref, cw_ref,
                   ga_ref, gb_ref, w0_ref, w1_ref, o_ref, ya_ref):
    i = pl.program_id(0)
    tm, c = u_ref.shape

    @pl.when(pl.program_id(1) == 0)
    def _():
        keep_prev = (i % tiles_per_seq != 0).astype(F32)
        keep_next = (i % tiles_per_seq != tiles_per_seq - 1).astype(F32)
        row = lax.broadcasted_iota(jnp.int32, (tm, 1), 0)
        for c0 in range(0, c, cchunk):
            cs = slice(c0, c0 + cchunk)
            u = u_ref[:, cs].astype(F32)
            prev_row = up_ref[HALO_ROWS - 1:HALO_ROWS, cs].astype(F32) * keep_prev
            next_row = un_ref[0:1, cs].astype(F32) * keep_next
            u_dn = jnp.where(row == 0, prev_row, pltpu.roll(u, 1, axis=0))
            u_up = jnp.where(row == tm - 1, next_row, pltpu.roll(u, tm - 1, axis=0))
            conv = u_dn * cw_ref[0:1, cs] + u * cw_ref[1:2, cs] + u_up * cw_ref[2:3, cs]
            ya_ref[:, cs] = (cb_ref[:, cs].astype(F32) * conv).astype(ya_ref.dtype)

    pa = _dot(ya_ref[...], w0_ref[0])
    pb = _dot(yb_ref[...], w1_ref[0])
    o_ref[...] = (ga_ref[...].astype(F32) * pa + gb_ref[...].astype(F32) * pb).astype(o_ref.dtype)


def _branch(cb, u, yb, conv_w, gates, w_br, seq, tm, tn):
    t, c = cb.shape
    d = w_br.shape[2]
    nb = d // tn
    hb = tm // HALO_ROWS
    last_hb = t // HALO_ROWS - 1
    row = lambda i, j: (i, 0)
    return pl.pallas_call(
        functools.partial(_branch_kernel, seq // tm, min(c, 512)),
        grid=(t // tm, nb),
        in_specs=[pl.BlockSpec((tm, c), row),
                  pl.BlockSpec((tm, c), row),
                  pl.BlockSpec((HALO_ROWS, c), lambda i, j: (jnp.maximum(i * hb - 1, 0), 0)),
                  pl.BlockSpec((HALO_ROWS, c), lambda i, j: (jnp.minimum((i + 1) * hb, last_hb), 0)),
                  pl.BlockSpec((tm, c), row),
                  pl.BlockSpec(conv_w.shape, lambda i, j: (0, 0)),
                  pl.BlockSpec((tm, tn), lambda i, j: (i, j)),
                  pl.BlockSpec((tm, tn), lambda i, j: (i, j + nb)),
                  pl.BlockSpec((1, c, tn), lambda i, j: (0, 0, j)),
                  pl.BlockSpec((1, c, tn), lambda i, j: (1, 0, j))],
        out_specs=pl.BlockSpec((tm, tn), lambda i, j: (i, j)),
        out_shape=jax.ShapeDtypeStruct((t, d), BF16),
        scratch_shapes=[pltpu.VMEM((tm, c), BF16)],
        compiler_params=_cparams("parallel", "arbitrary"),
        name="branch",
    )(cb, u, u, u, yb, conv_w, gates, gates, w_br, w_br)


def _out_proj_kernel(m_ref, w_ref, x_ref, o_ref):
    o_ref[...] = x_ref[...] + _dot(m_ref[...], w_ref[...])


def _out_proj(m, w_out, x, tm, tn):
    t, d = m.shape
    n = w_out.shape[1]
    return pl.pallas_call(
        _out_proj_kernel,
        grid=(t // tm, n // tn),
        in_specs=[pl.BlockSpec((tm, d), lambda i, j: (i, 0)),
                  pl.BlockSpec((d, tn), lambda i, j: (0, j)),
                  pl.BlockSpec((tm, tn), lambda i, j: (i, j))],
        out_specs=pl.BlockSpec((tm, tn), lambda i, j: (i, j)),
        out_shape=jax.ShapeDtypeStruct((t, n), F32),
        compiler_params=_cparams("parallel", "arbitrary"),
        name="out_proj",
    )(m, w_out, x)


def _ffn_kernel(final_norm, x_ref, gf_ref, wg_ref, wu_ref, wd_ref, gl_ref, o_ref, h2_ref):
    j = pl.program_id(1)

    @pl.when(j == 0)
    def _():
        x = x_ref[...]
        h2_ref[...] = _rms(x, gf_ref[...]).astype(h2_ref.dtype)
        o_ref[...] = x

    h2 = h2_ref[...]
    g = _dot(h2, wg_ref[...])
    a = (g * (1.0 / (1.0 + jnp.exp(-g)))) * _dot(h2, wu_ref[...])
    o_ref[...] += _dot(a.astype(BF16), wd_ref[...])

    if final_norm:
        @pl.when(j == pl.num_programs(1) - 1)
        def _():
            o_ref[...] = _rms(o_ref[...], gl_ref[...])


def _ffn(x1, g_ffn, w_g, w_u, w_d, g_final, final_norm, tm, tf):
    t, d = x1.shape
    f = w_g.shape[1]
    assert f % tf == 0
    return pl.pallas_call(
        functools.partial(_ffn_kernel, final_norm),
        grid=(t // tm, f // tf),
        in_specs=[pl.BlockSpec((tm, d), lambda i, j: (i, 0)),
                  pl.BlockSpec((1, d), lambda i, j: (0, 0)),
                  pl.BlockSpec((d, tf), lambda i, j: (0, j)),
                  pl.BlockSpec((d, tf), lambda i, j: (0, j)),
                  pl.BlockSpec((tf, d), lambda i, j: (j, 0)),
                  pl.BlockSpec((1, d), lambda i, j: (0, 0))],
        out_specs=pl.BlockSpec((tm, d), lambda i, j: (i, 0)),
        out_shape=jax.ShapeDtypeStruct((t, d), F32),
        scratch_shapes=[pltpu.VMEM((tm, d), BF16)],
        compiler_params=_cparams("parallel", "arbitrary"),
        name="ffn",
    )(x1, g_ffn, w_g, w_u, w_d, g_final)


def _pad_cols(w, width):
    return jnp.pad(w, ((0, 0), (0, width - w.shape[1])))


def _swap_halves(w):
    half = w.shape[-1] // 2
    return jnp.concatenate([w[..., half:], w[..., :half]], axis=-1)


def _q_weights(w_q_b, n_heads):
    r = w_q_b.shape[0]
    w = w_q_b.reshape(r, n_heads, QK_HEAD_DIM)
    nope, rope = w[..., :QK_NOPE_DIM], w[..., QK_NOPE_DIM:]
    zeros = jnp.zeros((r, n_heads, HEAD_PAD - QK_HEAD_DIM), w.dtype)
    main = jnp.concatenate([nope, rope, zeros], axis=-1).reshape(r, n_heads * HEAD_PAD)
    swapped = jnp.concatenate([_swap_halves(rope), zeros], axis=-1).reshape(r, n_heads * LANE)
    return main.astype(BF16), swapped.astype(BF16)


def _pick(pref, n):
    if n <= pref:
        return n
    t = pref
    while n % t:
        t //= 2
    return t


def kernel(x, positions, g_mix, w_in, b_gate, conv_w, g_q_a, w_q_b, g_kv_a, w_kv_b, w_branch,
           w_out, g_ffn, w_ffn_gate, w_ffn_up, w_ffn_down, g_final):
    batch, seq, d = x.shape
    depth = w_in.shape[0]
    t = batch * seq
    conv_dim = conv_w.shape[-1]
    q_rank = g_q_a.shape[-1]
    kv_rank = g_kv_a.shape[-1]
    n_heads = w_q_b.shape[-1] // QK_HEAD_DIM
    qa_col = 3 * conv_dim
    kva_col = qa_col + q_rank
    kr_col = kva_col + kv_rank
    gate_col = kr_col + QK_ROPE_DIM

    xf = x.reshape(t, d)
    pos = positions.reshape(t, 1)
    inv_freq = ROPE_THETA ** (-jnp.arange(0, QK_ROPE_DIM, 2, dtype=F32) / QK_ROPE_DIM)
    zeros = jnp.zeros((LANE - QK_ROPE_DIM,), F32)
    invf = jnp.concatenate([inv_freq, inv_freq, zeros])[None, :]
    half = QK_ROPE_DIM // 2
    sgn = jnp.concatenate([-jnp.ones((half,), F32), jnp.ones((half,), F32), zeros])[None, :]

    for l in range(depth):
        w_in_l = w_in[l]
        w_lat = w_in_l[:, :kva_col].astype(BF16)
        w_kr = w_in_l[:, kr_col:gate_col]
        wkvx = jnp.concatenate([w_in_l[:, kva_col:kr_col], _pad_cols(w_kr, LANE),
                                _pad_cols(_swap_halves(w_kr), LANE)], axis=1).astype(BF16)
        w_gate = w_in_l[:, gate_col:].astype(BF16)
        wq, wqs = _q_weights(w_q_b[l], n_heads)
        w_kv = w_kv_b[l].reshape(kv_rank, n_heads, QK_NOPE_DIM + V_HEAD_DIM)
        wkt = w_kv[..., :QK_NOPE_DIM].reshape(kv_rank, n_heads * QK_NOPE_DIM).T.astype(BF16)
        wv = w_kv[..., QK_NOPE_DIM:].reshape(kv_rank, n_heads * V_HEAD_DIM).astype(BF16)
        w_br = w_branch[l].astype(BF16)
        w_o = w_out[l].astype(BF16)
        w_fg = w_ffn_gate[l].astype(BF16)
        w_fu = w_ffn_up[l].astype(BF16)
        w_fd = w_ffn_down[l].astype(BF16)

        h, kt, v = _kv_proj(xf, g_mix[l][None, :], pos, invf, sgn, wkvx, g_kv_a[l][None, :], wkt, wv,
                            n_heads, batch, seq, _pick(256, seq))
        cb, u = _conv_proj(h, w_lat, conv_dim, _pick(1024, t), _pick(512, conv_dim))
        gates = _gate_proj(h, w_gate, b_gate[l][None, :], _pick(1024, t), _pick(1024, d))
        q = _q_proj(h, pos, invf, sgn, w_lat, qa_col, g_q_a[l][None, :], wq, wqs, n_heads,
                    _pick(512, seq))
        yb = _attention(q, kt, v, n_heads, batch, seq, _pick(1024, seq), _pick(512, seq))
        m = _branch(cb, u, yb, conv_w[l], gates, w_br, seq, _pick(1024, seq), _pick(512, d))
        xf = _out_proj(m, w_o, xf, _pick(1024, t), _pick(1024, d))
        xf = _ffn(xf, g_ffn[l][None, :], w_fg, w_fu, w_fd, g_final[None, :], l == depth - 1,
                  _pick(512, t), _pick(256, w_fg.shape[1]))
    return xf.reshape(batch, seq, d)
```

```python
import functools
import math

import jax
import jax.numpy as jnp
from jax import lax
from jax.experimental import pallas as pl
from jax.experimental.pallas import tpu as pltpu

F32 = jnp.float32
BF16 = jnp.bfloat16

RMS_EPS = 1e-6
ROPE_THETA = 10000.0
QK_NOPE_DIM = 128
QK_ROPE_DIM = 64
V_HEAD_DIM = 128
QK_HEAD_DIM = QK_NOPE_DIM + QK_ROPE_DIM
Q_SCALE = math.log2(math.e) / math.sqrt(QK_HEAD_DIM)
HEAD_PAD = 256
LANE = 128
HALO_ROWS = 16
VMEM_LIMIT = 60 * 1024 * 1024


def _cparams(*sem):
    return pltpu.CompilerParams(dimension_semantics=sem, vmem_limit_bytes=VMEM_LIMIT)


def _resident(block_shape, index_map):
    return pl.BlockSpec(block_shape, index_map, pipeline_mode=pl.Buffered(1))


def _dot(a, b):
    return jnp.dot(a, b, preferred_element_type=F32)


def _rms(x, g):
    inv = lax.rsqrt(jnp.mean(x * x, axis=-1, keepdims=True) + RMS_EPS)
    return x * inv * g


def _conv_proj_kernel(h_ref, wb_ref, wc_ref, wh_ref, cb_ref, u_ref):
    h = h_ref[...]
    cb_ref[...] = _dot(h, wb_ref[...]).astype(cb_ref.dtype)
    u_ref[...] = (_dot(h, wc_ref[...]) * _dot(h, wh_ref[...])).astype(u_ref.dtype)


def _conv_proj(h, w_in, conv_dim, tm, tn):
    t, d = h.shape
    nb = conv_dim // tn
    w_spec = lambda off: pl.BlockSpec((d, tn), lambda i, j: (0, j + off))
    out = jax.ShapeDtypeStruct((t, conv_dim), BF16)
    return pl.pallas_call(
        _conv_proj_kernel,
        grid=(t // tm, nb),
        in_specs=[pl.BlockSpec((tm, d), lambda i, j: (i, 0)),
                  w_spec(0), w_spec(nb), w_spec(2 * nb)],
        out_specs=[pl.BlockSpec((tm, tn), lambda i, j: (i, j))] * 2,
        out_shape=[out, out],
        compiler_params=_cparams("parallel", "arbitrary"),
        name="conv_proj",
    )(h, w_in, w_in, w_in)


def _gate_proj_kernel(h_ref, w_ref, b_ref, o_ref):
    z = _dot(h_ref[...], w_ref[...]) + b_ref[...]
    o_ref[...] = (1.0 / (1.0 + jnp.exp(-z))).astype(o_ref.dtype)


def _gate_proj(h, w_gate, b_gate, tm, tn):
    t, d = h.shape
    n = w_gate.shape[1]
    return pl.pallas_call(
        _gate_proj_kernel,
        grid=(t // tm, n // tn),
        in_specs=[pl.BlockSpec((tm, d), lambda i, j: (i, 0)),
                  pl.BlockSpec((d, tn), lambda i, j: (0, j)),
                  pl.BlockSpec((1, tn), lambda i, j: (0, j))],
        out_specs=pl.BlockSpec((tm, tn), lambda i, j: (i, j)),
        out_shape=jax.ShapeDtypeStruct((t, n), BF16),
        compiler_params=_cparams("parallel", "arbitrary"),
        name="gate_proj",
    )(h, w_gate, b_gate)


def _rope_tables(pos_ref, invf_ref, sgn_ref):
    ang = pos_ref[...].astype(F32) * invf_ref[...]
    return jnp.cos(ang), jnp.sin(ang) * sgn_ref[...]


def _q_proj_kernel(n_heads, h_ref, pos_ref, invf_ref, sgn_ref, wqa_ref, gq_ref, wq_ref, q_ref):
    qn = _rms(_dot(h_ref[...], wqa_ref[...]), gq_ref[...]).astype(BF16)
    z = _dot(qn, wq_ref[...])
    rope0 = n_heads * QK_NOPE_DIM
    swap0 = rope0 + n_heads * QK_ROPE_DIM
    cos, sin = _rope_tables(pos_ref, invf_ref, sgn_ref)
    cos, sin = cos * Q_SCALE, sin * Q_SCALE
    first = lax.broadcasted_iota(jnp.int32, (1, LANE), 1) < QK_ROPE_DIM
    for pair in range(n_heads // 2):
        g = pair * LANE
        rot = z[:, rope0 + g:rope0 + g + LANE] * cos + z[:, swap0 + g:swap0 + g + LANE] * sin
        for k, r in enumerate((rot, pltpu.roll(rot, QK_ROPE_DIM, axis=1))):
            hd = 2 * pair + k
            a = hd * HEAD_PAD
            nope = z[:, hd * QK_NOPE_DIM:(hd + 1) * QK_NOPE_DIM] * Q_SCALE
            q_ref[:, a:a + LANE] = nope.astype(q_ref.dtype)
            q_ref[:, a + LANE:a + HEAD_PAD] = jnp.where(first, r, 0.0).astype(q_ref.dtype)


def _q_proj(h, pos, invf, sgn, w_in, qa_col, g_q, wq, n_heads, tm):
    t, d = h.shape
    q_rank = g_q.shape[1]
    assert qa_col % q_rank == 0 and n_heads % 2 == 0
    const = lambda i: (0, 0)
    return pl.pallas_call(
        functools.partial(_q_proj_kernel, n_heads),
        grid=(t // tm,),
        in_specs=[pl.BlockSpec((tm, d), lambda i: (i, 0)),
                  pl.BlockSpec((tm, 1), lambda i: (i, 0)),
                  _resident((1, LANE), const),
                  _resident((1, LANE), const),
                  _resident((d, q_rank), lambda i: (0, qa_col // q_rank)),
                  _resident(g_q.shape, const),
                  _resident(wq.shape, const)],
        out_specs=pl.BlockSpec((tm, n_heads * HEAD_PAD), lambda i: (i, 0)),
        out_shape=jax.ShapeDtypeStruct((t, n_heads * HEAD_PAD), BF16),
        compiler_params=_cparams("parallel"),
        name="q_proj",
    )(h, pos, invf, sgn, w_in, g_q, wq)


def _kv_proj_kernel(n_heads, x_ref, gm_ref, pos_ref, invf_ref, sgn_ref, wkvx_ref, gkv_ref,
                    wkt_ref, wv_ref, h_ref, kt_ref, v_ref):
    h = _rms(x_ref[...], gm_ref[...]).astype(h_ref.dtype)
    h_ref[...] = h
    kv_rank = gkv_ref.shape[1]
    z = _dot(h, wkvx_ref[...])
    kvn = _rms(z[:, :kv_rank], gkv_ref[...]).astype(BF16)
    v = _dot(kvn, wv_ref[...])
    lane = lax.broadcasted_iota(jnp.int32, (h.shape[0], LANE), 1)
    ones_col = jnp.where(lane == 0, 1.0, 0.0).astype(v_ref.dtype)
    for hd in range(n_heads):
        a = hd * HEAD_PAD
        v_ref[:, a:a + V_HEAD_DIM] = v[:, hd * V_HEAD_DIM:(hd + 1) * V_HEAD_DIM].astype(v_ref.dtype)
        v_ref[:, a + V_HEAD_DIM:a + HEAD_PAD] = ones_col
    knt = lax.dot_general(wkt_ref[...], kvn, (((1,), (1,)), ((), ())),
                          preferred_element_type=F32)
    cos, sin = _rope_tables(pos_ref, invf_ref, sgn_ref)
    krot = z[:, kv_rank:kv_rank + LANE] * cos + z[:, kv_rank + LANE:] * sin
    krt = krot.T.astype(kt_ref.dtype)
    for hd in range(n_heads):
        kt_ref[0, hd, 0:LANE, :] = knt[hd * LANE:(hd + 1) * LANE, :].astype(kt_ref.dtype)
        kt_ref[0, hd, LANE:HEAD_PAD, :] = krt


def _kv_proj(x, g_mix, pos, invf, sgn, wkvx, g_kv, wkt, wv, n_heads, batch, seq, tm):
    t, d = x.shape
    spt = seq // tm
    const = lambda i: (0, 0)
    return pl.pallas_call(
        functools.partial(_kv_proj_kernel, n_heads),
        grid=(t // tm,),
        in_specs=[pl.BlockSpec((tm, d), lambda i: (i, 0)),
                  _resident(g_mix.shape, const),
                  pl.BlockSpec((tm, 1), lambda i: (i, 0)),
                  _resident((1, LANE), const),
                  _resident((1, LANE), const),
                  _resident(wkvx.shape, const),
                  _resident(g_kv.shape, const),
                  _resident(wkt.shape, const),
                  _resident(wv.shape, const)],
        out_specs=[pl.BlockSpec((tm, d), lambda i: (i, 0)),
                   pl.BlockSpec((1, n_heads, HEAD_PAD, tm), lambda i: (i // spt, 0, 0, i % spt)),
                   pl.BlockSpec((tm, n_heads * HEAD_PAD), lambda i: (i, 0))],
        out_shape=[jax.ShapeDtypeStruct((t, d), BF16),
                   jax.ShapeDtypeStruct((batch, n_heads, HEAD_PAD, seq), BF16),
                   jax.ShapeDtypeStruct((t, n_heads * HEAD_PAD), BF16)],
        compiler_params=_cparams("parallel"),
        name="kv_proj",
    )(x, g_mix, pos, invf, sgn, wkvx, g_kv, wkt, wv)


def _attn_kernel(kc, q_ref, kt_ref, v_ref, o_ref):
    q = q_ref[...]
    tq = q.shape[0]
    seq = kt_ref.shape[-1]
    dv = o_ref.shape[-1]
    m = jnp.full((tq, 1), -jnp.inf, F32)
    acc = jnp.zeros((tq, v_ref.shape[-1]), F32)
    for c0 in range(0, seq, kc):
        s = _dot(q, kt_ref[0, 0, :, c0:c0 + kc])
        m_new = jnp.maximum(m, jnp.max(s, axis=-1, keepdims=True))
        alpha = jnp.exp2(m - m_new)
        p = jnp.exp2(s - m_new).astype(BF16)
        acc = alpha * acc + _dot(p, v_ref[c0:c0 + kc, :])
        m = m_new
    o_ref[...] = (acc[:, :dv] / acc[:, dv:dv + 1]).astype(o_ref.dtype)


def _attention(q, kt, v, n_heads, batch, seq, tq, kc):
    t = q.shape[0]
    qpt = seq // tq
    return pl.pallas_call(
        functools.partial(_attn_kernel, kc),
        grid=(batch, n_heads, qpt),
        in_specs=[pl.BlockSpec((tq, HEAD_PAD), lambda b, hd, i: (b * qpt + i, hd)),
                  pl.BlockSpec((1, 1, HEAD_PAD, seq), lambda b, hd, i: (b, hd, 0, 0)),
                  pl.BlockSpec((seq, HEAD_PAD), lambda b, hd, i: (b, hd))],
        out_specs=pl.BlockSpec((tq, V_HEAD_DIM), lambda b, hd, i: (b * qpt + i, hd)),
        out_shape=jax.ShapeDtypeStruct((t, n_heads * V_HEAD_DIM), BF16),
        compiler_params=_cparams("parallel", "parallel", "arbitrary"),
        name="attention",
    )(q, kt, v)


def _branch_kernel(tiles_per_seq, cchunk, cb_ref, u_ref, up_ref, un_ref, yb_ref, cw_ref,
                   ga_ref, gb_ref, w0_ref, w1_ref, o_ref, ya_ref):
    i = pl.program_id(0)
    tm, c = u_ref.shape

    @pl.when(pl.program_id(1) == 0)
    def _():
        keep_prev = (i % tiles_per_seq != 0).astype(F32)
        keep_next = (i % tiles_per_seq != tiles_per_seq - 1).astype(F32)
        row = lax.broadcasted_iota(jnp.int32, (tm, 1), 0)
        for c0 in range(0, c, cchunk):
            cs = slice(c0, c0 + cchunk)
            u = u_ref[:, cs].astype(F32)
            prev_row = up_ref[HALO_ROWS - 1:HALO_ROWS, cs].astype(F32) * keep_prev
            next_row = un_ref[0:1, cs].astype(F32) * keep_next
            u_dn = jnp.where(row == 0, prev_row, pltpu.roll(u, 1, axis=0))
            u_up = jnp.where(row == tm - 1, next_row, pltpu.roll(u, tm - 1, axis=0))
            conv = u_dn * cw_ref[0:1, cs] + u * cw_ref[1:2, cs] + u_up * cw_ref[2:3, cs]
            ya_ref[:, cs] = (cb_ref[:, cs].astype(F32) * conv).astype(ya_ref.dtype)

    pa = _dot(ya_ref[...], w0_ref[0])
    pb = _dot(yb_ref[...], w1_ref[0])
    o_ref[...] = (ga_ref[...].astype(F32) * pa + gb_ref[...].astype(F32) * pb).astype(o_ref.dtype)


def _branch(cb, u, yb, conv_w, gates, w_br, seq, tm, tn):
    t, c = cb.shape
    d = w_br.shape[2]
    nb = d // tn
    hb = tm // HALO_ROWS
    last_hb = t // HALO_ROWS - 1
    row = lambda i, j: (i, 0)
    return pl.pallas_call(
        functools.partial(_branch_kernel, seq // tm, min(c, 512)),
        grid=(t // tm, nb),
        in_specs=[pl.BlockSpec((tm, c), row),
                  pl.BlockSpec((tm, c), row),
                  pl.BlockSpec((HALO_ROWS, c), lambda i, j: (jnp.maximum(i * hb - 1, 0), 0)),
                  pl.BlockSpec((HALO_ROWS, c), lambda i, j: (jnp.minimum((i + 1) * hb, last_hb), 0)),
                  pl.BlockSpec((tm, c), row),
                  pl.BlockSpec(conv_w.shape, lambda i, j: (0, 0)),
                  pl.BlockSpec((tm, tn), lambda i, j: (i, j)),
                  pl.BlockSpec((tm, tn), lambda i, j: (i, j + nb)),
                  pl.BlockSpec((1, c, tn), lambda i, j: (0, 0, j)),
                  pl.BlockSpec((1, c, tn), lambda i, j: (1, 0, j))],
        out_specs=pl.BlockSpec((tm, tn), lambda i, j: (i, j)),
        out_shape=jax.ShapeDtypeStruct((t, d), BF16),
        scratch_shapes=[pltpu.VMEM((tm, c), BF16)],
        compiler_params=_cparams("parallel", "arbitrary"),
        name="branch",
    )(cb, u, u, u, yb, conv_w, gates, gates, w_br, w_br)


def _out_proj_kernel(m_ref, w_ref, x_ref, o_ref):
    o_ref[...] = x_ref[...] + _dot(m_ref[...], w_ref[...])


def _out_proj(m, w_out, x, tm, tn):
    t, d = m.shape
    n = w_out.shape[1]
    return pl.pallas_call(
        _out_proj_kernel,
        grid=(t // tm, n // tn),
        in_specs=[pl.BlockSpec((tm, d), lambda i, j: (i, 0)),
                  pl.BlockSpec((d, tn), lambda i, j: (0, j)),
                  pl.BlockSpec((tm, tn), lambda i, j: (i, j))],
        out_specs=pl.BlockSpec((tm, tn), lambda i, j: (i, j)),
        out_shape=jax.ShapeDtypeStruct((t, n), F32),
        compiler_params=_cparams("parallel", "arbitrary"),
        name="out_proj",
    )(m, w_out, x)


def _ffn_kernel(final_norm, x_ref, gf_ref, wg_ref, wu_ref, wd_ref, gl_ref, o_ref, h2_ref):
    j = pl.program_id(1)

    @pl.when(j == 0)
    def _():
        x = x_ref[...]
        h2_ref[...] = _rms(x, gf_ref[...]).astype(h2_ref.dtype)
        o_ref[...] = x

    h2 = h2_ref[...]
    g = _dot(h2, wg_ref[...])
    a = (g * (1.0 / (1.0 + jnp.exp(-g)))) * _dot(h2, wu_ref[...])
    o_ref[...] += _dot(a.astype(BF16), wd_ref[...])

    if final_norm:
        @pl.when(j == pl.num_programs(1) - 1)
        def _():
            o_ref[...] = _rms(o_ref[...], gl_ref[...])


def _ffn(x1, g_ffn, w_g, w_u, w_d, g_final, final_norm, tm, tf):
    t, d = x1.shape
    f = w_g.shape[1]
    assert f % tf == 0
    return pl.pallas_call(
        functools.partial(_ffn_kernel, final_norm),
        grid=(t // tm, f // tf),
        in_specs=[pl.BlockSpec((tm, d), lambda i, j: (i, 0)),
                  pl.BlockSpec((1, d), lambda i, j: (0, 0)),
                  pl.BlockSpec((d, tf), lambda i, j: (0, j)),
                  pl.BlockSpec((d, tf), lambda i, j: (0, j)),
                  pl.BlockSpec((tf, d), lambda i, j: (j, 0)),
                  pl.BlockSpec((1, d), lambda i, j: (0, 0))],
        out_specs=pl.BlockSpec((tm, d), lambda i, j: (i, 0)),
        out_shape=jax.ShapeDtypeStruct((t, d), F32),
        scratch_shapes=[pltpu.VMEM((tm, d), BF16)],
        compiler_params=_cparams("parallel", "arbitrary"),
        name="ffn",
    )(x1, g_ffn, w_g, w_u, w_d, g_final)


def _pad_cols(w, width):
    return jnp.pad(w, ((0, 0), (0, width - w.shape[1])))


def _swap_halves(w):
    half = w.shape[-1] // 2
    return jnp.concatenate([w[..., half:], w[..., :half]], axis=-1)


def _q_weights(w_q_b, n_heads):
    r = w_q_b.shape[0]
    w = w_q_b.reshape(r, n_heads, QK_HEAD_DIM)
    nope, rope = w[..., :QK_NOPE_DIM], w[..., QK_NOPE_DIM:]
    parts = [nope.reshape(r, -1), rope.reshape(r, -1), _swap_halves(rope).reshape(r, -1)]
    return jnp.concatenate(parts, axis=1).astype(BF16)


def _pick(pref, n):
    if n <= pref:
        return n
    t = pref
    while n % t:
        t //= 2
    return t


def kernel(x, positions, g_mix, w_in, b_gate, conv_w, g_q_a, w_q_b, g_kv_a, w_kv_b, w_branch,
           w_out, g_ffn, w_ffn_gate, w_ffn_up, w_ffn_down, g_final):
    batch, seq, d = x.shape
    depth = w_in.shape[0]
    t = batch * seq
    conv_dim = conv_w.shape[-1]
    q_rank = g_q_a.shape[-1]
    kv_rank = g_kv_a.shape[-1]
    n_heads = w_q_b.shape[-1] // QK_HEAD_DIM
    qa_col = 3 * conv_dim
    kva_col = qa_col + q_rank
    kr_col = kva_col + kv_rank
    gate_col = kr_col + QK_ROPE_DIM

    xf = x.reshape(t, d)
    pos = positions.reshape(t, 1)
    inv_freq = ROPE_THETA ** (-jnp.arange(0, QK_ROPE_DIM, 2, dtype=F32) / QK_ROPE_DIM)
    reps = 2 * LANE // QK_ROPE_DIM
    half = QK_ROPE_DIM // 2
    invf = jnp.tile(inv_freq, reps)[None, :]
    sgn = jnp.tile(jnp.concatenate([-jnp.ones((half,), F32), jnp.ones((half,), F32)]), reps // 2)[None, :]

    for l in range(depth):
        w_in_l = w_in[l]
        w_lat = w_in_l[:, :kva_col].astype(BF16)
        w_kr = w_in_l[:, kr_col:gate_col]
        wkvx = jnp.concatenate([w_in_l[:, kva_col:kr_col], _pad_cols(w_kr, LANE),
                                _pad_cols(_swap_halves(w_kr), LANE)], axis=1).astype(BF16)
        w_gate = w_in_l[:, gate_col:].astype(BF16)
        wq = _q_weights(w_q_b[l], n_heads)
        w_kv = w_kv_b[l].reshape(kv_rank, n_heads, QK_NOPE_DIM + V_HEAD_DIM)
        wkt = w_kv[..., :QK_NOPE_DIM].reshape(kv_rank, n_heads * QK_NOPE_DIM).T.astype(BF16)
        wv = w_kv[..., QK_NOPE_DIM:].reshape(kv_rank, n_heads * V_HEAD_DIM).astype(BF16)
        w_br = w_branch[l].astype(BF16)
        w_o = w_out[l].astype(BF16)
        w_fg = w_ffn_gate[l].astype(BF16)
        w_fu = w_ffn_up[l].astype(BF16)
        w_fd = w_ffn_down[l].astype(BF16)

        h, kt, v = _kv_proj(xf, g_mix[l][None, :], pos, invf, sgn, wkvx, g_kv_a[l][None, :], wkt, wv,
                            n_heads, batch, seq, _pick(256, seq))
        cb, u = _conv_proj(h, w_lat, conv_dim, _pick(1024, t), _pick(512, conv_dim))
        gates = _gate_proj(h, w_gate, b_gate[l][None, :], _pick(1024, t), _pick(1024, d))
        q = _q_proj(h, pos, invf, sgn, w_lat, qa_col, g_q_a[l][None, :], wq, n_heads,
                    _pick(512, seq))
        yb = _attention(q, kt, v, n_heads, batch, seq, _pick(1024, seq), _pick(256, seq))
        m = _branch(cb, u, yb, conv_w[l], gates, w_br, seq, _pick(1024, seq), _pick(512, d))
        xf = _out_proj(m, w_o, xf, _pick(1024, t), _pick(1024, d))
        xf = _ffn(xf, g_ffn[l][None, :], w_fg, w_fu, w_fd, g_final[None, :], l == depth - 1,
                  _pick(512, t), _pick(256, w_fg.shape[1]))
    return xf.reshape(batch, seq, d)
```

```python
import functools
import math

import jax
import jax.numpy as jnp
from jax import lax
from jax.experimental import pallas as pl
from jax.experimental.pallas import tpu as pltpu

F32 = jnp.float32
BF16 = jnp.bfloat16

RMS_EPS = 1e-6
ROPE_THETA = 10000.0
QK_NOPE_DIM = 128
QK_ROPE_DIM = 64
V_HEAD_DIM = 128
QK_HEAD_DIM = QK_NOPE_DIM + QK_ROPE_DIM
Q_SCALE = math.log2(math.e) / math.sqrt(QK_HEAD_DIM)
HEAD_PAD = 256
LANE = 128
HALO_ROWS = 16
VMEM_LIMIT = 60 * 1024 * 1024


def _cparams(*sem):
    return pltpu.CompilerParams(dimension_semantics=sem, vmem_limit_bytes=VMEM_LIMIT)


def _resident(block_shape, index_map):
    return pl.BlockSpec(block_shape, index_map, pipeline_mode=pl.Buffered(1))


def _dot(a, b):
    return jnp.dot(a, b, preferred_element_type=F32)


def _dot_nt(a, b):
    return lax.dot_general(a, b, (((1,), (1,)), ((), ())), preferred_element_type=F32)


def _rms(x, g):
    inv = lax.rsqrt(jnp.mean(x * x, axis=-1, keepdims=True) + RMS_EPS)
    return x * inv * g


def _conv_proj_kernel(h_ref, wb_ref, wc_ref, wh_ref, cb_ref, u_ref):
    h = h_ref[...]
    cb_ref[...] = _dot_nt(h, wb_ref[...]).astype(cb_ref.dtype)
    u_ref[...] = (_dot_nt(h, wc_ref[...]) * _dot_nt(h, wh_ref[...])).astype(u_ref.dtype)


def _conv_proj(h, w_in_t, conv_dim, tm, tn):
    t, d = h.shape
    nb = conv_dim // tn
    w_spec = lambda off: pl.BlockSpec((tn, d), lambda i, j: (j + off, 0))
    out = jax.ShapeDtypeStruct((t, conv_dim), BF16)
    return pl.pallas_call(
        _conv_proj_kernel,
        grid=(t // tm, nb),
        in_specs=[pl.BlockSpec((tm, d), lambda i, j: (i, 0)),
                  w_spec(0), w_spec(nb), w_spec(2 * nb)],
        out_specs=[pl.BlockSpec((tm, tn), lambda i, j: (i, j))] * 2,
        out_shape=[out, out],
        compiler_params=_cparams("parallel", "arbitrary"),
        name="conv_proj",
    )(h, w_in_t, w_in_t, w_in_t)


def _gate_proj_kernel(h_ref, w_ref, b_ref, o_ref):
    z = _dot_nt(h_ref[...], w_ref[...]) + b_ref[...]
    o_ref[...] = (1.0 / (1.0 + jnp.exp(-z))).astype(o_ref.dtype)


def _gate_proj(h, w_in_t, gate_row, b_gate, tm, tn):
    t, d = h.shape
    n = b_gate.shape[1]
    return pl.pallas_call(
        _gate_proj_kernel,
        grid=(t // tm, n // tn),
        in_specs=[pl.BlockSpec((tm, d), lambda i, j: (i, 0)),
                  pl.BlockSpec((pl.Element(tn), pl.Element(d)),
                               lambda i, j: (pl.multiple_of(gate_row + j * tn, math.gcd(gate_row, tn)), 0)),
                  pl.BlockSpec((1, tn), lambda i, j: (0, j))],
        out_specs=pl.BlockSpec((tm, tn), lambda i, j: (i, j)),
        out_shape=jax.ShapeDtypeStruct((t, n), BF16),
        compiler_params=_cparams("parallel", "arbitrary"),
        name="gate_proj",
    )(h, w_in_t, b_gate)


def _rope_tables(pos_ref, invf_ref, sgn_ref):
    ang = pos_ref[...].astype(F32) * invf_ref[...]
    return jnp.cos(ang), jnp.sin(ang) * sgn_ref[...]


def _q_proj_kernel(n_heads, h_ref, pos_ref, invf_ref, sgn_ref, wqa_ref, gq_ref, wq_ref, q_ref):
    qn = _rms(_dot_nt(h_ref[...], wqa_ref[...]), gq_ref[...]).astype(BF16)
    z = _dot(qn, wq_ref[...])
    rope0 = n_heads * QK_NOPE_DIM
    swap0 = rope0 + n_heads * QK_ROPE_DIM
    cos, sin = _rope_tables(pos_ref, invf_ref, sgn_ref)
    cos, sin = cos * Q_SCALE, sin * Q_SCALE
    first = lax.broadcasted_iota(jnp.int32, (1, LANE), 1) < QK_ROPE_DIM
    for pair in range(n_heads // 2):
        g = pair * LANE
        rot = z[:, rope0 + g:rope0 + g + LANE] * cos + z[:, swap0 + g:swap0 + g + LANE] * sin
        for k, r in enumerate((rot, pltpu.roll(rot, QK_ROPE_DIM, axis=1))):
            hd = 2 * pair + k
            a = hd * HEAD_PAD
            nope = z[:, hd * QK_NOPE_DIM:(hd + 1) * QK_NOPE_DIM] * Q_SCALE
            q_ref[:, a:a + LANE] = nope.astype(q_ref.dtype)
            q_ref[:, a + LANE:a + HEAD_PAD] = jnp.where(first, r, 0.0).astype(q_ref.dtype)


def _q_proj(h, pos, invf, sgn, w_in_t, qa_col, g_q, wq, n_heads, tm):
    t, d = h.shape
    q_rank = g_q.shape[1]
    assert qa_col % q_rank == 0 and n_heads % 2 == 0
    const = lambda i: (0, 0)
    return pl.pallas_call(
        functools.partial(_q_proj_kernel, n_heads),
        grid=(t // tm,),
        in_specs=[pl.BlockSpec((tm, d), lambda i: (i, 0)),
                  pl.BlockSpec((tm, 1), lambda i: (i, 0)),
                  _resident((1, LANE), const),
                  _resident((1, LANE), const),
                  _resident((q_rank, d), lambda i: (qa_col // q_rank, 0)),
                  _resident(g_q.shape, const),
                  _resident(wq.shape, const)],
        out_specs=pl.BlockSpec((tm, n_heads * HEAD_PAD), lambda i: (i, 0)),
        out_shape=jax.ShapeDtypeStruct((t, n_heads * HEAD_PAD), BF16),
        compiler_params=_cparams("parallel"),
        name="q_proj",
    )(h, pos, invf, sgn, w_in_t, g_q, wq)


def _kv_proj_kernel(n_heads, x_ref, gm_ref, pos_ref, invf_ref, sgn_ref, wkvx_ref, gkv_ref,
                    wkt_ref, wv_ref, h_ref, kt_ref, v_ref):
    h = _rms(x_ref[...], gm_ref[...]).astype(h_ref.dtype)
    h_ref[...] = h
    kv_rank = gkv_ref.shape[1]
    z = _dot_nt(h, wkvx_ref[...])
    kvn = _rms(z[:, :kv_rank], gkv_ref[...]).astype(BF16)
    v = _dot(kvn, wv_ref[...])
    lane = lax.broadcasted_iota(jnp.int32, (h.shape[0], LANE), 1)
    ones_col = jnp.where(lane == 0, 1.0, 0.0).astype(v_ref.dtype)
    for hd in range(n_heads):
        a = hd * HEAD_PAD
        v_ref[:, a:a + V_HEAD_DIM] = v[:, hd * V_HEAD_DIM:(hd + 1) * V_HEAD_DIM].astype(v_ref.dtype)
        v_ref[:, a + V_HEAD_DIM:a + HEAD_PAD] = ones_col
    knt = lax.dot_general(wkt_ref[...], kvn, (((1,), (1,)), ((), ())),
                          preferred_element_type=F32)
    cos, sin = _rope_tables(pos_ref, invf_ref, sgn_ref)
    krot = z[:, kv_rank:kv_rank + LANE] * cos + z[:, kv_rank + LANE:] * sin
    krt = krot.T.astype(kt_ref.dtype)
    for hd in range(n_heads):
        kt_ref[0, hd, 0:LANE, :] = knt[hd * LANE:(hd + 1) * LANE, :].astype(kt_ref.dtype)
        kt_ref[0, hd, LANE:HEAD_PAD, :] = krt


def _kv_proj(x, g_mix, pos, invf, sgn, wkvx, g_kv, wkt, wv, n_heads, batch, seq, tm):
    t, d = x.shape
    spt = seq // tm
    const = lambda i: (0, 0)
    return pl.pallas_call(
        functools.partial(_kv_proj_kernel, n_heads),
        grid=(t // tm,),
        in_specs=[pl.BlockSpec((tm, d), lambda i: (i, 0)),
                  _resident(g_mix.shape, const),
                  pl.BlockSpec((tm, 1), lambda i: (i, 0)),
                  _resident((1, LANE), const),
                  _resident((1, LANE), const),
                  _resident(wkvx.shape, const),
                  _resident(g_kv.shape, const),
                  _resident(wkt.shape, const),
                  _resident(wv.shape, const)],
        out_specs=[pl.BlockSpec((tm, d), lambda i: (i, 0)),
                   pl.BlockSpec((1, n_heads, HEAD_PAD, tm), lambda i: (i // spt, 0, 0, i % spt)),
                   pl.BlockSpec((tm, n_heads * HEAD_PAD), lambda i: (i, 0))],
        out_shape=[jax.ShapeDtypeStruct((t, d), BF16),
                   jax.ShapeDtypeStruct((batch, n_heads, HEAD_PAD, seq), BF16),
                   jax.ShapeDtypeStruct((t, n_heads * HEAD_PAD), BF16)],
        compiler_params=_cparams("parallel"),
        name="kv_proj",
    )(x, g_mix, pos, invf, sgn, wkvx, g_kv, wkt, wv)


def _attn_kernel(kc, q_ref, kt_ref, v_ref, o_ref):
    q = q_ref[...]
    tq = q.shape[0]
    seq = kt_ref.shape[-1]
    dv = o_ref.shape[-1]
    m = jnp.full((tq, 1), -jnp.inf, F32)
    acc = jnp.zeros((tq, v_ref.shape[-1]), F32)
    for c0 in range(0, seq, kc):
        s = _dot(q, kt_ref[0, 0, :, c0:c0 + kc])
        m_new = jnp.maximum(m, jnp.max(s, axis=-1, keepdims=True))
        alpha = jnp.exp2(m - m_new)
        p = jnp.exp2(s - m_new).astype(BF16)
        acc = alpha * acc + _dot(p, v_ref[c0:c0 + kc, :])
        m = m_new
    o_ref[...] = (acc[:, :dv] / acc[:, dv:dv + 1]).astype(o_ref.dtype)


def _attention(q, kt, v, n_heads, batch, seq, tq, kc):
    t = q.shape[0]
    qpt = seq // tq
    return pl.pallas_call(
        functools.partial(_attn_kernel, kc),
        grid=(batch, n_heads, qpt),
        in_specs=[pl.BlockSpec((tq, HEAD_PAD), lambda b, hd, i: (b * qpt + i, hd)),
                  pl.BlockSpec((1, 1, HEAD_PAD, seq), lambda b, hd, i: (b, hd, 0, 0)),
                  pl.BlockSpec((seq, HEAD_PAD), lambda b, hd, i: (b, hd))],
        out_specs=pl.BlockSpec((tq, V_HEAD_DIM), lambda b, hd, i: (b * qpt + i, hd)),
        out_shape=jax.ShapeDtypeStruct((t, n_heads * V_HEAD_DIM), BF16),
        compiler_params=_cparams("parallel", "parallel", "arbitrary"),
        name="attention",
    )(q, kt, v)


def _branch_kernel(tiles_per_seq, cchunk, cb_ref, u_ref, up_ref, un_ref, yb_ref, cw_ref,
                   ga_ref, gb_ref, w0_ref, w1_ref, o_ref, ya_ref):
    i = pl.program_id(0)
    tm, c = u_ref.shape

    @pl.when(pl.program_id(1) == 0)
    def _():
        keep_prev = (i % tiles_per_seq != 0).astype(F32)
        keep_next = (i % tiles_per_seq != tiles_per_seq - 1).astype(F32)
        row = lax.broadcasted_iota(jnp.int32, (tm, 1), 0)
        for c0 in range(0, c, cchunk):
            cs = slice(c0, c0 + cchunk)
            u = u_ref[:, cs].astype(F32)
            prev_row = up_ref[HALO_ROWS - 1:HALO_ROWS, cs].astype(F32) * keep_prev
            next_row = un_ref[0:1, cs].astype(F32) * keep_next
            u_dn = jnp.where(row == 0, prev_row, pltpu.roll(u, 1, axis=0))
            u_up = jnp.where(row == tm - 1, next_row, pltpu.roll(u, tm - 1, axis=0))
            conv = u_dn * cw_ref[0:1, cs] + u * cw_ref[1:2, cs] + u_up * cw_ref[2:3, cs]
            ya_ref[:, cs] = (cb_ref[:, cs].astype(F32) * conv).astype(ya_ref.dtype)

    pa = _dot(ya_ref[...], w0_ref[0])
    pb = _dot(yb_ref[...], w1_ref[0])
    o_ref[...] = (ga_ref[...].astype(F32) * pa + gb_ref[...].astype(F32) * pb).astype(o_ref.dtype)


def _branch(cb, u, yb, conv_w, gates, w_br, seq, tm, tn):
    t, c = cb.shape
    d = w_br.shape[2]
    nb = d // tn
    hb = tm // HALO_ROWS
    last_hb = t // HALO_ROWS - 1
    row = lambda i, j: (i, 0)
    return pl.pallas_call(
        functools.partial(_branch_kernel, seq // tm, min(c, 512)),
        grid=(t // tm, nb),
        in_specs=[pl.BlockSpec((tm, c), row),
                  pl.BlockSpec((tm, c), row),
                  pl.BlockSpec((HALO_ROWS, c), lambda i, j: (jnp.maximum(i * hb - 1, 0), 0)),
                  pl.BlockSpec((HALO_ROWS, c), lambda i, j: (jnp.minimum((i + 1) * hb, last_hb), 0)),
                  pl.BlockSpec((tm, c), row),
                  pl.BlockSpec(conv_w.shape, lambda i, j: (0, 0)),
                  pl.BlockSpec((tm, tn), lambda i, j: (i, j)),
                  pl.BlockSpec((tm, tn), lambda i, j: (i, j + nb)),
                  pl.BlockSpec((1, c, tn), lambda i, j: (0, 0, j)),
                  pl.BlockSpec((1, c, tn), lambda i, j: (1, 0, j))],
        out_specs=pl.BlockSpec((tm, tn), lambda i, j: (i, j)),
        out_shape=jax.ShapeDtypeStruct((t, d), BF16),
        scratch_shapes=[pltpu.VMEM((tm, c), BF16)],
        compiler_params=_cparams("parallel", "arbitrary"),
        name="branch",
    )(cb, u, u, u, yb, conv_w, gates, gates, w_br, w_br)


def _out_proj_kernel(m_ref, w_ref, x_ref, o_ref):
    o_ref[...] = x_ref[...] + _dot(m_ref[...], w_ref[...])


def _out_proj(m, w_out, x, tm, tn):
    t, d = m.shape
    n = w_out.shape[1]
    return pl.pallas_call(
        _out_proj_kernel,
        grid=(t // tm, n // tn),
        in_specs=[pl.BlockSpec((tm, d), lambda i, j: (i, 0)),
                  pl.BlockSpec((d, tn), lambda i, j: (0, j)),
                  pl.BlockSpec((tm, tn), lambda i, j: (i, j))],
        out_specs=pl.BlockSpec((tm, tn), lambda i, j: (i, j)),
        out_shape=jax.ShapeDtypeStruct((t, n), F32),
        compiler_params=_cparams("parallel", "arbitrary"),
        name="out_proj",
    )(m, w_out, x)


def _ffn_kernel(final_norm, x_ref, gf_ref, wg_ref, wu_ref, wd_ref, gl_ref, o_ref, h2_ref):
    j = pl.program_id(1)

    @pl.when(j == 0)
    def _():
        x = x_ref[...]
        h2_ref[...] = _rms(x, gf_ref[...]).astype(h2_ref.dtype)
        o_ref[...] = x

    h2 = h2_ref[...]
    g = _dot(h2, wg_ref[...])
    a = (g * (1.0 / (1.0 + jnp.exp(-g)))) * _dot(h2, wu_ref[...])
    o_ref[...] += _dot(a.astype(BF16), wd_ref[...])

    if final_norm:
        @pl.when(j == pl.num_programs(1) - 1)
        def _():
            o_ref[...] = _rms(o_ref[...], gl_ref[...])


def _ffn(x1, g_ffn, w_g, w_u, w_d, g_final, final_norm, tm, tf):
    t, d = x1.shape
    f = w_g.shape[1]
    assert f % tf == 0
    return pl.pallas_call(
        functools.partial(_ffn_kernel, final_norm),
        grid=(t // tm, f // tf),
        in_specs=[pl.BlockSpec((tm, d), lambda i, j: (i, 0)),
                  pl.BlockSpec((1, d), lambda i, j: (0, 0)),
                  pl.BlockSpec((d, tf), lambda i, j: (0, j)),
                  pl.BlockSpec((d, tf), lambda i, j: (0, j)),
                  pl.BlockSpec((tf, d), lambda i, j: (j, 0)),
                  pl.BlockSpec((1, d), lambda i, j: (0, 0))],
        out_specs=pl.BlockSpec((tm, d), lambda i, j: (i, 0)),
        out_shape=jax.ShapeDtypeStruct((t, d), F32),
        scratch_shapes=[pltpu.VMEM((tm, d), BF16)],
        compiler_params=_cparams("parallel", "arbitrary"),
        name="ffn",
    )(x1, g_ffn, w_g, w_u, w_d, g_final)


def _swap_halves(w):
    half = w.shape[-1] // 2
    return jnp.concatenate([w[..., half:], w[..., :half]], axis=-1)


def _q_weights(w_q_b, n_heads):
    r = w_q_b.shape[0]
    w = w_q_b.reshape(r, n_heads, QK_HEAD_DIM)
    nope, rope = w[..., :QK_NOPE_DIM], w[..., QK_NOPE_DIM:]
    parts = [nope.reshape(r, -1), rope.reshape(r, -1), _swap_halves(rope).reshape(r, -1)]
    return jnp.concatenate(parts, axis=1).astype(BF16)


def _pick(pref, n):
    if n <= pref:
        return n
    t = pref
    while n % t:
        t //= 2
    return t


def kernel(x, positions, g_mix, w_in, b_gate, conv_w, g_q_a, w_q_b, g_kv_a, w_kv_b, w_branch,
           w_out, g_ffn, w_ffn_gate, w_ffn_up, w_ffn_down, g_final):
    batch, seq, d = x.shape
    depth = w_in.shape[0]
    t = batch * seq
    conv_dim = conv_w.shape[-1]
    q_rank = g_q_a.shape[-1]
    kv_rank = g_kv_a.shape[-1]
    n_heads = w_q_b.shape[-1] // QK_HEAD_DIM
    qa_col = 3 * conv_dim
    kva_col = qa_col + q_rank
    kr_col = kva_col + kv_rank
    gate_col = kr_col + QK_ROPE_DIM

    xf = x.reshape(t, d)
    pos = positions.reshape(t, 1)
    inv_freq = ROPE_THETA ** (-jnp.arange(0, QK_ROPE_DIM, 2, dtype=F32) / QK_ROPE_DIM)
    reps = 2 * LANE // QK_ROPE_DIM
    half = QK_ROPE_DIM // 2
    invf = jnp.tile(inv_freq, reps)[None, :]
    sgn = jnp.tile(jnp.concatenate([-jnp.ones((half,), F32), jnp.ones((half,), F32)]), reps // 2)[None, :]

    for l in range(depth):
        w_in_t = jnp.swapaxes(w_in[l], 0, 1).astype(BF16)
        w_kr_t = jnp.swapaxes(w_in[l][:, kr_col:gate_col], 0, 1)
        w_krs_t = jnp.concatenate([w_kr_t[half:], w_kr_t[:half]], axis=0)
        pad_t = jnp.zeros((LANE - QK_ROPE_DIM, d), F32)
        wkvx_t = jnp.concatenate([w_in_t[kva_col:kr_col],
                                  jnp.concatenate([w_kr_t, pad_t, w_krs_t, pad_t], axis=0).astype(BF16)],
                                 axis=0)
        wq = _q_weights(w_q_b[l], n_heads)
        w_kv = w_kv_b[l].reshape(kv_rank, n_heads, QK_NOPE_DIM + V_HEAD_DIM)
        wkt = w_kv[..., :QK_NOPE_DIM].reshape(kv_rank, n_heads * QK_NOPE_DIM).T.astype(BF16)
        wv = w_kv[..., QK_NOPE_DIM:].reshape(kv_rank, n_heads * V_HEAD_DIM).astype(BF16)
        w_br = w_branch[l].astype(BF16)
        w_o = w_out[l].astype(BF16)
        w_fg = w_ffn_gate[l].astype(BF16)
        w_fu = w_ffn_up[l].astype(BF16)
        w_fd = w_ffn_down[l].astype(BF16)

        h, kt, v = _kv_proj(xf, g_mix[l][None, :], pos, invf, sgn, wkvx_t, g_kv_a[l][None, :], wkt, wv,
                            n_heads, batch, seq, _pick(256, seq))
        cb, u = _conv_proj(h, w_in_t, conv_dim, _pick(1024, t), _pick(512, conv_dim))
        gates = _gate_proj(h, w_in_t, gate_col, b_gate[l][None, :], _pick(1024, t), _pick(1024, d))
        q = _q_proj(h, pos, invf, sgn, w_in_t, qa_col, g_q_a[l][None, :], wq, n_heads,
                    _pick(512, seq))
        yb = _attention(q, kt, v, n_heads, batch, seq, _pick(1024, seq), _pick(256, seq))
        m = _branch(cb, u, yb, conv_w[l], gates, w_br, seq, _pick(1024, seq), _pick(512, d))
        xf = _out_proj(m, w_o, xf, _pick(1024, t), _pick(1024, d))
        xf = _ffn(xf, g_ffn[l][None, :], w_fg, w_fu, w_fd, g_final[None, :], l == depth - 1,
                  _pick(512, t), _pick(256, w_fg.shape[1]))
    return xf.reshape(batch, seq, d)
```

```python
import functools
import math

import jax
import jax.numpy as jnp
from jax import lax
from jax.experimental import pallas as pl
from jax.experimental.pallas import tpu as pltpu

F32 = jnp.float32
BF16 = jnp.bfloat16

RMS_EPS = 1e-6
ROPE_THETA = 10000.0
QK_NOPE_DIM = 128
QK_ROPE_DIM = 64
V_HEAD_DIM = 128
QK_HEAD_DIM = QK_NOPE_DIM + QK_ROPE_DIM
Q_SCALE = math.log2(math.e) / math.sqrt(QK_HEAD_DIM)
HEAD_PAD = 256
LANE = 128
HALO_ROWS = 16
VMEM_LIMIT = 60 * 1024 * 1024


def _cparams(*sem):
    return pltpu.CompilerParams(dimension_semantics=sem, vmem_limit_bytes=VMEM_LIMIT)


def _resident(block_shape, index_map):
    return pl.BlockSpec(block_shape, index_map, pipeline_mode=pl.Buffered(1))


def _dot(a, b):
    return jnp.dot(a, b, preferred_element_type=F32)


def _dot_nt(a, b):
    return lax.dot_general(a, b, (((1,), (1,)), ((), ())), preferred_element_type=F32)


def _rms(x, g):
    inv = lax.rsqrt(jnp.mean(x * x, axis=-1, keepdims=True) + RMS_EPS)
    return x * inv * g


def _conv_proj_kernel(h_ref, wb_ref, wc_ref, wh_ref, cb_ref, u_ref):
    h = h_ref[...]
    cb_ref[...] = _dot_nt(h, wb_ref[...]).astype(cb_ref.dtype)
    u_ref[...] = (_dot_nt(h, wc_ref[...]) * _dot_nt(h, wh_ref[...])).astype(u_ref.dtype)


def _conv_proj(h, w_in_t, conv_dim, tm, tn):
    t, d = h.shape
    nb = conv_dim // tn
    w_spec = lambda off: pl.BlockSpec((tn, d), lambda i, j: (j + off, 0))
    out = jax.ShapeDtypeStruct((t, conv_dim), BF16)
    return pl.pallas_call(
        _conv_proj_kernel,
        grid=(t // tm, nb),
        in_specs=[pl.BlockSpec((tm, d), lambda i, j: (i, 0)),
                  w_spec(0), w_spec(nb), w_spec(2 * nb)],
        out_specs=[pl.BlockSpec((tm, tn), lambda i, j: (i, j))] * 2,
        out_shape=[out, out],
        compiler_params=_cparams("parallel", "arbitrary"),
        name="conv_proj",
    )(h, w_in_t, w_in_t, w_in_t)


def _gate_proj_kernel(h_ref, w_ref, b_ref, o_ref):
    z = _dot_nt(h_ref[...], w_ref[...]) + b_ref[...]
    o_ref[...] = (1.0 / (1.0 + jnp.exp(-z))).astype(o_ref.dtype)


def _gate_proj(h, w_in_t, gate_row, b_gate, tm, tn):
    t, d = h.shape
    n = b_gate.shape[1]
    return pl.pallas_call(
        _gate_proj_kernel,
        grid=(t // tm, n // tn),
        in_specs=[pl.BlockSpec((tm, d), lambda i, j: (i, 0)),
                  pl.BlockSpec((pl.Element(tn), pl.Element(d)),
                               lambda i, j: (pl.multiple_of(gate_row + j * tn, math.gcd(gate_row, tn)), 0)),
                  pl.BlockSpec((1, tn), lambda i, j: (0, j))],
        out_specs=pl.BlockSpec((tm, tn), lambda i, j: (i, j)),
        out_shape=jax.ShapeDtypeStruct((t, n), BF16),
        compiler_params=_cparams("parallel", "arbitrary"),
        name="gate_proj",
    )(h, w_in_t, b_gate)


def _rope_tables(pos_ref, invf_ref, sgn_ref):
    ang = pos_ref[...].astype(F32) * invf_ref[...]
    return jnp.cos(ang), jnp.sin(ang) * sgn_ref[...]


def _q_proj_kernel(n_heads, h_ref, pos_ref, invf_ref, sgn_ref, wqa_ref, gq_ref, wq_ref, q_ref):
    qn = _rms(_dot_nt(h_ref[...], wqa_ref[...]), gq_ref[...]).astype(BF16)
    z = _dot(qn, wq_ref[...])
    rope0 = n_heads * QK_NOPE_DIM
    swap0 = rope0 + n_heads * QK_ROPE_DIM
    cos, sin = _rope_tables(pos_ref, invf_ref, sgn_ref)
    cos, sin = cos * Q_SCALE, sin * Q_SCALE
    first = lax.broadcasted_iota(jnp.int32, (1, LANE), 1) < QK_ROPE_DIM
    for pair in range(n_heads // 2):
        g = pair * LANE
        rot = z[:, rope0 + g:rope0 + g + LANE] * cos + z[:, swap0 + g:swap0 + g + LANE] * sin
        for k, r in enumerate((rot, pltpu.roll(rot, QK_ROPE_DIM, axis=1))):
            hd = 2 * pair + k
            a = hd * HEAD_PAD
            nope = z[:, hd * QK_NOPE_DIM:(hd + 1) * QK_NOPE_DIM] * Q_SCALE
            q_ref[:, a:a + LANE] = nope.astype(q_ref.dtype)
            q_ref[:, a + LANE:a + HEAD_PAD] = jnp.where(first, r, 0.0).astype(q_ref.dtype)


def _q_proj(h, pos, invf, sgn, w_in_t, qa_col, g_q, wq, n_heads, tm):
    t, d = h.shape
    q_rank = g_q.shape[1]
    assert qa_col % q_rank == 0 and n_heads % 2 == 0
    const = lambda i: (0, 0)
    return pl.pallas_call(
        functools.partial(_q_proj_kernel, n_heads),
        grid=(t // tm,),
        in_specs=[pl.BlockSpec((tm, d), lambda i: (i, 0)),
                  pl.BlockSpec((tm, 1), lambda i: (i, 0)),
                  _resident((1, LANE), const),
                  _resident((1, LANE), const),
                  _resident((q_rank, d), lambda i: (qa_col // q_rank, 0)),
                  _resident(g_q.shape, const),
                  _resident(wq.shape, const)],
        out_specs=pl.BlockSpec((tm, n_heads * HEAD_PAD), lambda i: (i, 0)),
        out_shape=jax.ShapeDtypeStruct((t, n_heads * HEAD_PAD), BF16),
        compiler_params=_cparams("parallel"),
        name="q_proj",
    )(h, pos, invf, sgn, w_in_t, g_q, wq)


def _kv_proj_kernel(n_heads, x_ref, gm_ref, pos_ref, invf_ref, sgn_ref, wkvx_ref, gkv_ref,
                    wkt_ref, wv_ref, h_ref, kt_ref, v_ref):
    h = _rms(x_ref[...], gm_ref[...]).astype(h_ref.dtype)
    h_ref[...] = h
    kv_rank = gkv_ref.shape[1]
    z = _dot_nt(h, wkvx_ref[...])
    kvn = _rms(z[:, :kv_rank], gkv_ref[...]).astype(BF16)
    v = _dot(kvn, wv_ref[...])
    lane = lax.broadcasted_iota(jnp.int32, (h.shape[0], LANE), 1)
    ones_col = jnp.where(lane == 0, 1.0, 0.0).astype(v_ref.dtype)
    for hd in range(n_heads):
        a = hd * HEAD_PAD
        v_ref[:, a:a + V_HEAD_DIM] = v[:, hd * V_HEAD_DIM:(hd + 1) * V_HEAD_DIM].astype(v_ref.dtype)
        v_ref[:, a + V_HEAD_DIM:a + HEAD_PAD] = ones_col
    knt = lax.dot_general(wkt_ref[...], kvn, (((1,), (1,)), ((), ())),
                          preferred_element_type=F32)
    cos, sin = _rope_tables(pos_ref, invf_ref, sgn_ref)
    krot = z[:, kv_rank:kv_rank + LANE] * cos + z[:, kv_rank + LANE:] * sin
    krt = krot.T.astype(kt_ref.dtype)
    for hd in range(n_heads):
        kt_ref[0, hd, 0:LANE, :] = knt[hd * LANE:(hd + 1) * LANE, :].astype(kt_ref.dtype)
        kt_ref[0, hd, LANE:HEAD_PAD, :] = krt


def _kv_proj(x, g_mix, pos, invf, sgn, wkvx, g_kv, wkt, wv, n_heads, batch, seq, tm):
    t, d = x.shape
    spt = seq // tm
    const = lambda i: (0, 0)
    return pl.pallas_call(
        functools.partial(_kv_proj_kernel, n_heads),
        grid=(t // tm,),
        in_specs=[pl.BlockSpec((tm, d), lambda i: (i, 0)),
                  _resident(g_mix.shape, const),
                  pl.BlockSpec((tm, 1), lambda i: (i, 0)),
                  _resident((1, LANE), const),
                  _resident((1, LANE), const),
                  _resident(wkvx.shape, const),
                  _resident(g_kv.shape, const),
                  _resident(wkt.shape, const),
                  _resident(wv.shape, const)],
        out_specs=[pl.BlockSpec((tm, d), lambda i: (i, 0)),
                   pl.BlockSpec((1, n_heads, HEAD_PAD, tm), lambda i: (i // spt, 0, 0, i % spt)),
                   pl.BlockSpec((tm, n_heads * HEAD_PAD), lambda i: (i, 0))],
        out_shape=[jax.ShapeDtypeStruct((t, d), BF16),
                   jax.ShapeDtypeStruct((batch, n_heads, HEAD_PAD, seq), BF16),
                   jax.ShapeDtypeStruct((t, n_heads * HEAD_PAD), BF16)],
        compiler_params=_cparams("parallel"),
        name="kv_proj",
    )(x, g_mix, pos, invf, sgn, wkvx, g_kv, wkt, wv)


def _attn_kernel(kc, q_ref, kt_ref, v_ref, o_ref):
    q = q_ref[...]
    tq = q.shape[0]
    seq = kt_ref.shape[-1]
    dv = o_ref.shape[-1]
    m = jnp.full((tq, 1), -jnp.inf, F32)
    acc = jnp.zeros((tq, v_ref.shape[-1]), F32)
    for c0 in range(0, seq, kc):
        s = _dot(q, kt_ref[0, 0, :, c0:c0 + kc])
        m_new = jnp.maximum(m, jnp.max(s, axis=-1, keepdims=True))
        alpha = jnp.exp2(m - m_new)
        p = jnp.exp2(s - m_new).astype(BF16)
        acc = alpha * acc + _dot(p, v_ref[c0:c0 + kc, :])
        m = m_new
    o_ref[...] = (acc[:, :dv] / acc[:, dv:dv + 1]).astype(o_ref.dtype)


def _attention(q, kt, v, n_heads, batch, seq, tq, kc):
    t = q.shape[0]
    qpt = seq // tq
    return pl.pallas_call(
        functools.partial(_attn_kernel, kc),
        grid=(batch, n_heads, qpt),
        in_specs=[pl.BlockSpec((tq, HEAD_PAD), lambda b, hd, i: (b * qpt + i, hd)),
                  pl.BlockSpec((1, 1, HEAD_PAD, seq), lambda b, hd, i: (b, hd, 0, 0)),
                  pl.BlockSpec((seq, HEAD_PAD), lambda b, hd, i: (b, hd))],
        out_specs=pl.BlockSpec((tq, V_HEAD_DIM), lambda b, hd, i: (b * qpt + i, hd)),
        out_shape=jax.ShapeDtypeStruct((t, n_heads * V_HEAD_DIM), BF16),
        compiler_params=_cparams("parallel", "parallel", "arbitrary"),
        name="attention",
    )(q, kt, v)


def _branch_kernel(tiles_per_seq, cchunk, cb_ref, u_ref, up_ref, un_ref, yb_ref, cw_ref,
                   ga_ref, gb_ref, w0_ref, w1_ref, o_ref, ya_ref):
    i = pl.program_id(0)
    tm, c = u_ref.shape

    @pl.when(pl.program_id(1) == 0)
    def _():
        keep_prev = (i % tiles_per_seq != 0).astype(F32)
        keep_next = (i % tiles_per_seq != tiles_per_seq - 1).astype(F32)
        row = lax.broadcasted_iota(jnp.int32, (tm, 1), 0)
        for c0 in range(0, c, cchunk):
            cs = slice(c0, c0 + cchunk)
            u = u_ref[:, cs].astype(F32)
            prev_row = up_ref[HALO_ROWS - 1:HALO_ROWS, cs].astype(F32) * keep_prev
            next_row = un_ref[0:1, cs].astype(F32) * keep_next
            u_dn = jnp.where(row == 0, prev_row, pltpu.roll(u, 1, axis=0))
            u_up = jnp.where(row == tm - 1, next_row, pltpu.roll(u, tm - 1, axis=0))
            conv = u_dn * cw_ref[0:1, cs] + u * cw_ref[1:2, cs] + u_up * cw_ref[2:3, cs]
            ya_ref[:, cs] = (cb_ref[:, cs].astype(F32) * conv).astype(ya_ref.dtype)

    pa = _dot(ya_ref[...], w0_ref[0])
    pb = _dot(yb_ref[...], w1_ref[0])
    o_ref[...] = (ga_ref[...].astype(F32) * pa + gb_ref[...].astype(F32) * pb).astype(o_ref.dtype)


def _branch(cb, u, yb, conv_w, gates, w_br, seq, tm, tn):
    t, c = cb.shape
    d = w_br.shape[2]
    nb = d // tn
    hb = tm // HALO_ROWS
    last_hb = t // HALO_ROWS - 1
    row = lambda i, j: (i, 0)
    return pl.pallas_call(
        functools.partial(_branch_kernel, seq // tm, min(c, 512)),
        grid=(t // tm, nb),
        in_specs=[pl.BlockSpec((tm, c), row),
                  pl.BlockSpec((tm, c), row),
                  pl.BlockSpec((HALO_ROWS, c), lambda i, j: (jnp.maximum(i * hb - 1, 0), 0)),
                  pl.BlockSpec((HALO_ROWS, c), lambda i, j: (jnp.minimum((i + 1) * hb, last_hb), 0)),
                  pl.BlockSpec((tm, c), row),
                  pl.BlockSpec(conv_w.shape, lambda i, j: (0, 0)),
                  pl.BlockSpec((tm, tn), lambda i, j: (i, j)),
                  pl.BlockSpec((tm, tn), lambda i, j: (i, j + nb)),
                  pl.BlockSpec((1, c, tn), lambda i, j: (0, 0, j)),
                  pl.BlockSpec((1, c, tn), lambda i, j: (1, 0, j))],
        out_specs=pl.BlockSpec((tm, tn), lambda i, j: (i, j)),
        out_shape=jax.ShapeDtypeStruct((t, d), BF16),
        scratch_shapes=[pltpu.VMEM((tm, c), BF16)],
        compiler_params=_cparams("parallel", "arbitrary"),
        name="branch",
    )(cb, u, u, u, yb, conv_w, gates, gates, w_br, w_br)


def _out_proj_kernel(m_ref, w_ref, x_ref, o_ref):
    o_ref[...] = x_ref[...] + _dot(m_ref[...], w_ref[...])


def _out_proj(m, w_out, x, tm, tn):
    t, d = m.shape
    n = w_out.shape[1]
    return pl.pallas_call(
        _out_proj_kernel,
        grid=(t // tm, n // tn),
        in_specs=[pl.BlockSpec((tm, d), lambda i, j: (i, 0)),
                  pl.BlockSpec((d, tn), lambda i, j: (0, j)),
                  pl.BlockSpec((tm, tn), lambda i, j: (i, j))],
        out_specs=pl.BlockSpec((tm, tn), lambda i, j: (i, j)),
        out_shape=jax.ShapeDtypeStruct((t, n), F32),
        compiler_params=_cparams("parallel", "arbitrary"),
        name="out_proj",
    )(m, w_out, x)


def _ffn_kernel(final_norm, x_ref, gf_ref, wg_ref, wu_ref, wd_ref, gl_ref, o_ref, h2_ref, a_ref):
    j = pl.program_id(1)
    n = pl.num_programs(1) - 1

    def act():
        h2 = h2_ref[...]
        g = _dot(h2, wg_ref[...])
        return ((g * (1.0 / (1.0 + jnp.exp(-g)))) * _dot(h2, wu_ref[...])).astype(a_ref.dtype)

    def down():
        o_ref[...] += _dot(a_ref[...], wd_ref[...])

    @pl.when(j == 0)
    def _():
        x = x_ref[...]
        h2_ref[...] = _rms(x, gf_ref[...]).astype(h2_ref.dtype)
        o_ref[...] = x
        a_ref[...] = act()

    @pl.when(jnp.logical_and(j > 0, j < n))
    def _():
        a_next = act()
        down()
        a_ref[...] = a_next

    @pl.when(j == n)
    def _():
        down()
        if final_norm:
            o_ref[...] = _rms(o_ref[...], gl_ref[...])


def _ffn(x1, g_ffn, w_g, w_u, w_d, g_final, final_norm, tm, tf):
    t, d = x1.shape
    f = w_g.shape[1]
    assert f % tf == 0
    nf = f // tf
    return pl.pallas_call(
        functools.partial(_ffn_kernel, final_norm),
        grid=(t // tm, nf + 1),
        in_specs=[pl.BlockSpec((tm, d), lambda i, j: (i, 0)),
                  pl.BlockSpec((1, d), lambda i, j: (0, 0)),
                  pl.BlockSpec((d, tf), lambda i, j: (0, jnp.minimum(j, nf - 1))),
                  pl.BlockSpec((d, tf), lambda i, j: (0, jnp.minimum(j, nf - 1))),
                  pl.BlockSpec((tf, d), lambda i, j: (jnp.maximum(j - 1, 0), 0)),
                  pl.BlockSpec((1, d), lambda i, j: (0, 0))],
        out_specs=pl.BlockSpec((tm, d), lambda i, j: (i, 0)),
        out_shape=jax.ShapeDtypeStruct((t, d), F32),
        scratch_shapes=[pltpu.VMEM((tm, d), BF16), pltpu.VMEM((tm, tf), BF16)],
        compiler_params=_cparams("parallel", "arbitrary"),
        name="ffn",
    )(x1, g_ffn, w_g, w_u, w_d, g_final)


def _swap_halves(w):
    half = w.shape[-1] // 2
    return jnp.concatenate([w[..., half:], w[..., :half]], axis=-1)


def _q_weights(w_q_b, n_heads):
    r = w_q_b.shape[0]
    w = w_q_b.reshape(r, n_heads, QK_HEAD_DIM)
    nope, rope = w[..., :QK_NOPE_DIM], w[..., QK_NOPE_DIM:]
    parts = [nope.reshape(r, -1), rope.reshape(r, -1), _swap_halves(rope).reshape(r, -1)]
    return jnp.concatenate(parts, axis=1).astype(BF16)


def _pick(pref, n):
    if n <= pref:
        return n
    t = pref
    while n % t:
        t //= 2
    return t


def kernel(x, positions, g_mix, w_in, b_gate, conv_w, g_q_a, w_q_b, g_kv_a, w_kv_b, w_branch,
           w_out, g_ffn, w_ffn_gate, w_ffn_up, w_ffn_down, g_final):
    batch, seq, d = x.shape
    depth = w_in.shape[0]
    t = batch * seq
    conv_dim = conv_w.shape[-1]
    q_rank = g_q_a.shape[-1]
    kv_rank = g_kv_a.shape[-1]
    n_heads = w_q_b.shape[-1] // QK_HEAD_DIM
    qa_col = 3 * conv_dim
    kva_col = qa_col + q_rank
    kr_col = kva_col + kv_rank
    gate_col = kr_col + QK_ROPE_DIM

    xf = x.reshape(t, d)
    pos = positions.reshape(t, 1)
    inv_freq = ROPE_THETA ** (-jnp.arange(0, QK_ROPE_DIM, 2, dtype=F32) / QK_ROPE_DIM)
    reps = 2 * LANE // QK_ROPE_DIM
    half = QK_ROPE_DIM // 2
    invf = jnp.tile(inv_freq, reps)[None, :]
    sgn = jnp.tile(jnp.concatenate([-jnp.ones((half,), F32), jnp.ones((half,), F32)]), reps // 2)[None, :]

    for l in range(depth):
        w_in_t = jnp.swapaxes(w_in[l], 0, 1).astype(BF16)
        w_kr_t = jnp.swapaxes(w_in[l][:, kr_col:gate_col], 0, 1)
        w_krs_t = jnp.concatenate([w_kr_t[half:], w_kr_t[:half]], axis=0)
        pad_t = jnp.zeros((LANE - QK_ROPE_DIM, d), F32)
        wkvx_t = jnp.concatenate([w_in_t[kva_col:kr_col],
                                  jnp.concatenate([w_kr_t, pad_t, w_krs_t, pad_t], axis=0).astype(BF16)],
                                 axis=0)
        wq = _q_weights(w_q_b[l], n_heads)
        w_kv = w_kv_b[l].reshape(kv_rank, n_heads, QK_NOPE_DIM + V_HEAD_DIM)
        wkt = w_kv[..., :QK_NOPE_DIM].reshape(kv_rank, n_heads * QK_NOPE_DIM).T.astype(BF16)
        wv = w_kv[..., QK_NOPE_DIM:].reshape(kv_rank, n_heads * V_HEAD_DIM).astype(BF16)
        w_br = w_branch[l].astype(BF16)
        w_o = w_out[l].astype(BF16)
        w_fg = w_ffn_gate[l].astype(BF16)
        w_fu = w_ffn_up[l].astype(BF16)
        w_fd = w_ffn_down[l].astype(BF16)

        h, kt, v = _kv_proj(xf, g_mix[l][None, :], pos, invf, sgn, wkvx_t, g_kv_a[l][None, :], wkt, wv,
                            n_heads, batch, seq, _pick(256, seq))
        cb, u = _conv_proj(h, w_in_t, conv_dim, _pick(1024, t), _pick(512, conv_dim))
        gates = _gate_proj(h, w_in_t, gate_col, b_gate[l][None, :], _pick(1024, t), _pick(1024, d))
        q = _q_proj(h, pos, invf, sgn, w_in_t, qa_col, g_q_a[l][None, :], wq, n_heads,
                    _pick(512, seq))
        yb = _attention(q, kt, v, n_heads, batch, seq, _pick(1024, seq), _pick(256, seq))
        m = _branch(cb, u, yb, conv_w[l], gates, w_br, seq, _pick(1024, seq), _pick(512, d))
        xf = _out_proj(m, w_o, xf, _pick(1024, t), _pick(1024, d))
        xf = _ffn(xf, g_ffn[l][None, :], w_fg, w_fu, w_fd, g_final[None, :], l == depth - 1,
                  _pick(512, t), _pick(256, w_fg.shape[1]))
    return xf.reshape(batch, seq, d)
```

```python
import functools
import math
from typing import NamedTuple

import jax
import jax.numpy as jnp
from jax import lax
from jax.experimental import pallas as pl
from jax.experimental.pallas import tpu as pltpu

F32 = jnp.float32
BF16 = jnp.bfloat16

RMS_EPS = 1e-6
ROPE_THETA = 10000.0
QK_NOPE_DIM = 128
QK_ROPE_DIM = 64
V_HEAD_DIM = 128
QK_HEAD_DIM = QK_NOPE_DIM + QK_ROPE_DIM
Q_SCALE = math.log2(math.e) / math.sqrt(QK_HEAD_DIM)
HEAD_PAD = 256
LANE = 128
HALO_ROWS = 16
CONV_CHUNK = 512
VMEM_LIMIT = 60 * 1024 * 1024


def _cparams(*sem):
    return pltpu.CompilerParams(dimension_semantics=sem, vmem_limit_bytes=VMEM_LIMIT)


def _resident(block_shape, index_map):
    return pl.BlockSpec(block_shape, index_map, pipeline_mode=pl.Buffered(1))


def _dot(a, b):
    return jnp.dot(a, b, preferred_element_type=F32)


def _dot_nt(a, b):
    return lax.dot_general(a, b, (((1,), (1,)), ((), ())), preferred_element_type=F32)


def _rms(x, g):
    inv = lax.rsqrt(jnp.mean(x * x, axis=-1, keepdims=True) + RMS_EPS)
    return x * inv * g


def _conv_proj_kernel(h_ref, wb_ref, wc_ref, wh_ref, cb_ref, u_ref):
    h = h_ref[...]
    cb_ref[...] = _dot_nt(h, wb_ref[...]).astype(cb_ref.dtype)
    u_ref[...] = (_dot_nt(h, wc_ref[...]) * _dot_nt(h, wh_ref[...])).astype(u_ref.dtype)


def _conv_proj(h, w_in_t, conv_dim, tm, tn):
    t, d = h.shape
    nb = conv_dim // tn
    w_spec = lambda off: pl.BlockSpec((tn, d), lambda i, j: (j + off, 0))
    out = jax.ShapeDtypeStruct((t, conv_dim), BF16)
    return pl.pallas_call(
        _conv_proj_kernel,
        grid=(t // tm, nb),
        in_specs=[pl.BlockSpec((tm, d), lambda i, j: (i, 0)),
                  w_spec(0), w_spec(nb), w_spec(2 * nb)],
        out_specs=[pl.BlockSpec((tm, tn), lambda i, j: (i, j))] * 2,
        out_shape=[out, out],
        compiler_params=_cparams("parallel", "arbitrary"),
        name="conv_proj",
    )(h, w_in_t, w_in_t, w_in_t)


def _gate_proj_kernel(h_ref, w_ref, b_ref, o_ref):
    z = _dot_nt(h_ref[...], w_ref[...]) + b_ref[...]
    o_ref[...] = (1.0 / (1.0 + jnp.exp(-z))).astype(o_ref.dtype)


def _gate_proj(h, w_in_t, gate_row, b_gate, tm, tn):
    t, d = h.shape
    n = b_gate.shape[1]
    return pl.pallas_call(
        _gate_proj_kernel,
        grid=(t // tm, n // tn),
        in_specs=[pl.BlockSpec((tm, d), lambda i, j: (i, 0)),
                  pl.BlockSpec((pl.Element(tn), pl.Element(d)),
                               lambda i, j: (pl.multiple_of(gate_row + j * tn, math.gcd(gate_row, tn)), 0)),
                  pl.BlockSpec((1, tn), lambda i, j: (0, j))],
        out_specs=pl.BlockSpec((tm, tn), lambda i, j: (i, j)),
        out_shape=jax.ShapeDtypeStruct((t, n), BF16),
        compiler_params=_cparams("parallel", "arbitrary"),
        name="gate_proj",
    )(h, w_in_t, b_gate)


def _rope_tables(pos_ref, invf_ref, sgn_ref):
    ang = pos_ref[...].astype(F32) * invf_ref[...]
    return jnp.cos(ang), jnp.sin(ang) * sgn_ref[...]


def _q_proj_kernel(n_heads, h_ref, pos_ref, invf_ref, sgn_ref, wqa_ref, gq_ref, wq_ref, q_ref):
    qn = _rms(_dot_nt(h_ref[...], wqa_ref[...]), gq_ref[...]).astype(BF16)
    z = _dot(qn, wq_ref[...])
    rope0 = n_heads * QK_NOPE_DIM
    swap0 = rope0 + n_heads * QK_ROPE_DIM
    cos, sin = _rope_tables(pos_ref, invf_ref, sgn_ref)
    cos, sin = cos * Q_SCALE, sin * Q_SCALE
    first = lax.broadcasted_iota(jnp.int32, (1, LANE), 1) < QK_ROPE_DIM
    for pair in range(n_heads // 2):
        g = pair * LANE
        rot = z[:, rope0 + g:rope0 + g + LANE] * cos + z[:, swap0 + g:swap0 + g + LANE] * sin
        for k, r in enumerate((rot, pltpu.roll(rot, QK_ROPE_DIM, axis=1))):
            hd = 2 * pair + k
            a = hd * HEAD_PAD
            nope = z[:, hd * QK_NOPE_DIM:(hd + 1) * QK_NOPE_DIM] * Q_SCALE
            q_ref[:, a:a + LANE] = nope.astype(q_ref.dtype)
            q_ref[:, a + LANE:a + HEAD_PAD] = jnp.where(first, r, 0.0).astype(q_ref.dtype)


def _q_proj(h, pos, invf, sgn, w_in_t, qa_col, g_q, wq, n_heads, tm):
    t, d = h.shape
    q_rank = g_q.shape[1]
    assert qa_col % q_rank == 0 and n_heads % 2 == 0
    const = lambda i: (0, 0)
    return pl.pallas_call(
        functools.partial(_q_proj_kernel, n_heads),
        grid=(t // tm,),
        in_specs=[pl.BlockSpec((tm, d), lambda i: (i, 0)),
                  pl.BlockSpec((tm, 1), lambda i: (i, 0)),
                  _resident((1, LANE), const),
                  _resident((1, LANE), const),
                  _resident((q_rank, d), lambda i: (qa_col // q_rank, 0)),
                  _resident(g_q.shape, const),
                  _resident(wq.shape, const)],
        out_specs=pl.BlockSpec((tm, n_heads * HEAD_PAD), lambda i: (i, 0)),
        out_shape=jax.ShapeDtypeStruct((t, n_heads * HEAD_PAD), BF16),
        compiler_params=_cparams("parallel"),
        name="q_proj",
    )(h, pos, invf, sgn, w_in_t, g_q, wq)


def _kv_proj_kernel(n_heads, x_ref, gm_ref, pos_ref, invf_ref, sgn_ref, wkvx_ref, gkv_ref,
                    wkt_ref, wv_ref, h_ref, kt_ref, v_ref):
    h = _rms(x_ref[...], gm_ref[...]).astype(h_ref.dtype)
    h_ref[...] = h
    kv_rank = gkv_ref.shape[1]
    z = _dot_nt(h, wkvx_ref[...])
    kvn = _rms(z[:, :kv_rank], gkv_ref[...]).astype(BF16)
    v = _dot(kvn, wv_ref[...])
    lane = lax.broadcasted_iota(jnp.int32, (h.shape[0], LANE), 1)
    ones_col = jnp.where(lane == 0, 1.0, 0.0).astype(v_ref.dtype)
    for hd in range(n_heads):
        a = hd * HEAD_PAD
        v_ref[:, a:a + V_HEAD_DIM] = v[:, hd * V_HEAD_DIM:(hd + 1) * V_HEAD_DIM].astype(v_ref.dtype)
        v_ref[:, a + V_HEAD_DIM:a + HEAD_PAD] = ones_col
    knt = lax.dot_general(wkt_ref[...], kvn, (((1,), (1,)), ((), ())),
                          preferred_element_type=F32)
    cos, sin = _rope_tables(pos_ref, invf_ref, sgn_ref)
    krot = z[:, kv_rank:kv_rank + LANE] * cos + z[:, kv_rank + LANE:] * sin
    krt = krot.T.astype(kt_ref.dtype)
    for hd in range(n_heads):
        kt_ref[0, hd, 0:LANE, :] = knt[hd * LANE:(hd + 1) * LANE, :].astype(kt_ref.dtype)
        kt_ref[0, hd, LANE:HEAD_PAD, :] = krt


def _kv_proj(x, g_mix, pos, invf, sgn, wkvx, g_kv, wkt, wv, n_heads, batch, seq, tm):
    t, d = x.shape
    spt = seq // tm
    const = lambda i: (0, 0)
    return pl.pallas_call(
        functools.partial(_kv_proj_kernel, n_heads),
        grid=(t // tm,),
        in_specs=[pl.BlockSpec((tm, d), lambda i: (i, 0)),
                  _resident(g_mix.shape, const),
                  pl.BlockSpec((tm, 1), lambda i: (i, 0)),
                  _resident((1, LANE), const),
                  _resident((1, LANE), const),
                  _resident(wkvx.shape, const),
                  _resident(g_kv.shape, const),
                  _resident(wkt.shape, const),
                  _resident(wv.shape, const)],
        out_specs=[pl.BlockSpec((tm, d), lambda i: (i, 0)),
                   pl.BlockSpec((1, n_heads, HEAD_PAD, tm), lambda i: (i // spt, 0, 0, i % spt)),
                   pl.BlockSpec((tm, n_heads * HEAD_PAD), lambda i: (i, 0))],
        out_shape=[jax.ShapeDtypeStruct((t, d), BF16),
                   jax.ShapeDtypeStruct((batch, n_heads, HEAD_PAD, seq), BF16),
                   jax.ShapeDtypeStruct((t, n_heads * HEAD_PAD), BF16)],
        compiler_params=_cparams("parallel"),
        name="kv_proj",
    )(x, g_mix, pos, invf, sgn, wkvx, g_kv, wkt, wv)


def _attn_kernel(kc, q_ref, kt_ref, v_ref, o_ref):
    q = q_ref[...]
    tq = q.shape[0]
    seq = kt_ref.shape[-1]
    dv = o_ref.shape[-1]
    m = jnp.full((tq, 1), -jnp.inf, F32)
    acc = jnp.zeros((tq, v_ref.shape[-1]), F32)
    for c0 in range(0, seq, kc):
        s = _dot(q, kt_ref[0, 0, :, c0:c0 + kc])
        m_new = jnp.maximum(m, jnp.max(s, axis=-1, keepdims=True))
        alpha = jnp.exp2(m - m_new)
        p = jnp.exp2(s - m_new).astype(BF16)
        acc = alpha * acc + _dot(p, v_ref[c0:c0 + kc, :])
        m = m_new
    o_ref[...] = (acc[:, :dv] / acc[:, dv:dv + 1]).astype(o_ref.dtype)


def _attention(q, kt, v, n_heads, batch, seq, tq, kc):
    t = q.shape[0]
    qpt = seq // tq
    return pl.pallas_call(
        functools.partial(_attn_kernel, kc),
        grid=(batch, n_heads, qpt),
        in_specs=[pl.BlockSpec((tq, HEAD_PAD), lambda b, hd, i: (b * qpt + i, hd)),
                  pl.BlockSpec((1, 1, HEAD_PAD, seq), lambda b, hd, i: (b, hd, 0, 0)),
                  pl.BlockSpec((seq, HEAD_PAD), lambda b, hd, i: (b, hd))],
        out_specs=pl.BlockSpec((tq, V_HEAD_DIM), lambda b, hd, i: (b * qpt + i, hd)),
        out_shape=jax.ShapeDtypeStruct((t, n_heads * V_HEAD_DIM), BF16),
        compiler_params=_cparams("parallel", "parallel", "arbitrary"),
        name="attention",
    )(q, kt, v)


def _branch_kernel(tiles_per_seq, cchunk, cb_ref, u_ref, up_ref, un_ref, yb_ref, cw_ref,
                   ga_ref, gb_ref, w0_ref, w1_ref, o_ref, ya_ref):
    i = pl.program_id(0)
    j = pl.program_id(1)
    tm, c = u_ref.shape

    def project():
        pa = _dot(ya_ref[...], w0_ref[0])
        pb = _dot(yb_ref[...], w1_ref[0])
        o_ref[...] = (ga_ref[...].astype(F32) * pa + gb_ref[...].astype(F32) * pb).astype(o_ref.dtype)

    @pl.when(j == 0)
    def _():
        keep_prev = (i % tiles_per_seq != 0).astype(F32)
        keep_next = (i % tiles_per_seq != tiles_per_seq - 1).astype(F32)
        row = lax.broadcasted_iota(jnp.int32, (tm, 1), 0)
        for c0 in range(0, c, cchunk):
            cs = slice(c0, c0 + cchunk)
            u = u_ref[:, cs].astype(F32)
            prev_row = up_ref[HALO_ROWS - 1:HALO_ROWS, cs].astype(F32) * keep_prev
            next_row = un_ref[0:1, cs].astype(F32) * keep_next
            u_dn = jnp.where(row == 0, prev_row, pltpu.roll(u, 1, axis=0))
            u_up = jnp.where(row == tm - 1, next_row, pltpu.roll(u, tm - 1, axis=0))
            conv = u_dn * cw_ref[0:1, cs] + u * cw_ref[1:2, cs] + u_up * cw_ref[2:3, cs]
            ya_ref[:, cs] = (cb_ref[:, cs].astype(F32) * conv).astype(ya_ref.dtype)
        project()

    @pl.when(j != 0)
    def _():
        project()


def _branch(cb, u, yb, conv_w, gates, w_br, seq, tm, tn):
    t, c = cb.shape
    d = w_br.shape[2]
    nb = d // tn
    hb = tm // HALO_ROWS
    last_hb = t // HALO_ROWS - 1
    row = lambda i, j: (i, 0)
    return pl.pallas_call(
        functools.partial(_branch_kernel, seq // tm, _pick(CONV_CHUNK, c)),
        grid=(t // tm, nb),
        in_specs=[pl.BlockSpec((tm, c), row),
                  pl.BlockSpec((tm, c), row),
                  pl.BlockSpec((HALO_ROWS, c), lambda i, j: (jnp.maximum(i * hb - 1, 0), 0)),
                  pl.BlockSpec((HALO_ROWS, c), lambda i, j: (jnp.minimum((i + 1) * hb, last_hb), 0)),
                  pl.BlockSpec((tm, c), row),
                  pl.BlockSpec(conv_w.shape, lambda i, j: (0, 0)),
                  pl.BlockSpec((tm, tn), lambda i, j: (i, j)),
                  pl.BlockSpec((tm, tn), lambda i, j: (i, j + nb)),
                  pl.BlockSpec((1, c, tn), lambda i, j: (0, 0, j)),
                  pl.BlockSpec((1, c, tn), lambda i, j: (1, 0, j))],
        out_specs=pl.BlockSpec((tm, tn), lambda i, j: (i, j)),
        out_shape=jax.ShapeDtypeStruct((t, d), BF16),
        scratch_shapes=[pltpu.VMEM((tm, c), BF16)],
        compiler_params=_cparams("parallel", "arbitrary"),
        name="branch",
    )(cb, u, u, u, yb, conv_w, gates, gates, w_br, w_br)


def _out_proj_kernel(m_ref, w_ref, x_ref, o_ref):
    o_ref[...] = x_ref[...] + _dot(m_ref[...], w_ref[...])


def _out_proj(m, w_out, x, tm, tn):
    t, d = m.shape
    n = w_out.shape[1]
    return pl.pallas_call(
        _out_proj_kernel,
        grid=(t // tm, n // tn),
        in_specs=[pl.BlockSpec((tm, d), lambda i, j: (i, 0)),
                  pl.BlockSpec((d, tn), lambda i, j: (0, j)),
                  pl.BlockSpec((tm, tn), lambda i, j: (i, j))],
        out_specs=pl.BlockSpec((tm, tn), lambda i, j: (i, j)),
        out_shape=jax.ShapeDtypeStruct((t, n), F32),
        compiler_params=_cparams("parallel", "arbitrary"),
        name="out_proj",
    )(m, w_out, x)


def _ffn_kernel(final_norm, x_ref, gf_ref, wg_ref, wu_ref, wd_ref, gl_ref, o_ref, h2_ref):
    j = pl.program_id(1)

    @pl.when(j == 0)
    def _():
        x = x_ref[...]
        h2_ref[...] = _rms(x, gf_ref[...]).astype(h2_ref.dtype)
        o_ref[...] = x

    h2 = h2_ref[...]
    g = _dot(h2, wg_ref[...])
    a = (g * (1.0 / (1.0 + jnp.exp(-g)))) * _dot(h2, wu_ref[...])
    o_ref[...] += _dot(a.astype(BF16), wd_ref[...])

    if final_norm:
        @pl.when(j == pl.num_programs(1) - 1)
        def _():
            o_ref[...] = _rms(o_ref[...], gl_ref[...])


def _ffn(x1, g_ffn, w_g, w_u, w_d, g_final, final_norm, tm, tf):
    t, d = x1.shape
    f = w_g.shape[1]
    assert f % tf == 0
    return pl.pallas_call(
        functools.partial(_ffn_kernel, final_norm),
        grid=(t // tm, f // tf),
        in_specs=[pl.BlockSpec((tm, d), lambda i, j: (i, 0)),
                  pl.BlockSpec((1, d), lambda i, j: (0, 0)),
                  pl.BlockSpec((d, tf), lambda i, j: (0, j)),
                  pl.BlockSpec((d, tf), lambda i, j: (0, j)),
                  pl.BlockSpec((tf, d), lambda i, j: (j, 0)),
                  pl.BlockSpec((1, d), lambda i, j: (0, 0))],
        out_specs=pl.BlockSpec((tm, d), lambda i, j: (i, 0)),
        out_shape=jax.ShapeDtypeStruct((t, d), F32),
        scratch_shapes=[pltpu.VMEM((tm, d), BF16)],
        compiler_params=_cparams("parallel", "arbitrary"),
        name="ffn",
    )(x1, g_ffn, w_g, w_u, w_d, g_final)


def _swap_halves(w):
    half = w.shape[-1] // 2
    return jnp.concatenate([w[..., half:], w[..., :half]], axis=-1)


def _q_weights(w_q_b, n_heads):
    r = w_q_b.shape[0]
    w = w_q_b.reshape(r, n_heads, QK_HEAD_DIM)
    nope, rope = w[..., :QK_NOPE_DIM], w[..., QK_NOPE_DIM:]
    parts = [nope.reshape(r, -1), rope.reshape(r, -1), _swap_halves(rope).reshape(r, -1)]
    return jnp.concatenate(parts, axis=1).astype(BF16)


def _pick(pref, n):
    if n <= pref:
        return n
    t = pref
    while n % t:
        t //= 2
    return t


class _Tiles(NamedTuple):
    kv_rows: int
    q_rows: int
    proj_rows: int
    conv_cols: int
    gate_cols: int
    attn_q: int
    attn_k: int
    branch_rows: int
    branch_cols: int
    ffn_rows: int
    ffn_cols: int


def _tiles(t, seq, d, conv_dim, d_ff):
    return _Tiles(kv_rows=_pick(256, seq), q_rows=_pick(512, seq), proj_rows=_pick(1024, t),
                  conv_cols=_pick(512, conv_dim), gate_cols=_pick(1024, d),
                  attn_q=_pick(1024, seq), attn_k=_pick(256, seq),
                  branch_rows=_pick(1024, seq), branch_cols=_pick(512, d),
                  ffn_rows=_pick(512, t), ffn_cols=_pick(256, d_ff))


def kernel(x, positions, g_mix, w_in, b_gate, conv_w, g_q_a, w_q_b, g_kv_a, w_kv_b, w_branch,
           w_out, g_ffn, w_ffn_gate, w_ffn_up, w_ffn_down, g_final):
    batch, seq, d = x.shape
    depth = w_in.shape[0]
    t = batch * seq
    conv_dim = conv_w.shape[-1]
    q_rank = g_q_a.shape[-1]
    kv_rank = g_kv_a.shape[-1]
    n_heads = w_q_b.shape[-1] // QK_HEAD_DIM
    qa_col = 3 * conv_dim
    kva_col = qa_col + q_rank
    kr_col = kva_col + kv_rank
    gate_col = kr_col + QK_ROPE_DIM

    tl = _tiles(t, seq, d, conv_dim, w_ffn_gate.shape[-1])

    xf = x.reshape(t, d)
    pos = positions.reshape(t, 1)
    inv_freq = ROPE_THETA ** (-jnp.arange(0, QK_ROPE_DIM, 2, dtype=F32) / QK_ROPE_DIM)
    reps = 2 * LANE // QK_ROPE_DIM
    half = QK_ROPE_DIM // 2
    invf = jnp.tile(inv_freq, reps)[None, :]
    sgn = jnp.tile(jnp.concatenate([-jnp.ones((half,), F32), jnp.ones((half,), F32)]), reps // 2)[None, :]

    for l in range(depth):
        w_in_t = jnp.swapaxes(w_in[l], 0, 1).astype(BF16)
        w_kr_t = jnp.swapaxes(w_in[l][:, kr_col:gate_col], 0, 1)
        w_krs_t = jnp.concatenate([w_kr_t[half:], w_kr_t[:half]], axis=0)
        pad_t = jnp.zeros((LANE - QK_ROPE_DIM, d), F32)
        wkvx_t = jnp.concatenate([w_in_t[kva_col:kr_col],
                                  jnp.concatenate([w_kr_t, pad_t, w_krs_t, pad_t], axis=0).astype(BF16)],
                                 axis=0)
        wq = _q_weights(w_q_b[l], n_heads)
        w_kv = w_kv_b[l].reshape(kv_rank, n_heads, QK_NOPE_DIM + V_HEAD_DIM)
        wkt = w_kv[..., :QK_NOPE_DIM].reshape(kv_rank, n_heads * QK_NOPE_DIM).T.astype(BF16)
        wv = w_kv[..., QK_NOPE_DIM:].reshape(kv_rank, n_heads * V_HEAD_DIM).astype(BF16)
        w_br = w_branch[l].astype(BF16)
        w_o = w_out[l].astype(BF16)
        w_fg = w_ffn_gate[l].astype(BF16)
        w_fu = w_ffn_up[l].astype(BF16)
        w_fd = w_ffn_down[l].astype(BF16)

        h, kt, v = _kv_proj(xf, g_mix[l][None, :], pos, invf, sgn, wkvx_t, g_kv_a[l][None, :], wkt, wv,
                            n_heads, batch, seq, tl.kv_rows)
        cb, u = _conv_proj(h, w_in_t, conv_dim, tl.proj_rows, tl.conv_cols)
        gates = _gate_proj(h, w_in_t, gate_col, b_gate[l][None, :], tl.proj_rows, tl.gate_cols)
        q = _q_proj(h, pos, invf, sgn, w_in_t, qa_col, g_q_a[l][None, :], wq, n_heads, tl.q_rows)
        yb = _attention(q, kt, v, n_heads, batch, seq, tl.attn_q, tl.attn_k)
        m = _branch(cb, u, yb, conv_w[l], gates, w_br, seq, tl.branch_rows, tl.branch_cols)
        xf = _out_proj(m, w_o, xf, tl.proj_rows, tl.gate_cols)
        xf = _ffn(xf, g_ffn[l][None, :], w_fg, w_fu, w_fd, g_final[None, :], l == depth - 1,
                  tl.ffn_rows, tl.ffn_cols)
    return xf.reshape(batch, seq, d)
```

```python
import functools
import math
from typing import NamedTuple

import jax
import jax.numpy as jnp
from jax import lax
from jax.experimental import pallas as pl
from jax.experimental.pallas import tpu as pltpu

F32 = jnp.float32
BF16 = jnp.bfloat16

RMS_EPS = 1e-6
ROPE_THETA = 10000.0
QK_NOPE_DIM = 128
QK_ROPE_DIM = 64
V_HEAD_DIM = 128
QK_HEAD_DIM = QK_NOPE_DIM + QK_ROPE_DIM
Q_SCALE = math.log2(math.e) / math.sqrt(QK_HEAD_DIM)
HEAD_PAD = 256
LANE = 128
HALO_ROWS = 16
CONV_CHUNK = 512
VMEM_LIMIT = 60 * 1024 * 1024


def _cparams(*sem):
    return pltpu.CompilerParams(dimension_semantics=sem, vmem_limit_bytes=VMEM_LIMIT)


def _resident(block_shape, index_map):
    return pl.BlockSpec(block_shape, index_map, pipeline_mode=pl.Buffered(1))


def _dot(a, b):
    return jnp.dot(a, b, preferred_element_type=F32)


def _dot_nt(a, b):
    return lax.dot_general(a, b, (((1,), (1,)), ((), ())), preferred_element_type=F32)


def _rms(x, g):
    inv = lax.rsqrt(jnp.mean(x * x, axis=-1, keepdims=True) + RMS_EPS)
    return x * inv * g


def _conv_proj_kernel(h_ref, wb_ref, wc_ref, wh_ref, cb_ref, u_ref):
    h = h_ref[...]
    cb_ref[...] = _dot_nt(h, wb_ref[...]).astype(cb_ref.dtype)
    u_ref[...] = (_dot_nt(h, wc_ref[...]) * _dot_nt(h, wh_ref[...])).astype(u_ref.dtype)


def _conv_proj(h, w_in_t, conv_dim, tm, tn):
    t, d = h.shape
    nb = conv_dim // tn
    w_spec = lambda off: pl.BlockSpec((tn, d), lambda i, j: (j + off, 0))
    out = jax.ShapeDtypeStruct((t, conv_dim), BF16)
    return pl.pallas_call(
        _conv_proj_kernel,
        grid=(t // tm, nb),
        in_specs=[pl.BlockSpec((tm, d), lambda i, j: (i, 0)),
                  w_spec(0), w_spec(nb), w_spec(2 * nb)],
        out_specs=[pl.BlockSpec((tm, tn), lambda i, j: (i, j))] * 2,
        out_shape=[out, out],
        compiler_params=_cparams("parallel", "arbitrary"),
        name="conv_proj",
    )(h, w_in_t, w_in_t, w_in_t)


def _gate_proj_kernel(h_ref, w_ref, b_ref, o_ref):
    z = _dot_nt(h_ref[...], w_ref[...]) + b_ref[...]
    o_ref[...] = (1.0 / (1.0 + jnp.exp(-z))).astype(o_ref.dtype)


def _gate_proj(h, w_in_t, gate_row, b_gate, tm, tn):
    t, d = h.shape
    n = b_gate.shape[1]
    return pl.pallas_call(
        _gate_proj_kernel,
        grid=(t // tm, n // tn),
        in_specs=[pl.BlockSpec((tm, d), lambda i, j: (i, 0)),
                  pl.BlockSpec((pl.Element(tn), pl.Element(d)),
                               lambda i, j: (pl.multiple_of(gate_row + j * tn, math.gcd(gate_row, tn)), 0)),
                  pl.BlockSpec((1, tn), lambda i, j: (0, j))],
        out_specs=pl.BlockSpec((tm, tn), lambda i, j: (i, j)),
        out_shape=jax.ShapeDtypeStruct((t, n), BF16),
        compiler_params=_cparams("parallel", "arbitrary"),
        name="gate_proj",
    )(h, w_in_t, b_gate)


def _rope_tables(pos_ref, invf_ref, sgn_ref):
    ang = pos_ref[...].astype(F32) * invf_ref[...]
    return jnp.cos(ang), jnp.sin(ang) * sgn_ref[...]


def _q_proj_kernel(n_heads, h_ref, pos_ref, invf_ref, sgn_ref, wqa_ref, gq_ref, wq_ref, q_ref):
    qn = _rms(_dot_nt(h_ref[...], wqa_ref[...]), gq_ref[...]).astype(BF16)
    z = _dot(qn, wq_ref[...])
    rope0 = n_heads * QK_NOPE_DIM
    swap0 = rope0 + n_heads * QK_ROPE_DIM
    cos, sin = _rope_tables(pos_ref, invf_ref, sgn_ref)
    cos, sin = cos * Q_SCALE, sin * Q_SCALE
    first = lax.broadcasted_iota(jnp.int32, (1, LANE), 1) < QK_ROPE_DIM
    for pair in range(n_heads // 2):
        g = pair * LANE
        rot = z[:, rope0 + g:rope0 + g + LANE] * cos + z[:, swap0 + g:swap0 + g + LANE] * sin
        for k, r in enumerate((rot, pltpu.roll(rot, QK_ROPE_DIM, axis=1))):
            hd = 2 * pair + k
            a = hd * HEAD_PAD
            nope = z[:, hd * QK_NOPE_DIM:(hd + 1) * QK_NOPE_DIM] * Q_SCALE
            q_ref[:, a:a + LANE] = nope.astype(q_ref.dtype)
            q_ref[:, a + LANE:a + HEAD_PAD] = jnp.where(first, r, 0.0).astype(q_ref.dtype)


def _q_proj(h, pos, invf, sgn, w_in_t, qa_col, g_q, wq, n_heads, tm):
    t, d = h.shape
    q_rank = g_q.shape[1]
    assert qa_col % q_rank == 0 and n_heads % 2 == 0
    const = lambda i: (0, 0)
    return pl.pallas_call(
        functools.partial(_q_proj_kernel, n_heads),
        grid=(t // tm,),
        in_specs=[pl.BlockSpec((tm, d), lambda i: (i, 0)),
                  pl.BlockSpec((tm, 1), lambda i: (i, 0)),
                  _resident((1, LANE), const),
                  _resident((1, LANE), const),
                  _resident((q_rank, d), lambda i: (qa_col // q_rank, 0)),
                  _resident(g_q.shape, const),
                  _resident(wq.shape, const)],
        out_specs=pl.BlockSpec((tm, n_heads * HEAD_PAD), lambda i: (i, 0)),
        out_shape=jax.ShapeDtypeStruct((t, n_heads * HEAD_PAD), BF16),
        compiler_params=_cparams("parallel"),
        name="q_proj",
    )(h, pos, invf, sgn, w_in_t, g_q, wq)


def _kv_proj_kernel(n_heads, x_ref, gm_ref, pos_ref, invf_ref, sgn_ref, wkvx_ref, gkv_ref,
                    wkt_ref, wv_ref, h_ref, kt_ref, v_ref):
    h = _rms(x_ref[...], gm_ref[...]).astype(h_ref.dtype)
    h_ref[...] = h
    kv_rank = gkv_ref.shape[1]
    z = _dot_nt(h, wkvx_ref[...])
    kvn = _rms(z[:, :kv_rank], gkv_ref[...]).astype(BF16)
    v = _dot(kvn, wv_ref[...])
    lane = lax.broadcasted_iota(jnp.int32, (h.shape[0], LANE), 1)
    ones_col = jnp.where(lane == 0, 1.0, 0.0).astype(v_ref.dtype)
    for hd in range(n_heads):
        a = hd * HEAD_PAD
        v_ref[:, a:a + V_HEAD_DIM] = v[:, hd * V_HEAD_DIM:(hd + 1) * V_HEAD_DIM].astype(v_ref.dtype)
        v_ref[:, a + V_HEAD_DIM:a + HEAD_PAD] = ones_col
    knt = lax.dot_general(wkt_ref[...], kvn, (((1,), (1,)), ((), ())),
                          preferred_element_type=F32)
    cos, sin = _rope_tables(pos_ref, invf_ref, sgn_ref)
    krot = z[:, kv_rank:kv_rank + LANE] * cos + z[:, kv_rank + LANE:] * sin
    krt = krot.T.astype(kt_ref.dtype)
    for hd in range(n_heads):
        kt_ref[0, hd, 0:LANE, :] = knt[hd * LANE:(hd + 1) * LANE, :].astype(kt_ref.dtype)
        kt_ref[0, hd, LANE:HEAD_PAD, :] = krt


def _kv_proj(x, g_mix, pos, invf, sgn, wkvx, g_kv, wkt, wv, n_heads, batch, seq, tm):
    t, d = x.shape
    spt = seq // tm
    const = lambda i: (0, 0)
    return pl.pallas_call(
        functools.partial(_kv_proj_kernel, n_heads),
        grid=(t // tm,),
        in_specs=[pl.BlockSpec((tm, d), lambda i: (i, 0)),
                  _resident(g_mix.shape, const),
                  pl.BlockSpec((tm, 1), lambda i: (i, 0)),
                  _resident((1, LANE), const),
                  _resident((1, LANE), const),
                  _resident(wkvx.shape, const),
                  _resident(g_kv.shape, const),
                  _resident(wkt.shape, const),
                  _resident(wv.shape, const)],
        out_specs=[pl.BlockSpec((tm, d), lambda i: (i, 0)),
                   pl.BlockSpec((1, n_heads, HEAD_PAD, tm), lambda i: (i // spt, 0, 0, i % spt)),
                   pl.BlockSpec((tm, n_heads * HEAD_PAD), lambda i: (i, 0))],
        out_shape=[jax.ShapeDtypeStruct((t, d), BF16),
                   jax.ShapeDtypeStruct((batch, n_heads, HEAD_PAD, seq), BF16),
                   jax.ShapeDtypeStruct((t, n_heads * HEAD_PAD), BF16)],
        compiler_params=_cparams("parallel"),
        name="kv_proj",
    )(x, g_mix, pos, invf, sgn, wkvx, g_kv, wkt, wv)


def _attn_kernel(kc, q_ref, kt_ref, v_ref, o_ref):
    q = q_ref[...]
    tq = q.shape[0]
    seq = kt_ref.shape[-1]
    dv = o_ref.shape[-1]
    m = jnp.full((tq, 1), -jnp.inf, F32)
    acc = jnp.zeros((tq, v_ref.shape[-1]), F32)
    for c0 in range(0, seq, kc):
        s = _dot(q, kt_ref[0, 0, :, c0:c0 + kc])
        m_new = jnp.maximum(m, jnp.max(s, axis=-1, keepdims=True))
        alpha = jnp.exp2(m - m_new)
        p = jnp.exp2(s - m_new).astype(BF16)
        acc = alpha * acc + _dot(p, v_ref[c0:c0 + kc, :])
        m = m_new
    o_ref[...] = (acc[:, :dv] / acc[:, dv:dv + 1]).astype(o_ref.dtype)


def _attention(q, kt, v, n_heads, batch, seq, tq, kc):
    t = q.shape[0]
    qpt = seq // tq
    return pl.pallas_call(
        functools.partial(_attn_kernel, kc),
        grid=(batch, n_heads, qpt),
        in_specs=[pl.BlockSpec((tq, HEAD_PAD), lambda b, hd, i: (b * qpt + i, hd)),
                  pl.BlockSpec((1, 1, HEAD_PAD, seq), lambda b, hd, i: (b, hd, 0, 0)),
                  pl.BlockSpec((seq, HEAD_PAD), lambda b, hd, i: (b, hd))],
        out_specs=pl.BlockSpec((tq, V_HEAD_DIM), lambda b, hd, i: (b * qpt + i, hd)),
        out_shape=jax.ShapeDtypeStruct((t, n_heads * V_HEAD_DIM), BF16),
        compiler_params=_cparams("parallel", "parallel", "arbitrary"),
        name="attention",
    )(q, kt, v)


def _branch_kernel(tiles_per_seq, cchunk, cb_ref, u_ref, up_ref, un_ref, yb_ref, cw_ref,
                   ga_ref, gb_ref, w0_ref, w1_ref, o_ref, ya_ref):
    i = pl.program_id(0)
    j = pl.program_id(1)
    tm, c = u_ref.shape

    def project():
        pa = _dot(ya_ref[...], w0_ref[0])
        pb = _dot(yb_ref[...], w1_ref[0])
        o_ref[...] = (ga_ref[...].astype(F32) * pa + gb_ref[...].astype(F32) * pb).astype(o_ref.dtype)

    @pl.when(j == 0)
    def _():
        keep_prev = (i % tiles_per_seq != 0).astype(F32)
        keep_next = (i % tiles_per_seq != tiles_per_seq - 1).astype(F32)
        row = lax.broadcasted_iota(jnp.int32, (tm, 1), 0)
        for c0 in range(0, c, cchunk):
            cs = slice(c0, c0 + cchunk)
            u = u_ref[:, cs].astype(F32)
            prev_row = up_ref[HALO_ROWS - 1:HALO_ROWS, cs].astype(F32) * keep_prev
            next_row = un_ref[0:1, cs].astype(F32) * keep_next
            u_dn = jnp.where(row == 0, prev_row, pltpu.roll(u, 1, axis=0))
            u_up = jnp.where(row == tm - 1, next_row, pltpu.roll(u, tm - 1, axis=0))
            conv = u_dn * cw_ref[0:1, cs] + u * cw_ref[1:2, cs] + u_up * cw_ref[2:3, cs]
            ya_ref[:, cs] = (cb_ref[:, cs].astype(F32) * conv).astype(ya_ref.dtype)
        project()

    @pl.when(j != 0)
    def _():
        project()


def _branch(cb, u, yb, conv_w, gates, w_br, seq, tm, tn):
    t, c = cb.shape
    d = w_br.shape[2]
    nb = d // tn
    hb = tm // HALO_ROWS
    last_hb = t // HALO_ROWS - 1
    row = lambda i, j: (i, 0)
    return pl.pallas_call(
        functools.partial(_branch_kernel, seq // tm, _pick(CONV_CHUNK, c)),
        grid=(t // tm, nb),
        in_specs=[pl.BlockSpec((tm, c), row),
                  pl.BlockSpec((tm, c), row),
                  pl.BlockSpec((HALO_ROWS, c), lambda i, j: (jnp.maximum(i * hb - 1, 0), 0)),
                  pl.BlockSpec((HALO_ROWS, c), lambda i, j: (jnp.minimum((i + 1) * hb, last_hb), 0)),
                  pl.BlockSpec((tm, c), row),
                  pl.BlockSpec(conv_w.shape, lambda i, j: (0, 0)),
                  pl.BlockSpec((tm, tn), lambda i, j: (i, j)),
                  pl.BlockSpec((tm, tn), lambda i, j: (i, j + nb)),
                  pl.BlockSpec((1, c, tn), lambda i, j: (0, 0, j)),
                  pl.BlockSpec((1, c, tn), lambda i, j: (1, 0, j))],
        out_specs=pl.BlockSpec((tm, tn), lambda i, j: (i, j)),
        out_shape=jax.ShapeDtypeStruct((t, d), BF16),
        scratch_shapes=[pltpu.VMEM((tm, c), BF16)],
        compiler_params=_cparams("parallel", "arbitrary"),
        name="branch",
    )(cb, u, u, u, yb, conv_w, gates, gates, w_br, w_br)


def _out_proj_kernel(m_ref, w_ref, x_ref, o_ref):
    o_ref[...] = x_ref[...] + _dot(m_ref[...], w_ref[...])


def _out_proj(m, w_out, x, tm, tn):
    t, d = m.shape
    n = w_out.shape[1]
    return pl.pallas_call(
        _out_proj_kernel,
        grid=(t // tm, n // tn),
        in_specs=[pl.BlockSpec((tm, d), lambda i, j: (i, 0)),
                  pl.BlockSpec((d, tn), lambda i, j: (0, j)),
                  pl.BlockSpec((tm, tn), lambda i, j: (i, j))],
        out_specs=pl.BlockSpec((tm, tn), lambda i, j: (i, j)),
        out_shape=jax.ShapeDtypeStruct((t, n), F32),
        compiler_params=_cparams("parallel", "arbitrary"),
        name="out_proj",
    )(m, w_out, x)


def _ffn_kernel(final_norm, x_ref, gf_ref, wg_ref, wu_ref, wd_ref, gl_ref, o_ref, h2_ref):
    j = pl.program_id(1)
    last = pl.num_programs(1) - 1

    def accumulate():
        h2 = h2_ref[...]
        g = _dot(h2, wg_ref[...])
        a = (g * (1.0 / (1.0 + jnp.exp(-g)))) * _dot(h2, wu_ref[...])
        o_ref[...] += _dot(a.astype(BF16), wd_ref[...])

    @pl.when(j == 0)
    def _():
        x = x_ref[...]
        h2_ref[...] = _rms(x, gf_ref[...]).astype(h2_ref.dtype)
        o_ref[...] = x
        accumulate()

    @pl.when(jnp.logical_and(j > 0, j < last))
    def _():
        accumulate()

    @pl.when(j == last)
    def _():
        accumulate()
        if final_norm:
            o_ref[...] = _rms(o_ref[...], gl_ref[...])


def _ffn(x1, g_ffn, w_g, w_u, w_d, g_final, final_norm, tm, tf):
    t, d = x1.shape
    f = w_g.shape[1]
    assert f % tf == 0 and f // tf >= 2
    return pl.pallas_call(
        functools.partial(_ffn_kernel, final_norm),
        grid=(t // tm, f // tf),
        in_specs=[pl.BlockSpec((tm, d), lambda i, j: (i, 0)),
                  pl.BlockSpec((1, d), lambda i, j: (0, 0)),
                  pl.BlockSpec((d, tf), lambda i, j: (0, j)),
                  pl.BlockSpec((d, tf), lambda i, j: (0, j)),
                  pl.BlockSpec((tf, d), lambda i, j: (j, 0)),
                  pl.BlockSpec((1, d), lambda i, j: (0, 0))],
        out_specs=pl.BlockSpec((tm, d), lambda i, j: (i, 0)),
        out_shape=jax.ShapeDtypeStruct((t, d), F32),
        scratch_shapes=[pltpu.VMEM((tm, d), BF16)],
        compiler_params=_cparams("parallel", "arbitrary"),
        name="ffn",
    )(x1, g_ffn, w_g, w_u, w_d, g_final)


def _swap_halves(w):
    half = w.shape[-1] // 2
    return jnp.concatenate([w[..., half:], w[..., :half]], axis=-1)


def _q_weights(w_q_b, n_heads):
    r = w_q_b.shape[0]
    w = w_q_b.reshape(r, n_heads, QK_HEAD_DIM)
    nope, rope = w[..., :QK_NOPE_DIM], w[..., QK_NOPE_DIM:]
    parts = [nope.reshape(r, -1), rope.reshape(r, -1), _swap_halves(rope).reshape(r, -1)]
    return jnp.concatenate(parts, axis=1).astype(BF16)


def _pick(pref, n):
    if n <= pref:
        return n
    t = pref
    while n % t:
        t //= 2
    return t


class _Tiles(NamedTuple):
    kv_rows: int
    q_rows: int
    proj_rows: int
    conv_cols: int
    gate_cols: int
    attn_q: int
    attn_k: int
    branch_rows: int
    branch_cols: int
    ffn_rows: int
    ffn_cols: int


def _tiles(t, seq, d, conv_dim, d_ff):
    return _Tiles(kv_rows=_pick(256, seq), q_rows=_pick(512, seq), proj_rows=_pick(1024, t),
                  conv_cols=_pick(512, conv_dim), gate_cols=_pick(1024, d),
                  attn_q=_pick(1024, seq), attn_k=_pick(256, seq),
                  branch_rows=_pick(1024, seq), branch_cols=_pick(512, d),
                  ffn_rows=_pick(512, t), ffn_cols=_pick(256, d_ff))


def kernel(x, positions, g_mix, w_in, b_gate, conv_w, g_q_a, w_q_b, g_kv_a, w_kv_b, w_branch,
           w_out, g_ffn, w_ffn_gate, w_ffn_up, w_ffn_down, g_final):
    batch, seq, d = x.shape
    depth = w_in.shape[0]
    t = batch * seq
    conv_dim = conv_w.shape[-1]
    q_rank = g_q_a.shape[-1]
    kv_rank = g_kv_a.shape[-1]
    n_heads = w_q_b.shape[-1] // QK_HEAD_DIM
    qa_col = 3 * conv_dim
    kva_col = qa_col + q_rank
    kr_col = kva_col + kv_rank
    gate_col = kr_col + QK_ROPE_DIM

    tl = _tiles(t, seq, d, conv_dim, w_ffn_gate.shape[-1])

    xf = x.reshape(t, d)
    pos = positions.reshape(t, 1)
    inv_freq = ROPE_THETA ** (-jnp.arange(0, QK_ROPE_DIM, 2, dtype=F32) / QK_ROPE_DIM)
    reps = 2 * LANE // QK_ROPE_DIM
    half = QK_ROPE_DIM // 2
    invf = jnp.tile(inv_freq, reps)[None, :]
    sgn = jnp.tile(jnp.concatenate([-jnp.ones((half,), F32), jnp.ones((half,), F32)]), reps // 2)[None, :]

    for l in range(depth):
        w_in_t = jnp.swapaxes(w_in[l], 0, 1).astype(BF16)
        w_kr_t = jnp.swapaxes(w_in[l][:, kr_col:gate_col], 0, 1)
        w_krs_t = jnp.concatenate([w_kr_t[half:], w_kr_t[:half]], axis=0)
        pad_t = jnp.zeros((LANE - QK_ROPE_DIM, d), F32)
        wkvx_t = jnp.concatenate([w_in_t[kva_col:kr_col],
                                  jnp.concatenate([w_kr_t, pad_t, w_krs_t, pad_t], axis=0).astype(BF16)],
                                 axis=0)
        wq = _q_weights(w_q_b[l], n_heads)
        w_kv = w_kv_b[l].reshape(kv_rank, n_heads, QK_NOPE_DIM + V_HEAD_DIM)
        wkt = w_kv[..., :QK_NOPE_DIM].reshape(kv_rank, n_heads * QK_NOPE_DIM).T.astype(BF16)
        wv = w_kv[..., QK_NOPE_DIM:].reshape(kv_rank, n_heads * V_HEAD_DIM).astype(BF16)
        w_br = w_branch[l].astype(BF16)
        w_o = w_out[l].astype(BF16)
        w_fg = w_ffn_gate[l].astype(BF16)
        w_fu = w_ffn_up[l].astype(BF16)
        w_fd = w_ffn_down[l].astype(BF16)

        h, kt, v = _kv_proj(xf, g_mix[l][None, :], pos, invf, sgn, wkvx_t, g_kv_a[l][None, :], wkt, wv,
                            n_heads, batch, seq, tl.kv_rows)
        cb, u = _conv_proj(h, w_in_t, conv_dim, tl.proj_rows, tl.conv_cols)
        gates = _gate_proj(h, w_in_t, gate_col, b_gate[l][None, :], tl.proj_rows, tl.gate_cols)
        q = _q_proj(h, pos, invf, sgn, w_in_t, qa_col, g_q_a[l][None, :], wq, n_heads, tl.q_rows)
        yb = _attention(q, kt, v, n_heads, batch, seq, tl.attn_q, tl.attn_k)
        m = _branch(cb, u, yb, conv_w[l], gates, w_br, seq, tl.branch_rows, tl.branch_cols)
        xf = _out_proj(m, w_o, xf, tl.proj_rows, tl.gate_cols)
        xf = _ffn(xf, g_ffn[l][None, :], w_fg, w_fu, w_fd, g_final[None, :], l == depth - 1,
                  tl.ffn_rows, tl.ffn_cols)
    return xf.reshape(batch, seq, d)
```

```python
import functools
import math
from typing import NamedTuple

import jax
import jax.numpy as jnp
from jax import lax
from jax.experimental import pallas as pl
from jax.experimental.pallas import tpu as pltpu

F32 = jnp.float32
BF16 = jnp.bfloat16

RMS_EPS = 1e-6
ROPE_THETA = 10000.0
QK_NOPE_DIM = 128
QK_ROPE_DIM = 64
V_HEAD_DIM = 128
QK_HEAD_DIM = QK_NOPE_DIM + QK_ROPE_DIM
Q_SCALE = math.log2(math.e) / math.sqrt(QK_HEAD_DIM)
HEAD_PAD = 256
LANE = 128
HALO_ROWS = 16
CONV_CHUNK = 512
FFN_X_CHUNKS = 4
VMEM_LIMIT = 60 * 1024 * 1024


def _cparams(*sem):
    return pltpu.CompilerParams(dimension_semantics=sem, vmem_limit_bytes=VMEM_LIMIT)


def _resident(block_shape, index_map):
    return pl.BlockSpec(block_shape, index_map, pipeline_mode=pl.Buffered(1))


def _dot(a, b):
    return jnp.dot(a, b, preferred_element_type=F32)


def _dot_nt(a, b):
    return lax.dot_general(a, b, (((1,), (1,)), ((), ())), preferred_element_type=F32)


def _rms(x, g):
    inv = lax.rsqrt(jnp.mean(x * x, axis=-1, keepdims=True) + RMS_EPS)
    return x * inv * g


def _conv_proj_kernel(h_ref, wb_ref, wc_ref, wh_ref, cb_ref, u_ref):
    h = h_ref[...]
    cb_ref[...] = _dot_nt(h, wb_ref[...]).astype(cb_ref.dtype)
    u_ref[...] = (_dot_nt(h, wc_ref[...]) * _dot_nt(h, wh_ref[...])).astype(u_ref.dtype)


def _conv_proj(h, w_in_t, conv_dim, tm, tn):
    t, d = h.shape
    nb = conv_dim // tn
    w_spec = lambda off: pl.BlockSpec((tn, d), lambda i, j: (j + off, 0))
    out = jax.ShapeDtypeStruct((t, conv_dim), BF16)
    return pl.pallas_call(
        _conv_proj_kernel,
        grid=(t // tm, nb),
        in_specs=[pl.BlockSpec((tm, d), lambda i, j: (i, 0)),
                  w_spec(0), w_spec(nb), w_spec(2 * nb)],
        out_specs=[pl.BlockSpec((tm, tn), lambda i, j: (i, j))] * 2,
        out_shape=[out, out],
        compiler_params=_cparams("parallel", "arbitrary"),
        name="conv_proj",
    )(h, w_in_t, w_in_t, w_in_t)


def _gate_proj_kernel(h_ref, w_ref, b_ref, o_ref):
    z = _dot_nt(h_ref[...], w_ref[...]) + b_ref[...]
    o_ref[...] = (1.0 / (1.0 + jnp.exp(-z))).astype(o_ref.dtype)


def _gate_proj(h, w_in_t, gate_row, b_gate, tm, tn):
    t, d = h.shape
    n = b_gate.shape[1]
    return pl.pallas_call(
        _gate_proj_kernel,
        grid=(t // tm, n // tn),
        in_specs=[pl.BlockSpec((tm, d), lambda i, j: (i, 0)),
                  pl.BlockSpec((pl.Element(tn), pl.Element(d)),
                               lambda i, j: (pl.multiple_of(gate_row + j * tn, math.gcd(gate_row, tn)), 0)),
                  pl.BlockSpec((1, tn), lambda i, j: (0, j))],
        out_specs=pl.BlockSpec((tm, tn), lambda i, j: (i, j)),
        out_shape=jax.ShapeDtypeStruct((t, n), BF16),
        compiler_params=_cparams("parallel", "arbitrary"),
        name="gate_proj",
    )(h, w_in_t, b_gate)


def _rope_tables(pos_ref, invf_ref, sgn_ref):
    ang = pos_ref[...].astype(F32) * invf_ref[...]
    return jnp.cos(ang), jnp.sin(ang) * sgn_ref[...]


def _q_proj_kernel(n_heads, h_ref, pos_ref, invf_ref, sgn_ref, wqa_ref, gq_ref, wq_ref, q_ref):
    qn = _rms(_dot_nt(h_ref[...], wqa_ref[...]), gq_ref[...]).astype(BF16)
    z = _dot(qn, wq_ref[...])
    rope0 = n_heads * QK_NOPE_DIM
    swap0 = rope0 + n_heads * QK_ROPE_DIM
    cos, sin = _rope_tables(pos_ref, invf_ref, sgn_ref)
    cos, sin = cos * Q_SCALE, sin * Q_SCALE
    first = lax.broadcasted_iota(jnp.int32, (1, LANE), 1) < QK_ROPE_DIM
    for pair in range(n_heads // 2):
        g = pair * LANE
        rot = z[:, rope0 + g:rope0 + g + LANE] * cos + z[:, swap0 + g:swap0 + g + LANE] * sin
        for k, r in enumerate((rot, pltpu.roll(rot, QK_ROPE_DIM, axis=1))):
            hd = 2 * pair + k
            a = hd * HEAD_PAD
            nope = z[:, hd * QK_NOPE_DIM:(hd + 1) * QK_NOPE_DIM] * Q_SCALE
            q_ref[:, a:a + LANE] = nope.astype(q_ref.dtype)
            q_ref[:, a + LANE:a + HEAD_PAD] = jnp.where(first, r, 0.0).astype(q_ref.dtype)


def _q_proj(h, pos, invf, sgn, w_in_t, qa_col, g_q, wq, n_heads, tm):
    t, d = h.shape
    q_rank = g_q.shape[1]
    assert qa_col % q_rank == 0 and n_heads % 2 == 0
    const = lambda i: (0, 0)
    return pl.pallas_call(
        functools.partial(_q_proj_kernel, n_heads),
        grid=(t // tm,),
        in_specs=[pl.BlockSpec((tm, d), lambda i: (i, 0)),
                  pl.BlockSpec((tm, 1), lambda i: (i, 0)),
                  _resident((1, LANE), const),
                  _resident((1, LANE), const),
                  _resident((q_rank, d), lambda i: (qa_col // q_rank, 0)),
                  _resident(g_q.shape, const),
                  _resident(wq.shape, const)],
        out_specs=pl.BlockSpec((tm, n_heads * HEAD_PAD), lambda i: (i, 0)),
        out_shape=jax.ShapeDtypeStruct((t, n_heads * HEAD_PAD), BF16),
        compiler_params=_cparams("parallel"),
        name="q_proj",
    )(h, pos, invf, sgn, w_in_t, g_q, wq)


def _kv_proj_kernel(n_heads, x_ref, gm_ref, pos_ref, invf_ref, sgn_ref, wkvx_ref, gkv_ref,
                    wkt_ref, wv_ref, h_ref, kt_ref, v_ref):
    h = _rms(x_ref[...], gm_ref[...]).astype(h_ref.dtype)
    h_ref[...] = h
    kv_rank = gkv_ref.shape[1]
    z = _dot_nt(h, wkvx_ref[...])
    kvn = _rms(z[:, :kv_rank], gkv_ref[...]).astype(BF16)
    v = _dot(kvn, wv_ref[...])
    lane = lax.broadcasted_iota(jnp.int32, (h.shape[0], LANE), 1)
    ones_col = jnp.where(lane == 0, 1.0, 0.0).astype(v_ref.dtype)
    for hd in range(n_heads):
        a = hd * HEAD_PAD
        v_ref[:, a:a + V_HEAD_DIM] = v[:, hd * V_HEAD_DIM:(hd + 1) * V_HEAD_DIM].astype(v_ref.dtype)
        v_ref[:, a + V_HEAD_DIM:a + HEAD_PAD] = ones_col
    knt = lax.dot_general(wkt_ref[...], kvn, (((1,), (1,)), ((), ())),
                          preferred_element_type=F32)
    cos, sin = _rope_tables(pos_ref, invf_ref, sgn_ref)
    krot = z[:, kv_rank:kv_rank + LANE] * cos + z[:, kv_rank + LANE:] * sin
    krt = krot.T.astype(kt_ref.dtype)
    for hd in range(n_heads):
        kt_ref[0, hd, 0:LANE, :] = knt[hd * LANE:(hd + 1) * LANE, :].astype(kt_ref.dtype)
        kt_ref[0, hd, LANE:HEAD_PAD, :] = krt


def _kv_proj(x, g_mix, pos, invf, sgn, wkvx, g_kv, wkt, wv, n_heads, batch, seq, tm):
    t, d = x.shape
    spt = seq // tm
    const = lambda i: (0, 0)
    return pl.pallas_call(
        functools.partial(_kv_proj_kernel, n_heads),
        grid=(t // tm,),
        in_specs=[pl.BlockSpec((tm, d), lambda i: (i, 0)),
                  _resident(g_mix.shape, const),
                  pl.BlockSpec((tm, 1), lambda i: (i, 0)),
                  _resident((1, LANE), const),
                  _resident((1, LANE), const),
                  _resident(wkvx.shape, const),
                  _resident(g_kv.shape, const),
                  _resident(wkt.shape, const),
                  _resident(wv.shape, const)],
        out_specs=[pl.BlockSpec((tm, d), lambda i: (i, 0)),
                   pl.BlockSpec((1, n_heads, HEAD_PAD, tm), lambda i: (i // spt, 0, 0, i % spt)),
                   pl.BlockSpec((tm, n_heads * HEAD_PAD), lambda i: (i, 0))],
        out_shape=[jax.ShapeDtypeStruct((t, d), BF16),
                   jax.ShapeDtypeStruct((batch, n_heads, HEAD_PAD, seq), BF16),
                   jax.ShapeDtypeStruct((t, n_heads * HEAD_PAD), BF16)],
        compiler_params=_cparams("parallel"),
        name="kv_proj",
    )(x, g_mix, pos, invf, sgn, wkvx, g_kv, wkt, wv)


def _attn_kernel(kc, q_ref, kt_ref, v_ref, o_ref):
    q = q_ref[...]
    tq = q.shape[0]
    seq = kt_ref.shape[-1]
    dv = o_ref.shape[-1]
    m = jnp.full((tq, 1), -jnp.inf, F32)
    acc = jnp.zeros((tq, v_ref.shape[-1]), F32)
    for c0 in range(0, seq, kc):
        s = _dot(q, kt_ref[0, 0, :, c0:c0 + kc])
        m_new = jnp.maximum(m, jnp.max(s, axis=-1, keepdims=True))
        alpha = jnp.exp2(m - m_new)
        p = jnp.exp2(s - m_new).astype(BF16)
        acc = alpha * acc + _dot(p, v_ref[c0:c0 + kc, :])
        m = m_new
    o_ref[...] = (acc[:, :dv] / acc[:, dv:dv + 1]).astype(o_ref.dtype)


def _attention(q, kt, v, n_heads, batch, seq, tq, kc):
    t = q.shape[0]
    qpt = seq // tq
    return pl.pallas_call(
        functools.partial(_attn_kernel, kc),
        grid=(batch, n_heads, qpt),
        in_specs=[pl.BlockSpec((tq, HEAD_PAD), lambda b, hd, i: (b * qpt + i, hd)),
                  pl.BlockSpec((1, 1, HEAD_PAD, seq), lambda b, hd, i: (b, hd, 0, 0)),
                  pl.BlockSpec((seq, HEAD_PAD), lambda b, hd, i: (b, hd))],
        out_specs=pl.BlockSpec((tq, V_HEAD_DIM), lambda b, hd, i: (b * qpt + i, hd)),
        out_shape=jax.ShapeDtypeStruct((t, n_heads * V_HEAD_DIM), BF16),
        compiler_params=_cparams("parallel", "parallel", "arbitrary"),
        name="attention",
    )(q, kt, v)


def _branch_kernel(tiles_per_seq, cchunk, cb_ref, u_ref, up_ref, un_ref, yb_ref, cw_ref,
                   ga_ref, gb_ref, w0_ref, w1_ref, o_ref, ya_ref):
    i = pl.program_id(0)
    j = pl.program_id(1)
    tm, c = u_ref.shape

    def project():
        pa = _dot(ya_ref[...], w0_ref[0])
        pb = _dot(yb_ref[...], w1_ref[0])
        o_ref[...] = (ga_ref[...].astype(F32) * pa + gb_ref[...].astype(F32) * pb).astype(o_ref.dtype)

    @pl.when(j == 0)
    def _():
        keep_prev = (i % tiles_per_seq != 0).astype(F32)
        keep_next = (i % tiles_per_seq != tiles_per_seq - 1).astype(F32)
        row = lax.broadcasted_iota(jnp.int32, (tm, 1), 0)
        for c0 in range(0, c, cchunk):
            cs = slice(c0, c0 + cchunk)
            u = u_ref[:, cs].astype(F32)
            prev_row = up_ref[HALO_ROWS - 1:HALO_ROWS, cs].astype(F32) * keep_prev
            next_row = un_ref[0:1, cs].astype(F32) * keep_next
            u_dn = jnp.where(row == 0, prev_row, pltpu.roll(u, 1, axis=0))
            u_up = jnp.where(row == tm - 1, next_row, pltpu.roll(u, tm - 1, axis=0))
            conv = u_dn * cw_ref[0:1, cs] + u * cw_ref[1:2, cs] + u_up * cw_ref[2:3, cs]
            ya_ref[:, cs] = (cb_ref[:, cs].astype(F32) * conv).astype(ya_ref.dtype)
        project()

    @pl.when(j != 0)
    def _():
        project()


def _branch(cb, u, yb, conv_w, gates, w_br, seq, tm, tn):
    t, c = cb.shape
    d = w_br.shape[2]
    nb = d // tn
    hb = tm // HALO_ROWS
    last_hb = t // HALO_ROWS - 1
    row = lambda i, j: (i, 0)
    return pl.pallas_call(
        functools.partial(_branch_kernel, seq // tm, _pick(CONV_CHUNK, c)),
        grid=(t // tm, nb),
        in_specs=[pl.BlockSpec((tm, c), row),
                  pl.BlockSpec((tm, c), row),
                  pl.BlockSpec((HALO_ROWS, c), lambda i, j: (jnp.maximum(i * hb - 1, 0), 0)),
                  pl.BlockSpec((HALO_ROWS, c), lambda i, j: (jnp.minimum((i + 1) * hb, last_hb), 0)),
                  pl.BlockSpec((tm, c), row),
                  pl.BlockSpec(conv_w.shape, lambda i, j: (0, 0)),
                  pl.BlockSpec((tm, tn), lambda i, j: (i, j)),
                  pl.BlockSpec((tm, tn), lambda i, j: (i, j + nb)),
                  pl.BlockSpec((1, c, tn), lambda i, j: (0, 0, j)),
                  pl.BlockSpec((1, c, tn), lambda i, j: (1, 0, j))],
        out_specs=pl.BlockSpec((tm, tn), lambda i, j: (i, j)),
        out_shape=jax.ShapeDtypeStruct((t, d), BF16),
        scratch_shapes=[pltpu.VMEM((tm, c), BF16)],
        compiler_params=_cparams("parallel", "arbitrary"),
        name="branch",
    )(cb, u, u, u, yb, conv_w, gates, gates, w_br, w_br)


def _out_proj_kernel(m_ref, w_ref, x_ref, o_ref):
    o_ref[...] = x_ref[...] + _dot(m_ref[...], w_ref[...])


def _out_proj(m, w_out, x, tm, tn):
    t, d = m.shape
    n = w_out.shape[1]
    return pl.pallas_call(
        _out_proj_kernel,
        grid=(t // tm, n // tn),
        in_specs=[pl.BlockSpec((tm, d), lambda i, j: (i, 0)),
                  pl.BlockSpec((d, tn), lambda i, j: (0, j)),
                  pl.BlockSpec((tm, tn), lambda i, j: (i, j))],
        out_specs=pl.BlockSpec((tm, tn), lambda i, j: (i, j)),
        out_shape=jax.ShapeDtypeStruct((t, n), F32),
        compiler_params=_cparams("parallel", "arbitrary"),
        name="out_proj",
    )(m, w_out, x)


def _ffn_kernel(final_norm, n_x, *refs):
    x_refs = refs[:n_x]
    gf_ref, wg_ref, wu_ref, wd_ref, gl_ref, o_ref, h2_ref = refs[n_x:]
    j = pl.program_id(1)
    last = pl.num_programs(1) - 1

    def accumulate():
        h2 = h2_ref[...]
        g = _dot(h2, wg_ref[...])
        a = (g * (1.0 / (1.0 + jnp.exp(-g)))) * _dot(h2, wu_ref[...])
        o_ref[...] += _dot(a.astype(BF16), wd_ref[...])

    @pl.when(j == 0)
    def _():
        rows = x_refs[0].shape[0]
        for k, x_ref in enumerate(x_refs):
            x = x_ref[...]
            h2_ref[k * rows:(k + 1) * rows, :] = _rms(x, gf_ref[...]).astype(h2_ref.dtype)
            o_ref[k * rows:(k + 1) * rows, :] = x
        accumulate()

    @pl.when(jnp.logical_and(j > 0, j < last))
    def _():
        accumulate()

    @pl.when(j == last)
    def _():
        accumulate()
        if final_norm:
            o_ref[...] = _rms(o_ref[...], gl_ref[...])


def _ffn(x1, g_ffn, w_g, w_u, w_d, g_final, final_norm, tm, tf):
    t, d = x1.shape
    f = w_g.shape[1]
    nf = f // tf
    assert f % tf == 0 and nf >= 2
    n_tiles = t // tm

    n_x = FFN_X_CHUNKS if tm % (FFN_X_CHUNKS * 8) == 0 and nf > FFN_X_CHUNKS else 1
    rows = tm // n_x

    def x_spec(k):
        switch = nf * (k + 1) // (n_x + 1)
        def index(i, j):
            tile = jnp.minimum(i + (j >= switch).astype(jnp.int32), n_tiles - 1)
            return (tile * n_x + k, 0)
        return pl.BlockSpec((rows, d), index)

    return pl.pallas_call(
        functools.partial(_ffn_kernel, final_norm, n_x),
        grid=(n_tiles, nf),
        in_specs=[x_spec(k) for k in range(n_x)] + [
                  pl.BlockSpec((1, d), lambda i, j: (0, 0)),
                  pl.BlockSpec((d, tf), lambda i, j: (0, j)),
                  pl.BlockSpec((d, tf), lambda i, j: (0, j)),
                  pl.BlockSpec((tf, d), lambda i, j: (j, 0)),
                  pl.BlockSpec((1, d), lambda i, j: (0, 0))],
        out_specs=pl.BlockSpec((tm, d), lambda i, j: (i, 0)),
        out_shape=jax.ShapeDtypeStruct((t, d), F32),
        scratch_shapes=[pltpu.VMEM((tm, d), BF16)],
        compiler_params=_cparams("parallel", "arbitrary"),
        name="ffn",
    )(*([x1] * n_x), g_ffn, w_g, w_u, w_d, g_final)


def _swap_halves(w):
    half = w.shape[-1] // 2
    return jnp.concatenate([w[..., half:], w[..., :half]], axis=-1)


def _q_weights(w_q_b, n_heads):
    r = w_q_b.shape[0]
    w = w_q_b.reshape(r, n_heads, QK_HEAD_DIM)
    nope, rope = w[..., :QK_NOPE_DIM], w[..., QK_NOPE_DIM:]
    parts = [nope.reshape(r, -1), rope.reshape(r, -1), _swap_halves(rope).reshape(r, -1)]
    return jnp.concatenate(parts, axis=1).astype(BF16)


def _pick(pref, n):
    if n <= pref:
        return n
    t = pref
    while n % t:
        t //= 2
    return t


class _Tiles(NamedTuple):
    kv_rows: int
    q_rows: int
    proj_rows: int
    conv_cols: int
    gate_cols: int
    attn_q: int
    attn_k: int
    branch_rows: int
    branch_cols: int
    ffn_rows: int
    ffn_cols: int


def _tiles(t, seq, d, conv_dim, d_ff):
    return _Tiles(kv_rows=_pick(256, seq), q_rows=_pick(512, seq), proj_rows=_pick(1024, t),
                  conv_cols=_pick(512, conv_dim), gate_cols=_pick(1024, d),
                  attn_q=_pick(1024, seq), attn_k=_pick(256, seq),
                  branch_rows=_pick(1024, seq), branch_cols=_pick(512, d),
                  ffn_rows=_pick(512, t), ffn_cols=_pick(256, d_ff))


def kernel(x, positions, g_mix, w_in, b_gate, conv_w, g_q_a, w_q_b, g_kv_a, w_kv_b, w_branch,
           w_out, g_ffn, w_ffn_gate, w_ffn_up, w_ffn_down, g_final):
    batch, seq, d = x.shape
    depth = w_in.shape[0]
    t = batch * seq
    conv_dim = conv_w.shape[-1]
    q_rank = g_q_a.shape[-1]
    kv_rank = g_kv_a.shape[-1]
    n_heads = w_q_b.shape[-1] // QK_HEAD_DIM
    qa_col = 3 * conv_dim
    kva_col = qa_col + q_rank
    kr_col = kva_col + kv_rank
    gate_col = kr_col + QK_ROPE_DIM

    tl = _tiles(t, seq, d, conv_dim, w_ffn_gate.shape[-1])

    xf = x.reshape(t, d)
    pos = positions.reshape(t, 1)
    inv_freq = ROPE_THETA ** (-jnp.arange(0, QK_ROPE_DIM, 2, dtype=F32) / QK_ROPE_DIM)
    reps = 2 * LANE // QK_ROPE_DIM
    half = QK_ROPE_DIM // 2
    invf = jnp.tile(inv_freq, reps)[None, :]
    sgn = jnp.tile(jnp.concatenate([-jnp.ones((half,), F32), jnp.ones((half,), F32)]), reps // 2)[None, :]

    for l in range(depth):
        w_in_t = jnp.swapaxes(w_in[l], 0, 1).astype(BF16)
        w_kr_t = jnp.swapaxes(w_in[l][:, kr_col:gate_col], 0, 1)
        w_krs_t = jnp.concatenate([w_kr_t[half:], w_kr_t[:half]], axis=0)
        pad_t = jnp.zeros((LANE - QK_ROPE_DIM, d), F32)
        wkvx_t = jnp.concatenate([w_in_t[kva_col:kr_col],
                                  jnp.concatenate([w_kr_t, pad_t, w_krs_t, pad_t], axis=0).astype(BF16)],
                                 axis=0)
        wq = _q_weights(w_q_b[l], n_heads)
        w_kv = w_kv_b[l].reshape(kv_rank, n_heads, QK_NOPE_DIM + V_HEAD_DIM)
        wkt = w_kv[..., :QK_NOPE_DIM].reshape(kv_rank, n_heads * QK_NOPE_DIM).T.astype(BF16)
        wv = w_kv[..., QK_NOPE_DIM:].reshape(kv_rank, n_heads * V_HEAD_DIM).astype(BF16)
        w_br = w_branch[l].astype(BF16)
        w_o = w_out[l].astype(BF16)
        w_fg = w_ffn_gate[l].astype(BF16)
        w_fu = w_ffn_up[l].astype(BF16)
        w_fd = w_ffn_down[l].astype(BF16)

        h, kt, v = _kv_proj(xf, g_mix[l][None, :], pos, invf, sgn, wkvx_t, g_kv_a[l][None, :], wkt, wv,
                            n_heads, batch, seq, tl.kv_rows)
        cb, u = _conv_proj(h, w_in_t, conv_dim, tl.proj_rows, tl.conv_cols)
        gates = _gate_proj(h, w_in_t, gate_col, b_gate[l][None, :], tl.proj_rows, tl.gate_cols)
        q = _q_proj(h, pos, invf, sgn, w_in_t, qa_col, g_q_a[l][None, :], wq, n_heads, tl.q_rows)
        yb = _attention(q, kt, v, n_heads, batch, seq, tl.attn_q, tl.attn_k)
        m = _branch(cb, u, yb, conv_w[l], gates, w_br, seq, tl.branch_rows, tl.branch_cols)
        xf = _out_proj(m, w_o, xf, tl.proj_rows, tl.gate_cols)
        xf = _ffn(xf, g_ffn[l][None, :], w_fg, w_fu, w_fd, g_final[None, :], l == depth - 1,
                  tl.ffn_rows, tl.ffn_cols)
    return xf.reshape(batch, seq, d)
```

```python
import functools
import math
from typing import NamedTuple

import jax
import jax.numpy as jnp
from jax import lax
from jax.experimental import pallas as pl
from jax.experimental.pallas import tpu as pltpu

F32 = jnp.float32
BF16 = jnp.bfloat16

RMS_EPS = 1e-6
ROPE_THETA = 10000.0
QK_NOPE_DIM = 128
QK_ROPE_DIM = 64
V_HEAD_DIM = 128
QK_HEAD_DIM = QK_NOPE_DIM + QK_ROPE_DIM
Q_SCALE = math.log2(math.e) / math.sqrt(QK_HEAD_DIM)
HEAD_PAD = 256
LANE = 128
HALO_ROWS = 16
CONV_CHUNK = 512
NORM_ROWS = 256
VMEM_LIMIT = 63 * 1024 * 1024


def _cparams(*sem):
    return pltpu.CompilerParams(dimension_semantics=sem, vmem_limit_bytes=VMEM_LIMIT)


def _resident(block_shape, index_map):
    return pl.BlockSpec(block_shape, index_map, pipeline_mode=pl.Buffered(1))


def _dot(a, b):
    return jnp.dot(a, b, preferred_element_type=F32)


def _dot_nt(a, b):
    return lax.dot_general(a, b, (((1,), (1,)), ((), ())), preferred_element_type=F32)


def _rms(x, g):
    inv = lax.rsqrt(jnp.mean(x * x, axis=-1, keepdims=True) + RMS_EPS)
    return x * inv * g


def _conv_proj_kernel(h_ref, wb_ref, wc_ref, wh_ref, cb_ref, u_ref):
    h = h_ref[...]
    cb_ref[...] = _dot_nt(h, wb_ref[...]).astype(cb_ref.dtype)
    u_ref[...] = (_dot_nt(h, wc_ref[...]) * _dot_nt(h, wh_ref[...])).astype(u_ref.dtype)


def _conv_proj(h, w_in_t, conv_dim, tm, tn):
    t, d = h.shape
    nb = conv_dim // tn
    w_spec = lambda off: pl.BlockSpec((tn, d), lambda i, j: (j + off, 0))
    out = jax.ShapeDtypeStruct((t, conv_dim), BF16)
    return pl.pallas_call(
        _conv_proj_kernel,
        grid=(t // tm, nb),
        in_specs=[pl.BlockSpec((tm, d), lambda i, j: (i, 0)),
                  w_spec(0), w_spec(nb), w_spec(2 * nb)],
        out_specs=[pl.BlockSpec((tm, tn), lambda i, j: (i, j))] * 2,
        out_shape=[out, out],
        compiler_params=_cparams("parallel", "arbitrary"),
        name="conv_proj",
    )(h, w_in_t, w_in_t, w_in_t)


def _gate_proj_kernel(h_ref, w_ref, b_ref, o_ref):
    z = _dot_nt(h_ref[...], w_ref[...]) + b_ref[...]
    o_ref[...] = (1.0 / (1.0 + jnp.exp(-z))).astype(o_ref.dtype)


def _gate_proj(h, w_in_t, gate_row, b_gate, tm, tn):
    t, d = h.shape
    n = b_gate.shape[1]
    return pl.pallas_call(
        _gate_proj_kernel,
        grid=(t // tm, n // tn),
        in_specs=[pl.BlockSpec((tm, d), lambda i, j: (i, 0)),
                  pl.BlockSpec((pl.Element(tn), pl.Element(d)),
                               lambda i, j: (pl.multiple_of(gate_row + j * tn, math.gcd(gate_row, tn)), 0)),
                  pl.BlockSpec((1, tn), lambda i, j: (0, j))],
        out_specs=pl.BlockSpec((tm, tn), lambda i, j: (i, j)),
        out_shape=jax.ShapeDtypeStruct((t, n), BF16),
        compiler_params=_cparams("parallel", "arbitrary"),
        name="gate_proj",
    )(h, w_in_t, b_gate)


def _rope_tables(pos_ref, invf_ref, sgn_ref):
    ang = pos_ref[...].astype(F32) * invf_ref[...]
    return jnp.cos(ang), jnp.sin(ang) * sgn_ref[...]


def _q_proj_kernel(n_heads, h_ref, pos_ref, invf_ref, sgn_ref, wqa_ref, gq_ref, wq_ref, q_ref):
    qn = _rms(_dot_nt(h_ref[...], wqa_ref[...]), gq_ref[...]).astype(BF16)
    z = _dot(qn, wq_ref[...])
    rope0 = n_heads * QK_NOPE_DIM
    swap0 = rope0 + n_heads * QK_ROPE_DIM
    cos, sin = _rope_tables(pos_ref, invf_ref, sgn_ref)
    cos, sin = cos * Q_SCALE, sin * Q_SCALE
    first = lax.broadcasted_iota(jnp.int32, (1, LANE), 1) < QK_ROPE_DIM
    for pair in range(n_heads // 2):
        g = pair * LANE
        rot = z[:, rope0 + g:rope0 + g + LANE] * cos + z[:, swap0 + g:swap0 + g + LANE] * sin
        for k, r in enumerate((rot, pltpu.roll(rot, QK_ROPE_DIM, axis=1))):
            hd = 2 * pair + k
            a = hd * HEAD_PAD
            nope = z[:, hd * QK_NOPE_DIM:(hd + 1) * QK_NOPE_DIM] * Q_SCALE
            q_ref[:, a:a + LANE] = nope.astype(q_ref.dtype)
            q_ref[:, a + LANE:a + HEAD_PAD] = jnp.where(first, r, 0.0).astype(q_ref.dtype)


def _q_proj(h, pos, invf, sgn, w_in_t, qa_col, g_q, wq, n_heads, tm):
    t, d = h.shape
    q_rank = g_q.shape[1]
    assert qa_col % q_rank == 0 and n_heads % 2 == 0
    const = lambda i: (0, 0)
    return pl.pallas_call(
        functools.partial(_q_proj_kernel, n_heads),
        grid=(t // tm,),
        in_specs=[pl.BlockSpec((tm, d), lambda i: (i, 0)),
                  pl.BlockSpec((tm, 1), lambda i: (i, 0)),
                  _resident((1, LANE), const),
                  _resident((1, LANE), const),
                  _resident((q_rank, d), lambda i: (qa_col // q_rank, 0)),
                  _resident(g_q.shape, const),
                  _resident(wq.shape, const)],
        out_specs=pl.BlockSpec((tm, n_heads * HEAD_PAD), lambda i: (i, 0)),
        out_shape=jax.ShapeDtypeStruct((t, n_heads * HEAD_PAD), BF16),
        compiler_params=_cparams("parallel"),
        name="q_proj",
    )(h, pos, invf, sgn, w_in_t, g_q, wq)


def _kv_proj_kernel(n_heads, x_ref, gm_ref, pos_ref, invf_ref, sgn_ref, wkvx_ref, gkv_ref,
                    wkt_ref, wv_ref, h_ref, kt_ref, v_ref):
    h = _rms(x_ref[...], gm_ref[...]).astype(h_ref.dtype)
    h_ref[...] = h
    kv_rank = gkv_ref.shape[1]
    z = _dot_nt(h, wkvx_ref[...])
    kvn = _rms(z[:, :kv_rank], gkv_ref[...]).astype(BF16)
    v = _dot(kvn, wv_ref[...])
    lane = lax.broadcasted_iota(jnp.int32, (h.shape[0], LANE), 1)
    ones_col = jnp.where(lane == 0, 1.0, 0.0).astype(v_ref.dtype)
    for hd in range(n_heads):
        a = hd * HEAD_PAD
        v_ref[:, a:a + V_HEAD_DIM] = v[:, hd * V_HEAD_DIM:(hd + 1) * V_HEAD_DIM].astype(v_ref.dtype)
        v_ref[:, a + V_HEAD_DIM:a + HEAD_PAD] = ones_col
    knt = lax.dot_general(wkt_ref[...], kvn, (((1,), (1,)), ((), ())),
                          preferred_element_type=F32)
    cos, sin = _rope_tables(pos_ref, invf_ref, sgn_ref)
    krot = z[:, kv_rank:kv_rank + LANE] * cos + z[:, kv_rank + LANE:] * sin
    krt = krot.T.astype(kt_ref.dtype)
    for hd in range(n_heads):
        kt_ref[0, hd, 0:LANE, :] = knt[hd * LANE:(hd + 1) * LANE, :].astype(kt_ref.dtype)
        kt_ref[0, hd, LANE:HEAD_PAD, :] = krt


def _kv_proj(x, g_mix, pos, invf, sgn, wkvx, g_kv, wkt, wv, n_heads, batch, seq, tm):
    t, d = x.shape
    spt = seq // tm
    const = lambda i: (0, 0)
    return pl.pallas_call(
        functools.partial(_kv_proj_kernel, n_heads),
        grid=(t // tm,),
        in_specs=[pl.BlockSpec((tm, d), lambda i: (i, 0)),
                  _resident(g_mix.shape, const),
                  pl.BlockSpec((tm, 1), lambda i: (i, 0)),
                  _resident((1, LANE), const),
                  _resident((1, LANE), const),
                  _resident(wkvx.shape, const),
                  _resident(g_kv.shape, const),
                  _resident(wkt.shape, const),
                  _resident(wv.shape, const)],
        out_specs=[pl.BlockSpec((tm, d), lambda i: (i, 0)),
                   pl.BlockSpec((1, n_heads, HEAD_PAD, tm), lambda i: (i // spt, 0, 0, i % spt)),
                   pl.BlockSpec((tm, n_heads * HEAD_PAD), lambda i: (i, 0))],
        out_shape=[jax.ShapeDtypeStruct((t, d), BF16),
                   jax.ShapeDtypeStruct((batch, n_heads, HEAD_PAD, seq), BF16),
                   jax.ShapeDtypeStruct((t, n_heads * HEAD_PAD), BF16)],
        compiler_params=_cparams("parallel"),
        name="kv_proj",
    )(x, g_mix, pos, invf, sgn, wkvx, g_kv, wkt, wv)


def _attn_kernel(kc, q_ref, kt_ref, v_ref, o_ref):
    q = q_ref[...]
    tq = q.shape[0]
    seq = kt_ref.shape[-1]
    dv = o_ref.shape[-1]
    m = jnp.full((tq, 1), -jnp.inf, F32)
    acc = jnp.zeros((tq, v_ref.shape[-1]), F32)
    for c0 in range(0, seq, kc):
        s = _dot(q, kt_ref[0, 0, :, c0:c0 + kc])
        m_new = jnp.maximum(m, jnp.max(s, axis=-1, keepdims=True))
        alpha = jnp.exp2(m - m_new)
        p = jnp.exp2(s - m_new).astype(BF16)
        acc = alpha * acc + _dot(p, v_ref[c0:c0 + kc, :])
        m = m_new
    o_ref[...] = (acc[:, :dv] / acc[:, dv:dv + 1]).astype(o_ref.dtype)


def _attention(q, kt, v, n_heads, batch, seq, tq, kc):
    t = q.shape[0]
    qpt = seq // tq
    return pl.pallas_call(
        functools.partial(_attn_kernel, kc),
        grid=(batch, n_heads, qpt),
        in_specs=[pl.BlockSpec((tq, HEAD_PAD), lambda b, hd, i: (b * qpt + i, hd)),
                  pl.BlockSpec((1, 1, HEAD_PAD, seq), lambda b, hd, i: (b, hd, 0, 0)),
                  pl.BlockSpec((seq, HEAD_PAD), lambda b, hd, i: (b, hd))],
        out_specs=pl.BlockSpec((tq, V_HEAD_DIM), lambda b, hd, i: (b * qpt + i, hd)),
        out_shape=jax.ShapeDtypeStruct((t, n_heads * V_HEAD_DIM), BF16),
        compiler_params=_cparams("parallel", "parallel", "arbitrary"),
        name="attention",
    )(q, kt, v)


def _branch_kernel(tiles_per_seq, cchunk, cb_ref, u_ref, up_ref, un_ref, yb_ref, cw_ref,
                   ga_ref, gb_ref, w0_ref, w1_ref, o_ref, ya_ref):
    i = pl.program_id(0)
    j = pl.program_id(1)
    tm, c = u_ref.shape

    def project():
        pa = _dot(ya_ref[...], w0_ref[0])
        pb = _dot(yb_ref[...], w1_ref[0])
        o_ref[...] = (ga_ref[...].astype(F32) * pa + gb_ref[...].astype(F32) * pb).astype(o_ref.dtype)

    @pl.when(j == 0)
    def _():
        keep_prev = (i % tiles_per_seq != 0).astype(F32)
        keep_next = (i % tiles_per_seq != tiles_per_seq - 1).astype(F32)
        row = lax.broadcasted_iota(jnp.int32, (tm, 1), 0)
        for c0 in range(0, c, cchunk):
            cs = slice(c0, c0 + cchunk)
            u = u_ref[:, cs].astype(F32)
            prev_row = up_ref[HALO_ROWS - 1:HALO_ROWS, cs].astype(F32) * keep_prev
            next_row = un_ref[0:1, cs].astype(F32) * keep_next
            u_dn = jnp.where(row == 0, prev_row, pltpu.roll(u, 1, axis=0))
            u_up = jnp.where(row == tm - 1, next_row, pltpu.roll(u, tm - 1, axis=0))
            conv = u_dn * cw_ref[0:1, cs] + u * cw_ref[1:2, cs] + u_up * cw_ref[2:3, cs]
            ya_ref[:, cs] = (cb_ref[:, cs].astype(F32) * conv).astype(ya_ref.dtype)
        project()

    @pl.when(j != 0)
    def _():
        project()


def _branch(cb, u, yb, conv_w, gates, w_br, seq, tm, tn):
    t, c = cb.shape
    d = w_br.shape[2]
    nb = d // tn
    hb = tm // HALO_ROWS
    last_hb = t // HALO_ROWS - 1
    row = lambda i, j: (i, 0)
    return pl.pallas_call(
        functools.partial(_branch_kernel, seq // tm, _pick(CONV_CHUNK, c)),
        grid=(t // tm, nb),
        in_specs=[pl.BlockSpec((tm, c), row),
                  pl.BlockSpec((tm, c), row),
                  pl.BlockSpec((HALO_ROWS, c), lambda i, j: (jnp.maximum(i * hb - 1, 0), 0)),
                  pl.BlockSpec((HALO_ROWS, c), lambda i, j: (jnp.minimum((i + 1) * hb, last_hb), 0)),
                  pl.BlockSpec((tm, c), row),
                  pl.BlockSpec(conv_w.shape, lambda i, j: (0, 0)),
                  pl.BlockSpec((tm, tn), lambda i, j: (i, j)),
                  pl.BlockSpec((tm, tn), lambda i, j: (i, j + nb)),
                  pl.BlockSpec((1, c, tn), lambda i, j: (0, 0, j)),
                  pl.BlockSpec((1, c, tn), lambda i, j: (1, 0, j))],
        out_specs=pl.BlockSpec((tm, tn), lambda i, j: (i, j)),
        out_shape=jax.ShapeDtypeStruct((t, d), BF16),
        scratch_shapes=[pltpu.VMEM((tm, c), BF16)],
        compiler_params=_cparams("parallel", "arbitrary"),
        name="branch",
    )(cb, u, u, u, yb, conv_w, gates, gates, w_br, w_br)


def _out_proj_kernel(m_ref, w_ref, x_ref, o_ref):
    o_ref[...] = x_ref[...] + _dot(m_ref[...], w_ref[...])


def _out_proj(m, w_out, x, tm, tn):
    t, d = m.shape
    n = w_out.shape[1]
    return pl.pallas_call(
        _out_proj_kernel,
        grid=(t // tm, n // tn),
        in_specs=[pl.BlockSpec((tm, d), lambda i, j: (i, 0)),
                  pl.BlockSpec((d, tn), lambda i, j: (0, j)),
                  pl.BlockSpec((tm, tn), lambda i, j: (i, j))],
        out_specs=pl.BlockSpec((tm, tn), lambda i, j: (i, j)),
        out_shape=jax.ShapeDtypeStruct((t, n), F32),
        compiler_params=_cparams("parallel", "arbitrary"),
        name="out_proj",
    )(m, w_out, x)


def _ffn_kernel(final_norm, x_ref, gf_ref, wg_ref, wu_ref, wd_ref, gl_ref, o_ref, h2_ref):
    j = pl.program_id(1)
    last = pl.num_programs(1) - 1
    tm = x_ref.shape[0]
    step = _pick(NORM_ROWS, tm)
    chunks = [slice(r0, r0 + step) for r0 in range(0, tm, step)]

    def accumulate():
        h2 = h2_ref[...]
        g = _dot(h2, wg_ref[...])
        a = (g * (1.0 / (1.0 + jnp.exp(-g)))) * _dot(h2, wu_ref[...])
        o_ref[...] += _dot(a.astype(BF16), wd_ref[...])

    @pl.when(j == 0)
    def _():
        for rs in chunks:
            x = x_ref[rs, :]
            h2_ref[rs, :] = _rms(x, gf_ref[...]).astype(h2_ref.dtype)
            o_ref[rs, :] = x
        accumulate()

    @pl.when(jnp.logical_and(j > 0, j < last))
    def _():
        accumulate()

    @pl.when(j == last)
    def _():
        accumulate()
        if final_norm:
            for rs in chunks:
                o_ref[rs, :] = _rms(o_ref[rs, :], gl_ref[...])


def _ffn(x1, g_ffn, w_g, w_u, w_d, g_final, final_norm, tm, tf):
    t, d = x1.shape
    f = w_g.shape[1]
    nf = f // tf
    assert f % tf == 0 and nf >= 2
    return pl.pallas_call(
        functools.partial(_ffn_kernel, final_norm),
        grid=(t // tm, nf),
        in_specs=[_resident((tm, d), lambda i, j: (i, 0)),
                  pl.BlockSpec((1, d), lambda i, j: (0, 0)),
                  pl.BlockSpec((d, tf), lambda i, j: (0, j)),
                  pl.BlockSpec((d, tf), lambda i, j: (0, j)),
                  pl.BlockSpec((tf, d), lambda i, j: (j, 0)),
                  pl.BlockSpec((1, d), lambda i, j: (0, 0))],
        out_specs=_resident((tm, d), lambda i, j: (i, 0)),
        out_shape=jax.ShapeDtypeStruct((t, d), F32),
        scratch_shapes=[pltpu.VMEM((tm, d), BF16)],
        compiler_params=_cparams("parallel", "arbitrary"),
        name="ffn",
    )(x1, g_ffn, w_g, w_u, w_d, g_final)


def _swap_halves(w):
    half = w.shape[-1] // 2
    return jnp.concatenate([w[..., half:], w[..., :half]], axis=-1)


def _q_weights(w_q_b, n_heads):
    r = w_q_b.shape[0]
    w = w_q_b.reshape(r, n_heads, QK_HEAD_DIM)
    nope, rope = w[..., :QK_NOPE_DIM], w[..., QK_NOPE_DIM:]
    parts = [nope.reshape(r, -1), rope.reshape(r, -1), _swap_halves(rope).reshape(r, -1)]
    return jnp.concatenate(parts, axis=1).astype(BF16)


def _pick(pref, n):
    if n <= pref:
        return n
    t = pref
    while n % t:
        t //= 2
    return t


class _Tiles(NamedTuple):
    kv_rows: int
    q_rows: int
    proj_rows: int
    conv_cols: int
    gate_cols: int
    attn_q: int
    attn_k: int
    branch_rows: int
    branch_cols: int
    ffn_rows: int
    ffn_cols: int


def _tiles(t, seq, d, conv_dim, d_ff):
    return _Tiles(kv_rows=_pick(256, seq), q_rows=_pick(512, seq), proj_rows=_pick(1024, t),
                  conv_cols=_pick(512, conv_dim), gate_cols=_pick(1024, d),
                  attn_q=_pick(1024, seq), attn_k=_pick(256, seq),
                  branch_rows=_pick(1024, seq), branch_cols=_pick(512, d),
                  ffn_rows=_pick(1024, t), ffn_cols=_pick(256, d_ff))


def kernel(x, positions, g_mix, w_in, b_gate, conv_w, g_q_a, w_q_b, g_kv_a, w_kv_b, w_branch,
           w_out, g_ffn, w_ffn_gate, w_ffn_up, w_ffn_down, g_final):
    batch, seq, d = x.shape
    depth = w_in.shape[0]
    t = batch * seq
    conv_dim = conv_w.shape[-1]
    q_rank = g_q_a.shape[-1]
    kv_rank = g_kv_a.shape[-1]
    n_heads = w_q_b.shape[-1] // QK_HEAD_DIM
    qa_col = 3 * conv_dim
    kva_col = qa_col + q_rank
    kr_col = kva_col + kv_rank
    gate_col = kr_col + QK_ROPE_DIM

    tl = _tiles(t, seq, d, conv_dim, w_ffn_gate.shape[-1])

    xf = x.reshape(t, d)
    pos = positions.reshape(t, 1)
    inv_freq = ROPE_THETA ** (-jnp.arange(0, QK_ROPE_DIM, 2, dtype=F32) / QK_ROPE_DIM)
    reps = 2 * LANE // QK_ROPE_DIM
    half = QK_ROPE_DIM // 2
    invf = jnp.tile(inv_freq, reps)[None, :]
    sgn = jnp.tile(jnp.concatenate([-jnp.ones((half,), F32), jnp.ones((half,), F32)]), reps // 2)[None, :]

    for l in range(depth):
        w_in_t = jnp.swapaxes(w_in[l], 0, 1).astype(BF16)
        w_kr_t = jnp.swapaxes(w_in[l][:, kr_col:gate_col], 0, 1)
        w_krs_t = jnp.concatenate([w_kr_t[half:], w_kr_t[:half]], axis=0)
        pad_t = jnp.zeros((LANE - QK_ROPE_DIM, d), F32)
        wkvx_t = jnp.concatenate([w_in_t[kva_col:kr_col],
                                  jnp.concatenate([w_kr_t, pad_t, w_krs_t, pad_t], axis=0).astype(BF16)],
                                 axis=0)
        wq = _q_weights(w_q_b[l], n_heads)
        w_kv = w_kv_b[l].reshape(kv_rank, n_heads, QK_NOPE_DIM + V_HEAD_DIM)
        wkt = w_kv[..., :QK_NOPE_DIM].reshape(kv_rank, n_heads * QK_NOPE_DIM).T.astype(BF16)
        wv = w_kv[..., QK_NOPE_DIM:].reshape(kv_rank, n_heads * V_HEAD_DIM).astype(BF16)
        w_br = w_branch[l].astype(BF16)
        w_o = w_out[l].astype(BF16)
        w_fg = w_ffn_gate[l].astype(BF16)
        w_fu = w_ffn_up[l].astype(BF16)
        w_fd = w_ffn_down[l].astype(BF16)

        h, kt, v = _kv_proj(xf, g_mix[l][None, :], pos, invf, sgn, wkvx_t, g_kv_a[l][None, :], wkt, wv,
                            n_heads, batch, seq, tl.kv_rows)
        cb, u = _conv_proj(h, w_in_t, conv_dim, tl.proj_rows, tl.conv_cols)
        gates = _gate_proj(h, w_in_t, gate_col, b_gate[l][None, :], tl.proj_rows, tl.gate_cols)
        q = _q_proj(h, pos, invf, sgn, w_in_t, qa_col, g_q_a[l][None, :], wq, n_heads, tl.q_rows)
        yb = _attention(q, kt, v, n_heads, batch, seq, tl.attn_q, tl.attn_k)
        m = _branch(cb, u, yb, conv_w[l], gates, w_br, seq, tl.branch_rows, tl.branch_cols)
        xf = _out_proj(m, w_o, xf, tl.proj_rows, tl.gate_cols)
        xf = _ffn(xf, g_ffn[l][None, :], w_fg, w_fu, w_fd, g_final[None, :], l == depth - 1,
                  tl.ffn_rows, tl.ffn_cols)
    return xf.reshape(batch, seq, d)
```

```python
import functools
import math
from typing import NamedTuple

import jax
import jax.numpy as jnp
from jax import lax
from jax.experimental import pallas as pl
from jax.experimental.pallas import tpu as pltpu

F32 = jnp.float32
BF16 = jnp.bfloat16

RMS_EPS = 1e-6
ROPE_THETA = 10000.0
QK_NOPE_DIM = 128
QK_ROPE_DIM = 64
V_HEAD_DIM = 128
QK_HEAD_DIM = QK_NOPE_DIM + QK_ROPE_DIM
Q_SCALE = math.log2(math.e) / math.sqrt(QK_HEAD_DIM)
HEAD_PAD = 256
LANE = 128
BF16_ROWS = 16
HALO_ROWS = BF16_ROWS
CONV_CHUNK = 512
NORM_ROWS = 256
VMEM_LIMIT = 63 * 1024 * 1024


def _cparams(*sem):
    return pltpu.CompilerParams(dimension_semantics=sem, vmem_limit_bytes=VMEM_LIMIT)


def _resident(block_shape, index_map):
    return pl.BlockSpec(block_shape, index_map, pipeline_mode=pl.Buffered(1))


def _dot(a, b):
    return jnp.dot(a, b, preferred_element_type=F32)


def _dot_nt(a, b):
    return lax.dot_general(a, b, (((1,), (1,)), ((), ())), preferred_element_type=F32)


def _rms(x, g):
    inv = lax.rsqrt(jnp.mean(x * x, axis=-1, keepdims=True) + RMS_EPS)
    return x * inv * g


def _conv_proj_kernel(h_ref, wb_ref, wc_ref, wh_ref, cb_ref, u_ref):
    h = h_ref[...]
    cb_ref[...] = _dot_nt(h, wb_ref[...]).astype(cb_ref.dtype)
    u_ref[...] = (_dot_nt(h, wc_ref[...]) * _dot_nt(h, wh_ref[...])).astype(u_ref.dtype)


def _conv_proj(h, w_in_t, conv_dim, tm, tn):
    t, d = h.shape
    nb = conv_dim // tn
    w_spec = lambda off: pl.BlockSpec((tn, d), lambda i, j: (j + off, 0))
    out = jax.ShapeDtypeStruct((t, conv_dim), BF16)
    return pl.pallas_call(
        _conv_proj_kernel,
        grid=(t // tm, nb),
        in_specs=[pl.BlockSpec((tm, d), lambda i, j: (i, 0)),
                  w_spec(0), w_spec(nb), w_spec(2 * nb)],
        out_specs=[pl.BlockSpec((tm, tn), lambda i, j: (i, j))] * 2,
        out_shape=[out, out],
        compiler_params=_cparams("parallel", "arbitrary"),
        name="conv_proj",
    )(h, w_in_t, w_in_t, w_in_t)


def _gate_proj_kernel(h_ref, w_ref, b_ref, o_ref):
    z = _dot_nt(h_ref[...], w_ref[...]) + b_ref[...]
    o_ref[...] = (1.0 / (1.0 + jnp.exp(-z))).astype(o_ref.dtype)


def _gate_proj(h, w_in_t, gate_row, b_gate, tm, tn):
    t, d = h.shape
    n = b_gate.shape[1]
    return pl.pallas_call(
        _gate_proj_kernel,
        grid=(t // tm, n // tn),
        in_specs=[pl.BlockSpec((tm, d), lambda i, j: (i, 0)),
                  pl.BlockSpec((pl.Element(tn), pl.Element(d)),
                               lambda i, j: (pl.multiple_of(gate_row + j * tn, math.gcd(gate_row, tn)), 0)),
                  pl.BlockSpec((1, tn), lambda i, j: (0, j))],
        out_specs=pl.BlockSpec((tm, tn), lambda i, j: (i, j)),
        out_shape=jax.ShapeDtypeStruct((t, n), BF16),
        compiler_params=_cparams("parallel", "arbitrary"),
        name="gate_proj",
    )(h, w_in_t, b_gate)


def _rope_tables(pos_ref, invf_ref, sgn_ref):
    ang = pos_ref[...].astype(F32) * invf_ref[...]
    return jnp.cos(ang), jnp.sin(ang) * sgn_ref[...]


def _q_proj_kernel(n_heads, h_ref, pos_ref, invf_ref, sgn_ref, wqa_ref, gq_ref, wq_ref, q_ref):
    qn = _rms(_dot_nt(h_ref[...], wqa_ref[...]), gq_ref[...]).astype(BF16)
    z = _dot(qn, wq_ref[...])
    rope0 = n_heads * QK_NOPE_DIM
    swap0 = rope0 + n_heads * QK_ROPE_DIM
    cos, sin = _rope_tables(pos_ref, invf_ref, sgn_ref)
    cos, sin = cos * Q_SCALE, sin * Q_SCALE
    first = lax.broadcasted_iota(jnp.int32, (1, LANE), 1) < QK_ROPE_DIM
    for pair in range(n_heads // 2):
        g = pair * LANE
        rot = z[:, rope0 + g:rope0 + g + LANE] * cos + z[:, swap0 + g:swap0 + g + LANE] * sin
        for k, r in enumerate((rot, pltpu.roll(rot, QK_ROPE_DIM, axis=1))):
            hd = 2 * pair + k
            a = hd * HEAD_PAD
            nope = z[:, hd * QK_NOPE_DIM:(hd + 1) * QK_NOPE_DIM] * Q_SCALE
            q_ref[:, a:a + LANE] = nope.astype(q_ref.dtype)
            q_ref[:, a + LANE:a + HEAD_PAD] = jnp.where(first, r, 0.0).astype(q_ref.dtype)


def _q_proj(h, pos, invf, sgn, w_in_t, qa_col, g_q, wq, n_heads, tm):
    t, d = h.shape
    q_rank = g_q.shape[1]
    assert qa_col % q_rank == 0 and n_heads % 2 == 0
    const = lambda i: (0, 0)
    return pl.pallas_call(
        functools.partial(_q_proj_kernel, n_heads),
        grid=(t // tm,),
        in_specs=[pl.BlockSpec((tm, d), lambda i: (i, 0)),
                  pl.BlockSpec((tm, 1), lambda i: (i, 0)),
                  _resident((1, LANE), const),
                  _resident((1, LANE), const),
                  _resident((q_rank, d), lambda i: (qa_col // q_rank, 0)),
                  _resident(g_q.shape, const),
                  _resident(wq.shape, const)],
        out_specs=pl.BlockSpec((tm, n_heads * HEAD_PAD), lambda i: (i, 0)),
        out_shape=jax.ShapeDtypeStruct((t, n_heads * HEAD_PAD), BF16),
        compiler_params=_cparams("parallel"),
        name="q_proj",
    )(h, pos, invf, sgn, w_in_t, g_q, wq)


def _kv_proj_kernel(n_heads, x_ref, gm_ref, pos_ref, invf_ref, sgn_ref, wkvx_ref, gkv_ref,
                    wkt_ref, wv_ref, h_ref, kt_ref, v_ref):
    h = _rms(x_ref[...], gm_ref[...]).astype(h_ref.dtype)
    h_ref[...] = h
    kv_rank = gkv_ref.shape[1]
    z = _dot_nt(h, wkvx_ref[...])
    kvn = _rms(z[:, :kv_rank], gkv_ref[...]).astype(BF16)
    v = _dot(kvn, wv_ref[...])
    lane = lax.broadcasted_iota(jnp.int32, (h.shape[0], LANE), 1)
    ones_col = jnp.where(lane == 0, 1.0, 0.0).astype(v_ref.dtype)
    for hd in range(n_heads):
        a = hd * HEAD_PAD
        v_ref[:, a:a + V_HEAD_DIM] = v[:, hd * V_HEAD_DIM:(hd + 1) * V_HEAD_DIM].astype(v_ref.dtype)
        v_ref[:, a + V_HEAD_DIM:a + HEAD_PAD] = ones_col
    knt = lax.dot_general(wkt_ref[...], kvn, (((1,), (1,)), ((), ())),
                          preferred_element_type=F32)
    cos, sin = _rope_tables(pos_ref, invf_ref, sgn_ref)
    krot = z[:, kv_rank:kv_rank + LANE] * cos + z[:, kv_rank + LANE:] * sin
    krt = krot.T.astype(kt_ref.dtype)
    for hd in range(n_heads):
        kt_ref[0, hd, 0:LANE, :] = knt[hd * LANE:(hd + 1) * LANE, :].astype(kt_ref.dtype)
        kt_ref[0, hd, LANE:HEAD_PAD, :] = krt


def _kv_proj(x, g_mix, pos, invf, sgn, wkvx, g_kv, wkt, wv, n_heads, batch, seq, tm):
    t, d = x.shape
    spt = seq // tm
    const = lambda i: (0, 0)
    return pl.pallas_call(
        functools.partial(_kv_proj_kernel, n_heads),
        grid=(t // tm,),
        in_specs=[pl.BlockSpec((tm, d), lambda i: (i, 0)),
                  _resident(g_mix.shape, const),
                  pl.BlockSpec((tm, 1), lambda i: (i, 0)),
                  _resident((1, LANE), const),
                  _resident((1, LANE), const),
                  _resident(wkvx.shape, const),
                  _resident(g_kv.shape, const),
                  _resident(wkt.shape, const),
                  _resident(wv.shape, const)],
        out_specs=[pl.BlockSpec((tm, d), lambda i: (i, 0)),
                   pl.BlockSpec((1, n_heads, HEAD_PAD, tm), lambda i: (i // spt, 0, 0, i % spt)),
                   pl.BlockSpec((tm, n_heads * HEAD_PAD), lambda i: (i, 0))],
        out_shape=[jax.ShapeDtypeStruct((t, d), BF16),
                   jax.ShapeDtypeStruct((batch, n_heads, HEAD_PAD, seq), BF16),
                   jax.ShapeDtypeStruct((t, n_heads * HEAD_PAD), BF16)],
        compiler_params=_cparams("parallel"),
        name="kv_proj",
    )(x, g_mix, pos, invf, sgn, wkvx, g_kv, wkt, wv)


def _attn_kernel(kc, cast_blocks, q_ref, kt_ref, v_ref, *refs):
    n_cast = len(cast_blocks)
    w_refs, o_ref, wo_refs = refs[:n_cast], refs[n_cast], refs[n_cast + 1:]
    step = (pl.program_id(0) * pl.num_programs(1) + pl.program_id(1)) * pl.num_programs(2) + pl.program_id(2)
    for w_ref, wo_ref, nblk in zip(w_refs, wo_refs, cast_blocks):
        @pl.when(step < nblk)
        def _(w_ref=w_ref, wo_ref=wo_ref):
            wo_ref[...] = w_ref[...].astype(wo_ref.dtype)

    q = q_ref[...]
    tq = q.shape[0]
    seq = kt_ref.shape[-1]
    dv = o_ref.shape[-1]
    m = jnp.full((tq, 1), -jnp.inf, F32)
    acc = jnp.zeros((tq, v_ref.shape[-1]), F32)
    for c0 in range(0, seq, kc):
        s = _dot(q, kt_ref[0, 0, :, c0:c0 + kc])
        m_new = jnp.maximum(m, jnp.max(s, axis=-1, keepdims=True))
        alpha = jnp.exp2(m - m_new)
        p = jnp.exp2(s - m_new).astype(BF16)
        acc = alpha * acc + _dot(p, v_ref[c0:c0 + kc, :])
        m = m_new
    o_ref[...] = (acc[:, :dv] / acc[:, dv:dv + 1]).astype(o_ref.dtype)


def _cast_rows(rows, steps):
    rb = BF16_ROWS
    while rows % rb or rows // rb > steps:
        rb += BF16_ROWS
    return rb


def _attention(q, kt, v, cast_weights, n_heads, batch, seq, tq, kc):
    t = q.shape[0]
    qpt = seq // tq
    steps = batch * n_heads * qpt
    step = lambda b, hd, i: (b * n_heads + hd) * qpt + i
    cast_rows = [_cast_rows(w.shape[0], steps) for w in cast_weights]
    cast_blocks = tuple(w.shape[0] // rb for w, rb in zip(cast_weights, cast_rows))
    def w_specs():
        return [pl.BlockSpec((rb, w.shape[1]),
                             lambda b, hd, i, nblk=nblk: (jnp.minimum(step(b, hd, i), nblk - 1), 0))
                for w, rb, nblk in zip(cast_weights, cast_rows, cast_blocks)]
    outs = pl.pallas_call(
        functools.partial(_attn_kernel, kc, cast_blocks),
        grid=(batch, n_heads, qpt),
        in_specs=[pl.BlockSpec((tq, HEAD_PAD), lambda b, hd, i: (b * qpt + i, hd)),
                  pl.BlockSpec((1, 1, HEAD_PAD, seq), lambda b, hd, i: (b, hd, 0, 0)),
                  pl.BlockSpec((seq, HEAD_PAD), lambda b, hd, i: (b, hd))] + w_specs(),
        out_specs=[pl.BlockSpec((tq, V_HEAD_DIM), lambda b, hd, i: (b * qpt + i, hd))] + w_specs(),
        out_shape=[jax.ShapeDtypeStruct((t, n_heads * V_HEAD_DIM), BF16)]
                  + [jax.ShapeDtypeStruct(w.shape, BF16) for w in cast_weights],
        compiler_params=_cparams("arbitrary", "arbitrary", "arbitrary"),
        name="attention",
    )(q, kt, v, *cast_weights)
    return outs[0], outs[1:]


def _branch_kernel(tiles_per_seq, cchunk, cb_ref, u_ref, up_ref, un_ref, yb_ref, cw_ref,
                   ga_ref, gb_ref, w0_ref, w1_ref, o_ref, ya_ref):
    i = pl.program_id(0)
    j = pl.program_id(1)
    tm, c = u_ref.shape

    def project():
        pa = _dot(ya_ref[...], w0_ref[0])
        pb = _dot(yb_ref[...], w1_ref[0])
        o_ref[...] = (ga_ref[...].astype(F32) * pa + gb_ref[...].astype(F32) * pb).astype(o_ref.dtype)

    @pl.when(j == 0)
    def _():
        keep_prev = (i % tiles_per_seq != 0).astype(F32)
        keep_next = (i % tiles_per_seq != tiles_per_seq - 1).astype(F32)
        row = lax.broadcasted_iota(jnp.int32, (tm, 1), 0)
        for c0 in range(0, c, cchunk):
            cs = slice(c0, c0 + cchunk)
            u = u_ref[:, cs].astype(F32)
            prev_row = up_ref[HALO_ROWS - 1:HALO_ROWS, cs].astype(F32) * keep_prev
            next_row = un_ref[0:1, cs].astype(F32) * keep_next
            u_dn = jnp.where(row == 0, prev_row, pltpu.roll(u, 1, axis=0))
            u_up = jnp.where(row == tm - 1, next_row, pltpu.roll(u, tm - 1, axis=0))
            conv = u_dn * cw_ref[0:1, cs] + u * cw_ref[1:2, cs] + u_up * cw_ref[2:3, cs]
            ya_ref[:, cs] = (cb_ref[:, cs].astype(F32) * conv).astype(ya_ref.dtype)
        project()

    @pl.when(j != 0)
    def _():
        project()


def _branch(cb, u, yb, conv_w, gates, w_br, seq, tm, tn):
    t, c = cb.shape
    d = w_br.shape[2]
    nb = d // tn
    hb = tm // HALO_ROWS
    last_hb = t // HALO_ROWS - 1
    row = lambda i, j: (i, 0)
    return pl.pallas_call(
        functools.partial(_branch_kernel, seq // tm, _pick(CONV_CHUNK, c)),
        grid=(t // tm, nb),
        in_specs=[pl.BlockSpec((tm, c), row),
                  pl.BlockSpec((tm, c), row),
                  pl.BlockSpec((HALO_ROWS, c), lambda i, j: (jnp.maximum(i * hb - 1, 0), 0)),
                  pl.BlockSpec((HALO_ROWS, c), lambda i, j: (jnp.minimum((i + 1) * hb, last_hb), 0)),
                  pl.BlockSpec((tm, c), row),
                  pl.BlockSpec(conv_w.shape, lambda i, j: (0, 0)),
                  pl.BlockSpec((tm, tn), lambda i, j: (i, j)),
                  pl.BlockSpec((tm, tn), lambda i, j: (i, j + nb)),
                  pl.BlockSpec((1, c, tn), lambda i, j: (0, 0, j)),
                  pl.BlockSpec((1, c, tn), lambda i, j: (1, 0, j))],
        out_specs=pl.BlockSpec((tm, tn), lambda i, j: (i, j)),
        out_shape=jax.ShapeDtypeStruct((t, d), BF16),
        scratch_shapes=[pltpu.VMEM((tm, c), BF16)],
        compiler_params=_cparams("parallel", "arbitrary"),
        name="branch",
    )(cb, u, u, u, yb, conv_w, gates, gates, w_br, w_br)


def _out_proj_kernel(m_ref, w_ref, x_ref, o_ref):
    o_ref[...] = x_ref[...] + _dot(m_ref[...], w_ref[...])


def _out_proj(m, w_out, x, tm, tn):
    t, d = m.shape
    n = w_out.shape[1]
    return pl.pallas_call(
        _out_proj_kernel,
        grid=(t // tm, n // tn),
        in_specs=[pl.BlockSpec((tm, d), lambda i, j: (i, 0)),
                  pl.BlockSpec((d, tn), lambda i, j: (0, j)),
                  pl.BlockSpec((tm, tn), lambda i, j: (i, j))],
        out_specs=pl.BlockSpec((tm, tn), lambda i, j: (i, j)),
        out_shape=jax.ShapeDtypeStruct((t, n), F32),
        compiler_params=_cparams("parallel", "arbitrary"),
        name="out_proj",
    )(m, w_out, x)


def _ffn_kernel(final_norm, x_ref, gf_ref, wg_ref, wu_ref, wd_ref, gl_ref, o_ref, h2_ref):
    j = pl.program_id(1)
    last = pl.num_programs(1) - 1
    tm = x_ref.shape[0]
    step = _pick(NORM_ROWS, tm)
    chunks = [slice(r0, r0 + step) for r0 in range(0, tm, step)]

    def accumulate():
        h2 = h2_ref[...]
        g = _dot(h2, wg_ref[...])
        a = (g * (1.0 / (1.0 + jnp.exp(-g)))) * _dot(h2, wu_ref[...])
        o_ref[...] += _dot(a.astype(BF16), wd_ref[...])

    @pl.when(j == 0)
    def _():
        for rs in chunks:
            x = x_ref[rs, :]
            h2_ref[rs, :] = _rms(x, gf_ref[...]).astype(h2_ref.dtype)
            o_ref[rs, :] = x
        accumulate()

    @pl.when(jnp.logical_and(j > 0, j < last))
    def _():
        accumulate()

    @pl.when(j == last)
    def _():
        accumulate()
        if final_norm:
            for rs in chunks:
                o_ref[rs, :] = _rms(o_ref[rs, :], gl_ref[...])


def _ffn(x1, g_ffn, w_g, w_u, w_d, g_final, final_norm, tm, tf):
    t, d = x1.shape
    f = w_g.shape[1]
    nf = f // tf
    assert f % tf == 0 and nf >= 2
    return pl.pallas_call(
        functools.partial(_ffn_kernel, final_norm),
        grid=(t // tm, nf),
        in_specs=[_resident((tm, d), lambda i, j: (i, 0)),
                  pl.BlockSpec((1, d), lambda i, j: (0, 0)),
                  pl.BlockSpec((d, tf), lambda i, j: (0, j)),
                  pl.BlockSpec((d, tf), lambda i, j: (0, j)),
                  pl.BlockSpec((tf, d), lambda i, j: (j, 0)),
                  pl.BlockSpec((1, d), lambda i, j: (0, 0))],
        out_specs=_resident((tm, d), lambda i, j: (i, 0)),
        out_shape=jax.ShapeDtypeStruct((t, d), F32),
        scratch_shapes=[pltpu.VMEM((tm, d), BF16)],
        compiler_params=_cparams("parallel", "arbitrary"),
        name="ffn",
    )(x1, g_ffn, w_g, w_u, w_d, g_final)


def _swap_halves(w):
    half = w.shape[-1] // 2
    return jnp.concatenate([w[..., half:], w[..., :half]], axis=-1)


def _q_weights(w_q_b, n_heads):
    r = w_q_b.shape[0]
    w = w_q_b.reshape(r, n_heads, QK_HEAD_DIM)
    nope, rope = w[..., :QK_NOPE_DIM], w[..., QK_NOPE_DIM:]
    parts = [nope.reshape(r, -1), rope.reshape(r, -1), _swap_halves(rope).reshape(r, -1)]
    return jnp.concatenate(parts, axis=1).astype(BF16)


def _pick(pref, n):
    if n <= pref:
        return n
    t = pref
    while n % t:
        t //= 2
    return t


class _Tiles(NamedTuple):
    kv_rows: int
    q_rows: int
    proj_rows: int
    conv_cols: int
    gate_cols: int
    attn_q: int
    attn_k: int
    branch_rows: int
    branch_cols: int
    ffn_rows: int
    ffn_cols: int


def _tiles(t, seq, d, conv_dim, d_ff):
    return _Tiles(kv_rows=_pick(256, seq), q_rows=_pick(512, seq), proj_rows=_pick(1024, t),
                  conv_cols=_pick(512, conv_dim), gate_cols=_pick(1024, d),
                  attn_q=_pick(1024, seq), attn_k=_pick(256, seq),
                  branch_rows=_pick(1024, seq), branch_cols=_pick(512, d),
                  ffn_rows=_pick(1024, t), ffn_cols=_pick(256, d_ff))


def kernel(x, positions, g_mix, w_in, b_gate, conv_w, g_q_a, w_q_b, g_kv_a, w_kv_b, w_branch,
           w_out, g_ffn, w_ffn_gate, w_ffn_up, w_ffn_down, g_final):
    batch, seq, d = x.shape
    depth = w_in.shape[0]
    t = batch * seq
    conv_dim = conv_w.shape[-1]
    q_rank = g_q_a.shape[-1]
    kv_rank = g_kv_a.shape[-1]
    n_heads = w_q_b.shape[-1] // QK_HEAD_DIM
    qa_col = 3 * conv_dim
    kva_col = qa_col + q_rank
    kr_col = kva_col + kv_rank
    gate_col = kr_col + QK_ROPE_DIM

    tl = _tiles(t, seq, d, conv_dim, w_ffn_gate.shape[-1])

    xf = x.reshape(t, d)
    pos = positions.reshape(t, 1)
    inv_freq = ROPE_THETA ** (-jnp.arange(0, QK_ROPE_DIM, 2, dtype=F32) / QK_ROPE_DIM)
    reps = 2 * LANE // QK_ROPE_DIM
    half = QK_ROPE_DIM // 2
    invf = jnp.tile(inv_freq, reps)[None, :]
    sgn = jnp.tile(jnp.concatenate([-jnp.ones((half,), F32), jnp.ones((half,), F32)]), reps // 2)[None, :]

    for l in range(depth):
        w_in_t = jnp.swapaxes(w_in[l], 0, 1).astype(BF16)
        w_kr_t = jnp.swapaxes(w_in[l][:, kr_col:gate_col], 0, 1)
        w_krs_t = jnp.concatenate([w_kr_t[half:], w_kr_t[:half]], axis=0)
        pad_t = jnp.zeros((LANE - QK_ROPE_DIM, d), F32)
        wkvx_t = jnp.concatenate([w_in_t[kva_col:kr_col],
                                  jnp.concatenate([w_kr_t, pad_t, w_krs_t, pad_t], axis=0).astype(BF16)],
                                 axis=0)
        wq = _q_weights(w_q_b[l], n_heads)
        w_kv = w_kv_b[l].reshape(kv_rank, n_heads, QK_NOPE_DIM + V_HEAD_DIM)
        wkt = w_kv[..., :QK_NOPE_DIM].reshape(kv_rank, n_heads * QK_NOPE_DIM).T.astype(BF16)
        wv = w_kv[..., QK_NOPE_DIM:].reshape(kv_rank, n_heads * V_HEAD_DIM).astype(BF16)
        w_br = w_branch[l].astype(BF16)
        w_o = w_out[l].astype(BF16)

        h, kt, v = _kv_proj(xf, g_mix[l][None, :], pos, invf, sgn, wkvx_t, g_kv_a[l][None, :], wkt, wv,
                            n_heads, batch, seq, tl.kv_rows)
        cb, u = _conv_proj(h, w_in_t, conv_dim, tl.proj_rows, tl.conv_cols)
        gates = _gate_proj(h, w_in_t, gate_col, b_gate[l][None, :], tl.proj_rows, tl.gate_cols)
        q = _q_proj(h, pos, invf, sgn, w_in_t, qa_col, g_q_a[l][None, :], wq, n_heads, tl.q_rows)
        yb, (w_fg, w_fu, w_fd) = _attention(q, kt, v, (w_ffn_gate[l], w_ffn_up[l], w_ffn_down[l]),
                                            n_heads, batch, seq, tl.attn_q, tl.attn_k)
        m = _branch(cb, u, yb, conv_w[l], gates, w_br, seq, tl.branch_rows, tl.branch_cols)
        xf = _out_proj(m, w_o, xf, tl.proj_rows, tl.gate_cols)
        xf = _ffn(xf, g_ffn[l][None, :], w_fg, w_fu, w_fd, g_final[None, :], l == depth - 1,
                  tl.ffn_rows, tl.ffn_cols)
    return xf.reshape(batch, seq, d)
```

```python
import functools
import math
from typing import NamedTuple

import jax
import jax.numpy as jnp
from jax import lax
from jax.experimental import pallas as pl
from jax.experimental.pallas import tpu as pltpu

F32 = jnp.float32
BF16 = jnp.bfloat16

RMS_EPS = 1e-6
ROPE_THETA = 10000.0
QK_NOPE_DIM = 128
QK_ROPE_DIM = 64
V_HEAD_DIM = 128
QK_HEAD_DIM = QK_NOPE_DIM + QK_ROPE_DIM
Q_SCALE = math.log2(math.e) / math.sqrt(QK_HEAD_DIM)
HEAD_PAD = 256
LANE = 128
BF16_ROWS = 16
HALO_ROWS = BF16_ROWS
CONV_CHUNK = 512
NORM_ROWS = 256
VMEM_LIMIT = 63 * 1024 * 1024


def _cparams(*sem):
    return pltpu.CompilerParams(dimension_semantics=sem, vmem_limit_bytes=VMEM_LIMIT)


def _resident(block_shape, index_map):
    return pl.BlockSpec(block_shape, index_map, pipeline_mode=pl.Buffered(1))


def _dot(a, b):
    return jnp.dot(a, b, preferred_element_type=F32)


def _dot_nt(a, b):
    return lax.dot_general(a, b, (((1,), (1,)), ((), ())), preferred_element_type=F32)


def _rms(x, g):
    inv = lax.rsqrt(jnp.mean(x * x, axis=-1, keepdims=True) + RMS_EPS)
    return x * inv * g


def _cast_rows(rows, steps):
    rb = BF16_ROWS
    while rows % rb or rows // rb > steps:
        rb += BF16_ROWS
    return rb


def _cast_plan(weights, steps, step_of):
    rows = [_cast_rows(w.shape[0], steps) for w in weights]
    blocks = tuple(w.shape[0] // rb for w, rb in zip(weights, rows))

    def specs():
        return [pl.BlockSpec((rb, w.shape[1]),
                             lambda *idx, nblk=nblk: (jnp.minimum(step_of(*idx), nblk - 1), 0))
                for w, rb, nblk in zip(weights, rows, blocks)]
    return blocks, specs, [jax.ShapeDtypeStruct(w.shape, BF16) for w in weights]


def _cast_step(step, blocks, w_refs, wo_refs):
    for w_ref, wo_ref, nblk in zip(w_refs, wo_refs, blocks):
        @pl.when(step < nblk)
        def _(w_ref=w_ref, wo_ref=wo_ref):
            wo_ref[...] = w_ref[...].astype(wo_ref.dtype)


def _conv_proj_kernel(h_ref, wb_ref, wc_ref, wh_ref, cb_ref, u_ref):
    h = h_ref[...]
    cb_ref[...] = _dot_nt(h, wb_ref[...]).astype(cb_ref.dtype)
    u_ref[...] = (_dot_nt(h, wc_ref[...]) * _dot_nt(h, wh_ref[...])).astype(u_ref.dtype)


def _conv_proj(h, w_in_t, conv_dim, tm, tn):
    t, d = h.shape
    nb = conv_dim // tn
    w_spec = lambda off: pl.BlockSpec((tn, d), lambda i, j: (j + off, 0))
    out = jax.ShapeDtypeStruct((t, conv_dim), BF16)
    return pl.pallas_call(
        _conv_proj_kernel,
        grid=(t // tm, nb),
        in_specs=[pl.BlockSpec((tm, d), lambda i, j: (i, 0)),
                  w_spec(0), w_spec(nb), w_spec(2 * nb)],
        out_specs=[pl.BlockSpec((tm, tn), lambda i, j: (i, j))] * 2,
        out_shape=[out, out],
        compiler_params=_cparams("parallel", "arbitrary"),
        name="conv_proj",
    )(h, w_in_t, w_in_t, w_in_t)


def _gate_proj_kernel(cast_blocks, h_ref, w_ref, b_ref, *refs):
    n_cast = len(cast_blocks)
    w_refs, o_ref, wo_refs = refs[:n_cast], refs[n_cast], refs[n_cast + 1:]
    _cast_step(pl.program_id(0) * pl.num_programs(1) + pl.program_id(1), cast_blocks, w_refs, wo_refs)
    z = _dot_nt(h_ref[...], w_ref[...]) + b_ref[...]
    o_ref[...] = (1.0 / (1.0 + jnp.exp(-z))).astype(o_ref.dtype)


def _gate_proj(h, w_in_t, gate_row, b_gate, cast_weights, tm, tn):
    t, d = h.shape
    n = b_gate.shape[1]
    nb = n // tn
    cast_blocks, w_specs, w_shapes = _cast_plan(cast_weights, (t // tm) * nb, lambda i, j: i * nb + j)
    outs = pl.pallas_call(
        functools.partial(_gate_proj_kernel, cast_blocks),
        grid=(t // tm, nb),
        in_specs=[pl.BlockSpec((tm, d), lambda i, j: (i, 0)),
                  pl.BlockSpec((pl.Element(tn), pl.Element(d)),
                               lambda i, j: (pl.multiple_of(gate_row + j * tn, math.gcd(gate_row, tn)), 0)),
                  pl.BlockSpec((1, tn), lambda i, j: (0, j))] + w_specs(),
        out_specs=[pl.BlockSpec((tm, tn), lambda i, j: (i, j))] + w_specs(),
        out_shape=[jax.ShapeDtypeStruct((t, n), BF16)] + w_shapes,
        compiler_params=_cparams("arbitrary", "arbitrary"),
        name="gate_proj",
    )(h, w_in_t, b_gate, *cast_weights)
    return outs[0], outs[1:]


def _rope_tables(pos_ref, invf_ref, sgn_ref):
    ang = pos_ref[...].astype(F32) * invf_ref[...]
    return jnp.cos(ang), jnp.sin(ang) * sgn_ref[...]


def _q_proj_kernel(n_heads, h_ref, pos_ref, invf_ref, sgn_ref, wqa_ref, gq_ref, wq_ref, q_ref):
    qn = _rms(_dot_nt(h_ref[...], wqa_ref[...]), gq_ref[...]).astype(BF16)
    z = _dot(qn, wq_ref[...])
    rope0 = n_heads * QK_NOPE_DIM
    swap0 = rope0 + n_heads * QK_ROPE_DIM
    cos, sin = _rope_tables(pos_ref, invf_ref, sgn_ref)
    cos, sin = cos * Q_SCALE, sin * Q_SCALE
    first = lax.broadcasted_iota(jnp.int32, (1, LANE), 1) < QK_ROPE_DIM
    for pair in range(n_heads // 2):
        g = pair * LANE
        rot = z[:, rope0 + g:rope0 + g + LANE] * cos + z[:, swap0 + g:swap0 + g + LANE] * sin
        for k, r in enumerate((rot, pltpu.roll(rot, QK_ROPE_DIM, axis=1))):
            hd = 2 * pair + k
            a = hd * HEAD_PAD
            nope = z[:, hd * QK_NOPE_DIM:(hd + 1) * QK_NOPE_DIM] * Q_SCALE
            q_ref[:, a:a + LANE] = nope.astype(q_ref.dtype)
            q_ref[:, a + LANE:a + HEAD_PAD] = jnp.where(first, r, 0.0).astype(q_ref.dtype)


def _q_proj(h, pos, invf, sgn, w_in_t, qa_col, g_q, wq, n_heads, tm):
    t, d = h.shape
    q_rank = g_q.shape[1]
    assert qa_col % q_rank == 0 and n_heads % 2 == 0
    const = lambda i: (0, 0)
    return pl.pallas_call(
        functools.partial(_q_proj_kernel, n_heads),
        grid=(t // tm,),
        in_specs=[pl.BlockSpec((tm, d), lambda i: (i, 0)),
                  pl.BlockSpec((tm, 1), lambda i: (i, 0)),
                  _resident((1, LANE), const),
                  _resident((1, LANE), const),
                  _resident((q_rank, d), lambda i: (qa_col // q_rank, 0)),
                  _resident(g_q.shape, const),
                  _resident(wq.shape, const)],
        out_specs=pl.BlockSpec((tm, n_heads * HEAD_PAD), lambda i: (i, 0)),
        out_shape=jax.ShapeDtypeStruct((t, n_heads * HEAD_PAD), BF16),
        compiler_params=_cparams("parallel"),
        name="q_proj",
    )(h, pos, invf, sgn, w_in_t, g_q, wq)


def _kv_proj_kernel(n_heads, x_ref, gm_ref, pos_ref, invf_ref, sgn_ref, wkvx_ref, gkv_ref,
                    wkt_ref, wv_ref, h_ref, kt_ref, v_ref):
    h = _rms(x_ref[...], gm_ref[...]).astype(h_ref.dtype)
    h_ref[...] = h
    kv_rank = gkv_ref.shape[1]
    z = _dot_nt(h, wkvx_ref[...])
    kvn = _rms(z[:, :kv_rank], gkv_ref[...]).astype(BF16)
    v = _dot(kvn, wv_ref[...])
    lane = lax.broadcasted_iota(jnp.int32, (h.shape[0], LANE), 1)
    ones_col = jnp.where(lane == 0, 1.0, 0.0).astype(v_ref.dtype)
    for hd in range(n_heads):
        a = hd * HEAD_PAD
        v_ref[:, a:a + V_HEAD_DIM] = v[:, hd * V_HEAD_DIM:(hd + 1) * V_HEAD_DIM].astype(v_ref.dtype)
        v_ref[:, a + V_HEAD_DIM:a + HEAD_PAD] = ones_col
    knt = lax.dot_general(wkt_ref[...], kvn, (((1,), (1,)), ((), ())),
                          preferred_element_type=F32)
    cos, sin = _rope_tables(pos_ref, invf_ref, sgn_ref)
    krot = z[:, kv_rank:kv_rank + LANE] * cos + z[:, kv_rank + LANE:] * sin
    krt = krot.T.astype(kt_ref.dtype)
    for hd in range(n_heads):
        kt_ref[0, hd, 0:LANE, :] = knt[hd * LANE:(hd + 1) * LANE, :].astype(kt_ref.dtype)
        kt_ref[0, hd, LANE:HEAD_PAD, :] = krt


def _kv_proj(x, g_mix, pos, invf, sgn, wkvx, g_kv, wkt, wv, n_heads, batch, seq, tm):
    t, d = x.shape
    spt = seq // tm
    const = lambda i: (0, 0)
    return pl.pallas_call(
        functools.partial(_kv_proj_kernel, n_heads),
        grid=(t // tm,),
        in_specs=[pl.BlockSpec((tm, d), lambda i: (i, 0)),
                  _resident(g_mix.shape, const),
                  pl.BlockSpec((tm, 1), lambda i: (i, 0)),
                  _resident((1, LANE), const),
                  _resident((1, LANE), const),
                  _resident(wkvx.shape, const),
                  _resident(g_kv.shape, const),
                  _resident(wkt.shape, const),
                  _resident(wv.shape, const)],
        out_specs=[pl.BlockSpec((tm, d), lambda i: (i, 0)),
                   pl.BlockSpec((1, n_heads, HEAD_PAD, tm), lambda i: (i // spt, 0, 0, i % spt)),
                   pl.BlockSpec((tm, n_heads * HEAD_PAD), lambda i: (i, 0))],
        out_shape=[jax.ShapeDtypeStruct((t, d), BF16),
                   jax.ShapeDtypeStruct((batch, n_heads, HEAD_PAD, seq), BF16),
                   jax.ShapeDtypeStruct((t, n_heads * HEAD_PAD), BF16)],
        compiler_params=_cparams("parallel"),
        name="kv_proj",
    )(x, g_mix, pos, invf, sgn, wkvx, g_kv, wkt, wv)


def _attn_kernel(kc, cast_blocks, q_ref, kt_ref, v_ref, *refs):
    n_cast = len(cast_blocks)
    w_refs, o_ref, wo_refs = refs[:n_cast], refs[n_cast], refs[n_cast + 1:]
    step = (pl.program_id(0) * pl.num_programs(1) + pl.program_id(1)) * pl.num_programs(2) + pl.program_id(2)
    _cast_step(step, cast_blocks, w_refs, wo_refs)

    q = q_ref[...]
    tq = q.shape[0]
    seq = kt_ref.shape[-1]
    dv = o_ref.shape[-1]
    m = jnp.full((tq, 1), -jnp.inf, F32)
    acc = jnp.zeros((tq, v_ref.shape[-1]), F32)
    for c0 in range(0, seq, kc):
        s = _dot(q, kt_ref[0, 0, :, c0:c0 + kc])
        m_new = jnp.maximum(m, jnp.max(s, axis=-1, keepdims=True))
        alpha = jnp.exp2(m - m_new)
        p = jnp.exp2(s - m_new).astype(BF16)
        acc = alpha * acc + _dot(p, v_ref[c0:c0 + kc, :])
        m = m_new
    o_ref[...] = (acc[:, :dv] / acc[:, dv:dv + 1]).astype(o_ref.dtype)


def _attention(q, kt, v, cast_weights, n_heads, batch, seq, tq, kc):
    t = q.shape[0]
    qpt = seq // tq
    cast_blocks, w_specs, w_shapes = _cast_plan(
        cast_weights, batch * n_heads * qpt, lambda b, hd, i: (b * n_heads + hd) * qpt + i)
    outs = pl.pallas_call(
        functools.partial(_attn_kernel, kc, cast_blocks),
        grid=(batch, n_heads, qpt),
        in_specs=[pl.BlockSpec((tq, HEAD_PAD), lambda b, hd, i: (b * qpt + i, hd)),
                  pl.BlockSpec((1, 1, HEAD_PAD, seq), lambda b, hd, i: (b, hd, 0, 0)),
                  pl.BlockSpec((seq, HEAD_PAD), lambda b, hd, i: (b, hd))] + w_specs(),
        out_specs=[pl.BlockSpec((tq, V_HEAD_DIM), lambda b, hd, i: (b * qpt + i, hd))] + w_specs(),
        out_shape=[jax.ShapeDtypeStruct((t, n_heads * V_HEAD_DIM), BF16)] + w_shapes,
        compiler_params=_cparams("arbitrary", "arbitrary", "arbitrary"),
        name="attention",
    )(q, kt, v, *cast_weights)
    return outs[0], outs[1:]


def _branch_kernel(tiles_per_seq, cchunk, cb_ref, u_ref, up_ref, un_ref, yb_ref, cw_ref,
                   ga_ref, gb_ref, w0_ref, w1_ref, o_ref, ya_ref):
    i = pl.program_id(0)
    j = pl.program_id(1)
    tm, c = u_ref.shape

    def project():
        pa = _dot(ya_ref[...], w0_ref[0])
        pb = _dot(yb_ref[...], w1_ref[0])
        o_ref[...] = (ga_ref[...].astype(F32) * pa + gb_ref[...].astype(F32) * pb).astype(o_ref.dtype)

    @pl.when(j == 0)
    def _():
        keep_prev = (i % tiles_per_seq != 0).astype(F32)
        keep_next = (i % tiles_per_seq != tiles_per_seq - 1).astype(F32)
        row = lax.broadcasted_iota(jnp.int32, (tm, 1), 0)
        for c0 in range(0, c, cchunk):
            cs = slice(c0, c0 + cchunk)
            u = u_ref[:, cs].astype(F32)
            prev_row = up_ref[HALO_ROWS - 1:HALO_ROWS, cs].astype(F32) * keep_prev
            next_row = un_ref[0:1, cs].astype(F32) * keep_next
            u_dn = jnp.where(row == 0, prev_row, pltpu.roll(u, 1, axis=0))
            u_up = jnp.where(row == tm - 1, next_row, pltpu.roll(u, tm - 1, axis=0))
            conv = u_dn * cw_ref[0:1, cs] + u * cw_ref[1:2, cs] + u_up * cw_ref[2:3, cs]
            ya_ref[:, cs] = (cb_ref[:, cs].astype(F32) * conv).astype(ya_ref.dtype)
        project()

    @pl.when(j != 0)
    def _():
        project()


def _branch(cb, u, yb, conv_w, gates, w_br, seq, tm, tn):
    t, c = cb.shape
    d = w_br.shape[2]
    nb = d // tn
    hb = tm // HALO_ROWS
    last_hb = t // HALO_ROWS - 1
    row = lambda i, j: (i, 0)
    return pl.pallas_call(
        functools.partial(_branch_kernel, seq // tm, _pick(CONV_CHUNK, c)),
        grid=(t // tm, nb),
        in_specs=[pl.BlockSpec((tm, c), row),
                  pl.BlockSpec((tm, c), row),
                  pl.BlockSpec((HALO_ROWS, c), lambda i, j: (jnp.maximum(i * hb - 1, 0), 0)),
                  pl.BlockSpec((HALO_ROWS, c), lambda i, j: (jnp.minimum((i + 1) * hb, last_hb), 0)),
                  pl.BlockSpec((tm, c), row),
                  pl.BlockSpec(conv_w.shape, lambda i, j: (0, 0)),
                  pl.BlockSpec((tm, tn), lambda i, j: (i, j)),
                  pl.BlockSpec((tm, tn), lambda i, j: (i, j + nb)),
                  pl.BlockSpec((1, c, tn), lambda i, j: (0, 0, j)),
                  pl.BlockSpec((1, c, tn), lambda i, j: (1, 0, j))],
        out_specs=pl.BlockSpec((tm, tn), lambda i, j: (i, j)),
        out_shape=jax.ShapeDtypeStruct((t, d), BF16),
        scratch_shapes=[pltpu.VMEM((tm, c), BF16)],
        compiler_params=_cparams("parallel", "arbitrary"),
        name="branch",
    )(cb, u, u, u, yb, conv_w, gates, gates, w_br, w_br)


def _out_proj_kernel(m_ref, w_ref, x_ref, o_ref):
    o_ref[...] = x_ref[...] + _dot(m_ref[...], w_ref[...])


def _out_proj(m, w_out, x, tm, tn):
    t, d = m.shape
    n = w_out.shape[1]
    return pl.pallas_call(
        _out_proj_kernel,
        grid=(t // tm, n // tn),
        in_specs=[pl.BlockSpec((tm, d), lambda i, j: (i, 0)),
                  pl.BlockSpec((d, tn), lambda i, j: (0, j)),
                  pl.BlockSpec((tm, tn), lambda i, j: (i, j))],
        out_specs=pl.BlockSpec((tm, tn), lambda i, j: (i, j)),
        out_shape=jax.ShapeDtypeStruct((t, n), F32),
        compiler_params=_cparams("parallel", "arbitrary"),
        name="out_proj",
    )(m, w_out, x)


def _ffn_kernel(final_norm, x_ref, gf_ref, wg_ref, wu_ref, wd_ref, gl_ref, o_ref, h2_ref):
    j = pl.program_id(1)
    last = pl.num_programs(1) - 1
    tm = x_ref.shape[0]
    step = _pick(NORM_ROWS, tm)
    chunks = [slice(r0, r0 + step) for r0 in range(0, tm, step)]

    def accumulate():
        h2 = h2_ref[...]
        g = _dot(h2, wg_ref[...])
        a = (g * (1.0 / (1.0 + jnp.exp(-g)))) * _dot(h2, wu_ref[...])
        o_ref[...] += _dot(a.astype(BF16), wd_ref[...])

    @pl.when(j == 0)
    def _():
        for rs in chunks:
            x = x_ref[rs, :]
            h2_ref[rs, :] = _rms(x, gf_ref[...]).astype(h2_ref.dtype)
            o_ref[rs, :] = x
        accumulate()

    @pl.when(jnp.logical_and(j > 0, j < last))
    def _():
        accumulate()

    @pl.when(j == last)
    def _():
        accumulate()
        if final_norm:
            for rs in chunks:
                o_ref[rs, :] = _rms(o_ref[rs, :], gl_ref[...])


def _ffn(x1, g_ffn, w_g, w_u, w_d, g_final, final_norm, tm, tf):
    t, d = x1.shape
    f = w_g.shape[1]
    nf = f // tf
    assert f % tf == 0 and nf >= 2
    return pl.pallas_call(
        functools.partial(_ffn_kernel, final_norm),
        grid=(t // tm, nf),
        in_specs=[_resident((tm, d), lambda i, j: (i, 0)),
                  pl.BlockSpec((1, d), lambda i, j: (0, 0)),
                  pl.BlockSpec((d, tf), lambda i, j: (0, j)),
                  pl.BlockSpec((d, tf), lambda i, j: (0, j)),
                  pl.BlockSpec((tf, d), lambda i, j: (j, 0)),
                  pl.BlockSpec((1, d), lambda i, j: (0, 0))],
        out_specs=_resident((tm, d), lambda i, j: (i, 0)),
        out_shape=jax.ShapeDtypeStruct((t, d), F32),
        scratch_shapes=[pltpu.VMEM((tm, d), BF16)],
        compiler_params=_cparams("parallel", "arbitrary"),
        name="ffn",
    )(x1, g_ffn, w_g, w_u, w_d, g_final)


def _swap_halves(w):
    half = w.shape[-1] // 2
    return jnp.concatenate([w[..., half:], w[..., :half]], axis=-1)


def _q_weights(w_q_b, n_heads):
    r = w_q_b.shape[0]
    w = w_q_b.reshape(r, n_heads, QK_HEAD_DIM)
    nope, rope = w[..., :QK_NOPE_DIM], w[..., QK_NOPE_DIM:]
    parts = [nope.reshape(r, -1), rope.reshape(r, -1), _swap_halves(rope).reshape(r, -1)]
    return jnp.concatenate(parts, axis=1).astype(BF16)


def _pick(pref, n):
    if n <= pref:
        return n
    t = pref
    while n % t:
        t //= 2
    return t


class _Tiles(NamedTuple):
    kv_rows: int
    q_rows: int
    proj_rows: int
    conv_cols: int
    gate_cols: int
    attn_q: int
    attn_k: int
    branch_rows: int
    branch_cols: int
    ffn_rows: int
    ffn_cols: int


def _tiles(t, seq, d, conv_dim, d_ff):
    return _Tiles(kv_rows=_pick(256, seq), q_rows=_pick(512, seq), proj_rows=_pick(1024, t),
                  conv_cols=_pick(512, conv_dim), gate_cols=_pick(1024, d),
                  attn_q=_pick(1024, seq), attn_k=_pick(256, seq),
                  branch_rows=_pick(1024, seq), branch_cols=_pick(512, d),
                  ffn_rows=_pick(1024, t), ffn_cols=_pick(256, d_ff))


def kernel(x, positions, g_mix, w_in, b_gate, conv_w, g_q_a, w_q_b, g_kv_a, w_kv_b, w_branch,
           w_out, g_ffn, w_ffn_gate, w_ffn_up, w_ffn_down, g_final):
    batch, seq, d = x.shape
    depth = w_in.shape[0]
    t = batch * seq
    conv_dim = conv_w.shape[-1]
    q_rank = g_q_a.shape[-1]
    kv_rank = g_kv_a.shape[-1]
    n_heads = w_q_b.shape[-1] // QK_HEAD_DIM
    qa_col = 3 * conv_dim
    kva_col = qa_col + q_rank
    kr_col = kva_col + kv_rank
    gate_col = kr_col + QK_ROPE_DIM

    tl = _tiles(t, seq, d, conv_dim, w_ffn_gate.shape[-1])

    xf = x.reshape(t, d)
    pos = positions.reshape(t, 1)
    inv_freq = ROPE_THETA ** (-jnp.arange(0, QK_ROPE_DIM, 2, dtype=F32) / QK_ROPE_DIM)
    reps = 2 * LANE // QK_ROPE_DIM
    half = QK_ROPE_DIM // 2
    invf = jnp.tile(inv_freq, reps)[None, :]
    sgn = jnp.tile(jnp.concatenate([-jnp.ones((half,), F32), jnp.ones((half,), F32)]), reps // 2)[None, :]

    for l in range(depth):
        w_in_t = jnp.swapaxes(w_in[l], 0, 1).astype(BF16)
        w_kr_t = jnp.swapaxes(w_in[l][:, kr_col:gate_col], 0, 1)
        w_krs_t = jnp.concatenate([w_kr_t[half:], w_kr_t[:half]], axis=0)
        pad_t = jnp.zeros((LANE - QK_ROPE_DIM, d), F32)
        wkvx_t = jnp.concatenate([w_in_t[kva_col:kr_col],
                                  jnp.concatenate([w_kr_t, pad_t, w_krs_t, pad_t], axis=0).astype(BF16)],
                                 axis=0)
        wq = _q_weights(w_q_b[l], n_heads)
        w_kv = w_kv_b[l].reshape(kv_rank, n_heads, QK_NOPE_DIM + V_HEAD_DIM)
        wkt = w_kv[..., :QK_NOPE_DIM].reshape(kv_rank, n_heads * QK_NOPE_DIM).T.astype(BF16)
        wv = w_kv[..., QK_NOPE_DIM:].reshape(kv_rank, n_heads * V_HEAD_DIM).astype(BF16)

        h, kt, v = _kv_proj(xf, g_mix[l][None, :], pos, invf, sgn, wkvx_t, g_kv_a[l][None, :], wkt, wv,
                            n_heads, batch, seq, tl.kv_rows)
        cb, u = _conv_proj(h, w_in_t, conv_dim, tl.proj_rows, tl.conv_cols)
        gates, (w_br, w_o) = _gate_proj(h, w_in_t, gate_col, b_gate[l][None, :],
                                        (w_branch[l].reshape(-1, d), w_out[l]), tl.proj_rows, tl.gate_cols)
        w_br = w_br.reshape(w_branch.shape[1:])
        q = _q_proj(h, pos, invf, sgn, w_in_t, qa_col, g_q_a[l][None, :], wq, n_heads, tl.q_rows)
        yb, (w_fg, w_fu, w_fd) = _attention(q, kt, v, (w_ffn_gate[l], w_ffn_up[l], w_ffn_down[l]),
                                            n_heads, batch, seq, tl.attn_q, tl.attn_k)
        m = _branch(cb, u, yb, conv_w[l], gates, w_br, seq, tl.branch_rows, tl.branch_cols)
        xf = _out_proj(m, w_o, xf, tl.proj_rows, tl.gate_cols)
        xf = _ffn(xf, g_ffn[l][None, :], w_fg, w_fu, w_fd, g_final[None, :], l == depth - 1,
                  tl.ffn_rows, tl.ffn_cols)
    return xf.reshape(batch, seq, d)
```

```python
import functools
import math
from typing import NamedTuple

import jax
import jax.numpy as jnp
from jax import lax
from jax.experimental import pallas as pl
from jax.experimental.pallas import tpu as pltpu

F32 = jnp.float32
BF16 = jnp.bfloat16

RMS_EPS = 1e-6
ROPE_THETA = 10000.0
QK_NOPE_DIM = 128
QK_ROPE_DIM = 64
V_HEAD_DIM = 128
QK_HEAD_DIM = QK_NOPE_DIM + QK_ROPE_DIM
Q_SCALE = math.log2(math.e) / math.sqrt(QK_HEAD_DIM)
HEAD_PAD = 256
LANE = 128
BF16_ROWS = 16
HALO_ROWS = BF16_ROWS
CONV_CHUNK = 512
NORM_ROWS = 256
VMEM_LIMIT = 63 * 1024 * 1024


def _cparams(*sem):
    return pltpu.CompilerParams(dimension_semantics=sem, vmem_limit_bytes=VMEM_LIMIT)


def _resident(block_shape, index_map):
    return pl.BlockSpec(block_shape, index_map, pipeline_mode=pl.Buffered(1))


def _dot(a, b):
    return jnp.dot(a, b, preferred_element_type=F32)


def _dot_nt(a, b):
    return lax.dot_general(a, b, (((1,), (1,)), ((), ())), preferred_element_type=F32)


def _rms(x, g):
    inv = lax.rsqrt(jnp.mean(x * x, axis=-1, keepdims=True) + RMS_EPS)
    return x * inv * g


def _cast_rows(rows, steps):
    rb = BF16_ROWS
    while rows % rb or rows // rb > steps:
        rb += BF16_ROWS
    return rb


def _cast_plan(weights, steps, step_of):
    rows = [_cast_rows(w.shape[0], steps) for w in weights]
    blocks = tuple(w.shape[0] // rb for w, rb in zip(weights, rows))

    def specs():
        return [pl.BlockSpec((rb, w.shape[1]),
                             lambda *idx, nblk=nblk: (jnp.minimum(step_of(*idx), nblk - 1), 0))
                for w, rb, nblk in zip(weights, rows, blocks)]
    return blocks, specs, [jax.ShapeDtypeStruct(w.shape, BF16) for w in weights]


def _cast_step(step, blocks, w_refs, wo_refs):
    for w_ref, wo_ref, nblk in zip(w_refs, wo_refs, blocks):
        @pl.when(step < nblk)
        def _(w_ref=w_ref, wo_ref=wo_ref):
            wo_ref[...] = w_ref[...].astype(wo_ref.dtype)


def _conv_proj_kernel(h_ref, wb_ref, wc_ref, wh_ref, cb_ref, u_ref):
    h = h_ref[...]
    cb_ref[...] = _dot_nt(h, wb_ref[...]).astype(cb_ref.dtype)
    u_ref[...] = (_dot_nt(h, wc_ref[...]) * _dot_nt(h, wh_ref[...])).astype(u_ref.dtype)


def _conv_proj(h, w_in_t, conv_dim, tm, tn):
    t, d = h.shape
    nb = conv_dim // tn
    w_spec = lambda off: pl.BlockSpec((tn, d), lambda i, j: (j + off, 0))
    out = jax.ShapeDtypeStruct((t, conv_dim), BF16)
    return pl.pallas_call(
        _conv_proj_kernel,
        grid=(t // tm, nb),
        in_specs=[pl.BlockSpec((tm, d), lambda i, j: (i, 0)),
                  w_spec(0), w_spec(nb), w_spec(2 * nb)],
        out_specs=[pl.BlockSpec((tm, tn), lambda i, j: (i, j))] * 2,
        out_shape=[out, out],
        compiler_params=_cparams("parallel", "arbitrary"),
        name="conv_proj",
    )(h, w_in_t, w_in_t, w_in_t)


def _gate_proj_kernel(cast_blocks, h_ref, w_ref, b_ref, *refs):
    n_cast = len(cast_blocks)
    w_refs, o_ref, wo_refs = refs[:n_cast], refs[n_cast], refs[n_cast + 1:]
    _cast_step(pl.program_id(0) * pl.num_programs(1) + pl.program_id(1), cast_blocks, w_refs, wo_refs)
    z = _dot_nt(h_ref[...], w_ref[...]) + b_ref[...]
    o_ref[...] = (1.0 / (1.0 + jnp.exp(-z))).astype(o_ref.dtype)


def _gate_proj(h, w_in_t, gate_row, b_gate, cast_weights, tm, tn):
    t, d = h.shape
    n = b_gate.shape[1]
    nb = n // tn
    cast_blocks, w_specs, w_shapes = _cast_plan(cast_weights, (t // tm) * nb, lambda i, j: i * nb + j)
    outs = pl.pallas_call(
        functools.partial(_gate_proj_kernel, cast_blocks),
        grid=(t // tm, nb),
        in_specs=[pl.BlockSpec((tm, d), lambda i, j: (i, 0)),
                  pl.BlockSpec((pl.Element(tn), pl.Element(d)),
                               lambda i, j: (pl.multiple_of(gate_row + j * tn, math.gcd(gate_row, tn)), 0)),
                  pl.BlockSpec((1, tn), lambda i, j: (0, j))] + w_specs(),
        out_specs=[pl.BlockSpec((tm, tn), lambda i, j: (i, j))] + w_specs(),
        out_shape=[jax.ShapeDtypeStruct((t, n), BF16)] + w_shapes,
        compiler_params=_cparams("arbitrary", "arbitrary"),
        name="gate_proj",
    )(h, w_in_t, b_gate, *cast_weights)
    return outs[0], outs[1:]


def _rope_tables(pos_ref, invf_ref, sgn_ref):
    ang = pos_ref[...].astype(F32) * invf_ref[...]
    return jnp.cos(ang), jnp.sin(ang) * sgn_ref[...]


def _q_proj_kernel(n_heads, h_ref, pos_ref, invf_ref, sgn_ref, wqa_ref, gq_ref, wq_ref, q_ref):
    qn = _rms(_dot_nt(h_ref[...], wqa_ref[...]), gq_ref[...]).astype(BF16)
    z = _dot(qn, wq_ref[...])
    rope0 = n_heads * QK_NOPE_DIM
    swap0 = rope0 + n_heads * QK_ROPE_DIM
    cos, sin = _rope_tables(pos_ref, invf_ref, sgn_ref)
    cos, sin = cos * Q_SCALE, sin * Q_SCALE
    first = lax.broadcasted_iota(jnp.int32, (1, LANE), 1) < QK_ROPE_DIM
    for pair in range(n_heads // 2):
        g = pair * LANE
        rot = z[:, rope0 + g:rope0 + g + LANE] * cos + z[:, swap0 + g:swap0 + g + LANE] * sin
        for k, r in enumerate((rot, pltpu.roll(rot, QK_ROPE_DIM, axis=1))):
            hd = 2 * pair + k
            a = hd * HEAD_PAD
            nope = z[:, hd * QK_NOPE_DIM:(hd + 1) * QK_NOPE_DIM] * Q_SCALE
            q_ref[:, a:a + LANE] = nope.astype(q_ref.dtype)
            q_ref[:, a + LANE:a + HEAD_PAD] = jnp.where(first, r, 0.0).astype(q_ref.dtype)


def _q_proj(h, pos, invf, sgn, w_in_t, qa_col, g_q, wq, n_heads, tm):
    t, d = h.shape
    q_rank = g_q.shape[1]
    assert qa_col % q_rank == 0 and n_heads % 2 == 0
    const = lambda i: (0, 0)
    return pl.pallas_call(
        functools.partial(_q_proj_kernel, n_heads),
        grid=(t // tm,),
        in_specs=[pl.BlockSpec((tm, d), lambda i: (i, 0)),
                  pl.BlockSpec((tm, 1), lambda i: (i, 0)),
                  _resident((1, LANE), const),
                  _resident((1, LANE), const),
                  _resident((q_rank, d), lambda i: (qa_col // q_rank, 0)),
                  _resident(g_q.shape, const),
                  _resident(wq.shape, const)],
        out_specs=pl.BlockSpec((tm, n_heads * HEAD_PAD), lambda i: (i, 0)),
        out_shape=jax.ShapeDtypeStruct((t, n_heads * HEAD_PAD), BF16),
        compiler_params=_cparams("parallel"),
        name="q_proj",
    )(h, pos, invf, sgn, w_in_t, g_q, wq)


def _kv_proj_kernel(n_heads, x_ref, gm_ref, pos_ref, invf_ref, sgn_ref, wkvx_ref, gkv_ref,
                    wkt_ref, wv_ref, h_ref, kt_ref, v_ref):
    h = _rms(x_ref[...], gm_ref[...]).astype(h_ref.dtype)
    h_ref[...] = h
    kv_rank = gkv_ref.shape[1]
    z = _dot_nt(h, wkvx_ref[...])
    kvn = _rms(z[:, :kv_rank], gkv_ref[...]).astype(BF16)
    v_ref[...] = _dot(kvn, wv_ref[...]).astype(v_ref.dtype)
    knt = lax.dot_general(wkt_ref[...], kvn, (((1,), (1,)), ((), ())),
                          preferred_element_type=F32)
    cos, sin = _rope_tables(pos_ref, invf_ref, sgn_ref)
    krot = z[:, kv_rank:kv_rank + LANE] * cos + z[:, kv_rank + LANE:] * sin
    krt = krot.T.astype(kt_ref.dtype)
    for hd in range(n_heads):
        kt_ref[0, hd, 0:LANE, :] = knt[hd * LANE:(hd + 1) * LANE, :].astype(kt_ref.dtype)
        kt_ref[0, hd, LANE:HEAD_PAD, :] = krt


def _kv_proj(x, g_mix, pos, invf, sgn, wkvx, g_kv, wkt, wv, n_heads, batch, seq, tm):
    t, d = x.shape
    spt = seq // tm
    const = lambda i: (0, 0)
    return pl.pallas_call(
        functools.partial(_kv_proj_kernel, n_heads),
        grid=(t // tm,),
        in_specs=[pl.BlockSpec((tm, d), lambda i: (i, 0)),
                  _resident(g_mix.shape, const),
                  pl.BlockSpec((tm, 1), lambda i: (i, 0)),
                  _resident((1, LANE), const),
                  _resident((1, LANE), const),
                  _resident(wkvx.shape, const),
                  _resident(g_kv.shape, const),
                  _resident(wkt.shape, const),
                  _resident(wv.shape, const)],
        out_specs=[pl.BlockSpec((tm, d), lambda i: (i, 0)),
                   pl.BlockSpec((1, n_heads, HEAD_PAD, tm), lambda i: (i // spt, 0, 0, i % spt)),
                   pl.BlockSpec((tm, n_heads * V_HEAD_DIM), lambda i: (i, 0))],
        out_shape=[jax.ShapeDtypeStruct((t, d), BF16),
                   jax.ShapeDtypeStruct((batch, n_heads, HEAD_PAD, seq), BF16),
                   jax.ShapeDtypeStruct((t, n_heads * V_HEAD_DIM), BF16)],
        compiler_params=_cparams("parallel"),
        name="kv_proj",
    )(x, g_mix, pos, invf, sgn, wkvx, g_kv, wkt, wv)


def _attn_kernel(kc, cast_blocks, q_ref, kt_ref, v_ref, *refs):
    n_cast = len(cast_blocks)
    w_refs, o_ref = refs[:n_cast], refs[n_cast]
    wo_refs, vx_ref = refs[n_cast + 1:2 * n_cast + 1], refs[2 * n_cast + 1]
    step = (pl.program_id(0) * pl.num_programs(1) + pl.program_id(1)) * pl.num_programs(2) + pl.program_id(2)
    _cast_step(step, cast_blocks, w_refs, wo_refs)

    dv = o_ref.shape[-1]

    @pl.when(pl.program_id(2) == 0)
    def _():
        lane = lax.broadcasted_iota(jnp.int32, (vx_ref.shape[0], vx_ref.shape[1] - dv), 1)
        vx_ref[:, :dv] = v_ref[...]
        vx_ref[:, dv:] = jnp.where(lane == 0, 1.0, 0.0).astype(vx_ref.dtype)

    q = q_ref[...]
    tq = q.shape[0]
    seq = kt_ref.shape[-1]
    m = jnp.full((tq, 1), -jnp.inf, F32)
    acc = jnp.zeros((tq, vx_ref.shape[-1]), F32)
    for c0 in range(0, seq, kc):
        s = _dot(q, kt_ref[0, 0, :, c0:c0 + kc])
        m_new = jnp.maximum(m, jnp.max(s, axis=-1, keepdims=True))
        alpha = jnp.exp2(m - m_new)
        p = jnp.exp2(s - m_new).astype(BF16)
        acc = alpha * acc + _dot(p, vx_ref[c0:c0 + kc, :])
        m = m_new
    o_ref[...] = (acc[:, :dv] / acc[:, dv:dv + 1]).astype(o_ref.dtype)


def _attention(q, kt, v, cast_weights, n_heads, batch, seq, tq, kc):
    t = q.shape[0]
    qpt = seq // tq
    cast_blocks, w_specs, w_shapes = _cast_plan(
        cast_weights, batch * n_heads * qpt, lambda b, hd, i: (b * n_heads + hd) * qpt + i)
    outs = pl.pallas_call(
        functools.partial(_attn_kernel, kc, cast_blocks),
        grid=(batch, n_heads, qpt),
        in_specs=[pl.BlockSpec((tq, HEAD_PAD), lambda b, hd, i: (b * qpt + i, hd)),
                  pl.BlockSpec((1, 1, HEAD_PAD, seq), lambda b, hd, i: (b, hd, 0, 0)),
                  pl.BlockSpec((seq, V_HEAD_DIM), lambda b, hd, i: (b, hd))] + w_specs(),
        out_specs=[pl.BlockSpec((tq, V_HEAD_DIM), lambda b, hd, i: (b * qpt + i, hd))] + w_specs(),
        out_shape=[jax.ShapeDtypeStruct((t, n_heads * V_HEAD_DIM), BF16)] + w_shapes,
        scratch_shapes=[pltpu.VMEM((seq, HEAD_PAD), BF16)],
        compiler_params=_cparams("arbitrary", "arbitrary", "arbitrary"),
        name="attention",
    )(q, kt, v, *cast_weights)
    return outs[0], outs[1:]


def _branch_kernel(tiles_per_seq, cchunk, cb_ref, u_ref, up_ref, un_ref, yb_ref, cw_ref,
                   ga_ref, gb_ref, w0_ref, w1_ref, o_ref, ya_ref):
    i = pl.program_id(0)
    j = pl.program_id(1)
    tm, c = u_ref.shape

    def project():
        pa = _dot(ya_ref[...], w0_ref[0])
        pb = _dot(yb_ref[...], w1_ref[0])
        o_ref[...] = (ga_ref[...].astype(F32) * pa + gb_ref[...].astype(F32) * pb).astype(o_ref.dtype)

    @pl.when(j == 0)
    def _():
        keep_prev = (i % tiles_per_seq != 0).astype(F32)
        keep_next = (i % tiles_per_seq != tiles_per_seq - 1).astype(F32)
        row = lax.broadcasted_iota(jnp.int32, (tm, 1), 0)
        for c0 in range(0, c, cchunk):
            cs = slice(c0, c0 + cchunk)
            u = u_ref[:, cs].astype(F32)
            prev_row = up_ref[HALO_ROWS - 1:HALO_ROWS, cs].astype(F32) * keep_prev
            next_row = un_ref[0:1, cs].astype(F32) * keep_next
            u_dn = jnp.where(row == 0, prev_row, pltpu.roll(u, 1, axis=0))
            u_up = jnp.where(row == tm - 1, next_row, pltpu.roll(u, tm - 1, axis=0))
            conv = u_dn * cw_ref[0:1, cs] + u * cw_ref[1:2, cs] + u_up * cw_ref[2:3, cs]
            ya_ref[:, cs] = (cb_ref[:, cs].astype(F32) * conv).astype(ya_ref.dtype)
        project()

    @pl.when(j != 0)
    def _():
        project()


def _branch(cb, u, yb, conv_w, gates, w_br, seq, tm, tn):
    t, c = cb.shape
    d = w_br.shape[2]
    nb = d // tn
    hb = tm // HALO_ROWS
    last_hb = t // HALO_ROWS - 1
    row = lambda i, j: (i, 0)
    return pl.pallas_call(
        functools.partial(_branch_kernel, seq // tm, _pick(CONV_CHUNK, c)),
        grid=(t // tm, nb),
        in_specs=[pl.BlockSpec((tm, c), row),
                  pl.BlockSpec((tm, c), row),
                  pl.BlockSpec((HALO_ROWS, c), lambda i, j: (jnp.maximum(i * hb - 1, 0), 0)),
                  pl.BlockSpec((HALO_ROWS, c), lambda i, j: (jnp.minimum((i + 1) * hb, last_hb), 0)),
                  pl.BlockSpec((tm, c), row),
                  pl.BlockSpec(conv_w.shape, lambda i, j: (0, 0)),
                  pl.BlockSpec((tm, tn), lambda i, j: (i, j)),
                  pl.BlockSpec((tm, tn), lambda i, j: (i, j + nb)),
                  pl.BlockSpec((1, c, tn), lambda i, j: (0, 0, j)),
                  pl.BlockSpec((1, c, tn), lambda i, j: (1, 0, j))],
        out_specs=pl.BlockSpec((tm, tn), lambda i, j: (i, j)),
        out_shape=jax.ShapeDtypeStruct((t, d), BF16),
        scratch_shapes=[pltpu.VMEM((tm, c), BF16)],
        compiler_params=_cparams("parallel", "arbitrary"),
        name="branch",
    )(cb, u, u, u, yb, conv_w, gates, gates, w_br, w_br)


def _out_proj_kernel(m_ref, w_ref, x_ref, o_ref):
    o_ref[...] = x_ref[...] + _dot(m_ref[...], w_ref[...])


def _out_proj(m, w_out, x, tm, tn):
    t, d = m.shape
    n = w_out.shape[1]
    return pl.pallas_call(
        _out_proj_kernel,
        grid=(t // tm, n // tn),
        in_specs=[pl.BlockSpec((tm, d), lambda i, j: (i, 0)),
                  pl.BlockSpec((d, tn), lambda i, j: (0, j)),
                  pl.BlockSpec((tm, tn), lambda i, j: (i, j))],
        out_specs=pl.BlockSpec((tm, tn), lambda i, j: (i, j)),
        out_shape=jax.ShapeDtypeStruct((t, n), F32),
        compiler_params=_cparams("parallel", "arbitrary"),
        name="out_proj",
    )(m, w_out, x)


def _ffn_kernel(final_norm, x_ref, gf_ref, wg_ref, wu_ref, wd_ref, gl_ref, o_ref, h2_ref):
    j = pl.program_id(1)
    last = pl.num_programs(1) - 1
    tm = x_ref.shape[0]
    step = _pick(NORM_ROWS, tm)
    chunks = [slice(r0, r0 + step) for r0 in range(0, tm, step)]

    def accumulate():
        h2 = h2_ref[...]
        g = _dot(h2, wg_ref[...])
        a = (g * (1.0 / (1.0 + jnp.exp(-g)))) * _dot(h2, wu_ref[...])
        o_ref[...] += _dot(a.astype(BF16), wd_ref[...])

    @pl.when(j == 0)
    def _():
        for rs in chunks:
            x = x_ref[rs, :]
            h2_ref[rs, :] = _rms(x, gf_ref[...]).astype(h2_ref.dtype)
            o_ref[rs, :] = x
        accumulate()

    @pl.when(jnp.logical_and(j > 0, j < last))
    def _():
        accumulate()

    @pl.when(j == last)
    def _():
        accumulate()
        if final_norm:
            for rs in chunks:
                o_ref[rs, :] = _rms(o_ref[rs, :], gl_ref[...])


def _ffn(x1, g_ffn, w_g, w_u, w_d, g_final, final_norm, tm, tf):
    t, d = x1.shape
    f = w_g.shape[1]
    nf = f // tf
    assert f % tf == 0 and nf >= 2
    return pl.pallas_call(
        functools.partial(_ffn_kernel, final_norm),
        grid=(t // tm, nf),
        in_specs=[_resident((tm, d), lambda i, j: (i, 0)),
                  pl.BlockSpec((1, d), lambda i, j: (0, 0)),
                  pl.BlockSpec((d, tf), lambda i, j: (0, j)),
                  pl.BlockSpec((d, tf), lambda i, j: (0, j)),
                  pl.BlockSpec((tf, d), lambda i, j: (j, 0)),
                  pl.BlockSpec((1, d), lambda i, j: (0, 0))],
        out_specs=_resident((tm, d), lambda i, j: (i, 0)),
        out_shape=jax.ShapeDtypeStruct((t, d), F32),
        scratch_shapes=[pltpu.VMEM((tm, d), BF16)],
        compiler_params=_cparams("parallel", "arbitrary"),
        name="ffn",
    )(x1, g_ffn, w_g, w_u, w_d, g_final)


def _swap_halves(w):
    half = w.shape[-1] // 2
    return jnp.concatenate([w[..., half:], w[..., :half]], axis=-1)


def _q_weights(w_q_b, n_heads):
    r = w_q_b.shape[0]
    w = w_q_b.reshape(r, n_heads, QK_HEAD_DIM)
    nope, rope = w[..., :QK_NOPE_DIM], w[..., QK_NOPE_DIM:]
    parts = [nope.reshape(r, -1), rope.reshape(r, -1), _swap_halves(rope).reshape(r, -1)]
    return jnp.concatenate(parts, axis=1).astype(BF16)


def _pick(pref, n):
    if n <= pref:
        return n
    t = pref
    while n % t:
        t //= 2
    return t


class _Tiles(NamedTuple):
    kv_rows: int
    q_rows: int
    proj_rows: int
    conv_cols: int
    gate_cols: int
    attn_q: int
    attn_k: int
    branch_rows: int
    branch_cols: int
    ffn_rows: int
    ffn_cols: int


def _tiles(t, seq, d, conv_dim, d_ff):
    return _Tiles(kv_rows=_pick(256, seq), q_rows=_pick(512, seq), proj_rows=_pick(1024, t),
                  conv_cols=_pick(512, conv_dim), gate_cols=_pick(1024, d),
                  attn_q=_pick(1024, seq), attn_k=_pick(256, seq),
                  branch_rows=_pick(1024, seq), branch_cols=_pick(512, d),
                  ffn_rows=_pick(1024, t), ffn_cols=_pick(256, d_ff))


def kernel(x, positions, g_mix, w_in, b_gate, conv_w, g_q_a, w_q_b, g_kv_a, w_kv_b, w_branch,
           w_out, g_ffn, w_ffn_gate, w_ffn_up, w_ffn_down, g_final):
    batch, seq, d = x.shape
    depth = w_in.shape[0]
    t = batch * seq
    conv_dim = conv_w.shape[-1]
    q_rank = g_q_a.shape[-1]
    kv_rank = g_kv_a.shape[-1]
    n_heads = w_q_b.shape[-1] // QK_HEAD_DIM
    qa_col = 3 * conv_dim
    kva_col = qa_col + q_rank
    kr_col = kva_col + kv_rank
    gate_col = kr_col + QK_ROPE_DIM

    tl = _tiles(t, seq, d, conv_dim, w_ffn_gate.shape[-1])

    xf = x.reshape(t, d)
    pos = positions.reshape(t, 1)
    inv_freq = ROPE_THETA ** (-jnp.arange(0, QK_ROPE_DIM, 2, dtype=F32) / QK_ROPE_DIM)
    reps = 2 * LANE // QK_ROPE_DIM
    half = QK_ROPE_DIM // 2
    invf = jnp.tile(inv_freq, reps)[None, :]
    sgn = jnp.tile(jnp.concatenate([-jnp.ones((half,), F32), jnp.ones((half,), F32)]), reps // 2)[None, :]

    for l in range(depth):
        w_in_t = jnp.swapaxes(w_in[l], 0, 1).astype(BF16)
        w_kr_t = jnp.swapaxes(w_in[l][:, kr_col:gate_col], 0, 1)
        w_krs_t = jnp.concatenate([w_kr_t[half:], w_kr_t[:half]], axis=0)
        pad_t = jnp.zeros((LANE - QK_ROPE_DIM, d), F32)
        wkvx_t = jnp.concatenate([w_in_t[kva_col:kr_col],
                                  jnp.concatenate([w_kr_t, pad_t, w_krs_t, pad_t], axis=0).astype(BF16)],
                                 axis=0)
        wq = _q_weights(w_q_b[l], n_heads)
        w_kv = w_kv_b[l].reshape(kv_rank, n_heads, QK_NOPE_DIM + V_HEAD_DIM)
        wkt = w_kv[..., :QK_NOPE_DIM].reshape(kv_rank, n_heads * QK_NOPE_DIM).T.astype(BF16)
        wv = w_kv[..., QK_NOPE_DIM:].reshape(kv_rank, n_heads * V_HEAD_DIM).astype(BF16)

        h, kt, v = _kv_proj(xf, g_mix[l][None, :], pos, invf, sgn, wkvx_t, g_kv_a[l][None, :], wkt, wv,
                            n_heads, batch, seq, tl.kv_rows)
        cb, u = _conv_proj(h, w_in_t, conv_dim, tl.proj_rows, tl.conv_cols)
        gates, (w_br, w_o) = _gate_proj(h, w_in_t, gate_col, b_gate[l][None, :],
                                        (w_branch[l].reshape(-1, d), w_out[l]), tl.proj_rows, tl.gate_cols)
        w_br = w_br.reshape(w_branch.shape[1:])
        q = _q_proj(h, pos, invf, sgn, w_in_t, qa_col, g_q_a[l][None, :], wq, n_heads, tl.q_rows)
        yb, (w_fg, w_fu, w_fd) = _attention(q, kt, v, (w_ffn_gate[l], w_ffn_up[l], w_ffn_down[l]),
                                            n_heads, batch, seq, tl.attn_q, tl.attn_k)
        m = _branch(cb, u, yb, conv_w[l], gates, w_br, seq, tl.branch_rows, tl.branch_cols)
        xf = _out_proj(m, w_o, xf, tl.proj_rows, tl.gate_cols)
        xf = _ffn(xf, g_ffn[l][None, :], w_fg, w_fu, w_fd, g_final[None, :], l == depth - 1,
                  tl.ffn_rows, tl.ffn_cols)
    return xf.reshape(batch, seq, d)
```

```python
import functools
import math
from typing import NamedTuple

import jax
import jax.numpy as jnp
from jax import lax
from jax.experimental import pallas as pl
from jax.experimental.pallas import tpu as pltpu

F32 = jnp.float32
BF16 = jnp.bfloat16

RMS_EPS = 1e-6
ROPE_THETA = 10000.0
QK_NOPE_DIM = 128
QK_ROPE_DIM = 64
V_HEAD_DIM = 128
QK_HEAD_DIM = QK_NOPE_DIM + QK_ROPE_DIM
Q_SCALE = math.log2(math.e) / math.sqrt(QK_HEAD_DIM)
HEAD_PAD = 256
LANE = 128
BF16_ROWS = 16
HALO_ROWS = BF16_ROWS
CONV_CHUNK = 512
NORM_ROWS = 256
CAST_ROWS = 512
VMEM_LIMIT = 63 * 1024 * 1024


def _cparams(*sem):
    return pltpu.CompilerParams(dimension_semantics=sem, vmem_limit_bytes=VMEM_LIMIT)


def _resident(block_shape, index_map):
    return pl.BlockSpec(block_shape, index_map, pipeline_mode=pl.Buffered(1))


def _dot(a, b):
    return jnp.dot(a, b, preferred_element_type=F32)


def _dot_nt(a, b):
    return lax.dot_general(a, b, (((1,), (1,)), ((), ())), preferred_element_type=F32)


def _rms(x, g):
    inv = lax.rsqrt(jnp.mean(x * x, axis=-1, keepdims=True) + RMS_EPS)
    return x * inv * g


def _cast_rows(rows, steps, first_row=0):
    for rb in range(BF16_ROWS, rows + 1, BF16_ROWS):
        if rows % rb == 0 and first_row % rb == 0 and rows // rb <= steps:
            return rb
    return None


def _cast_plan(weights, steps, step_of, row_ranges=None):
    ranges = row_ranges or [(0, w.shape[0]) for w in weights]
    rows = [_cast_rows(n, steps, r0) for r0, n in ranges]
    assert all(rows), "no block size fits; cast that weight outside"
    blocks = tuple(n // rb for (_, n), rb in zip(ranges, rows))

    def specs(with_offset):
        return [pl.BlockSpec((rb, w.shape[1]),
                             lambda *idx, nblk=nblk, off=(r0 // rb if with_offset else 0):
                                 (off + jnp.minimum(step_of(*idx), nblk - 1), 0))
                for w, rb, nblk, (r0, _) in zip(weights, rows, blocks, ranges)]
    shapes = [jax.ShapeDtypeStruct((n, w.shape[1]), BF16) for w, (_, n) in zip(weights, ranges)]
    return blocks, functools.partial(specs, True), functools.partial(specs, False), shapes


def _cast_step(step, blocks, w_refs, wo_refs):
    for w_ref, wo_ref, nblk in zip(w_refs, wo_refs, blocks):
        @pl.when(step < nblk)
        def _(w_ref=w_ref, wo_ref=wo_ref):
            wo_ref[...] = w_ref[...].astype(wo_ref.dtype)


def _cast_kernel(w_ref, o_ref):
    o_ref[...] = w_ref[...].astype(o_ref.dtype)


def _cast_head_rows(w, n_rows, rb):
    assert n_rows % rb == 0
    return pl.pallas_call(
        _cast_kernel,
        grid=(n_rows // rb,),
        in_specs=[pl.BlockSpec((rb, w.shape[1]), lambda i: (i, 0))],
        out_specs=pl.BlockSpec((rb, w.shape[1]), lambda i: (i, 0)),
        out_shape=jax.ShapeDtypeStruct((n_rows, w.shape[1]), BF16),
        compiler_params=_cparams("parallel"),
        name="cast_head",
    )(w)


def _conv_proj_kernel(cast_blocks, h_ref, wb_ref, wc_ref, wh_ref, *refs):
    n_cast = len(cast_blocks)
    w_refs, (cb_ref, u_ref), wo_refs = refs[:n_cast], refs[n_cast:n_cast + 2], refs[n_cast + 2:]
    _cast_step(pl.program_id(0) * pl.num_programs(1) + pl.program_id(1), cast_blocks, w_refs, wo_refs)
    h = h_ref[...]
    cb_ref[...] = _dot_nt(h, wb_ref[...]).astype(cb_ref.dtype)
    u_ref[...] = (_dot_nt(h, wc_ref[...]) * _dot_nt(h, wh_ref[...])).astype(u_ref.dtype)


def _conv_proj(h, w_in_t, conv_dim, cast_weights, cast_ranges, tm, tn):
    t, d = h.shape
    nb = conv_dim // tn
    w_spec = lambda off: pl.BlockSpec((tn, d), lambda i, j: (j + off, 0))
    out = jax.ShapeDtypeStruct((t, conv_dim), BF16)
    cast_blocks, w_in_specs, w_out_specs, w_shapes = _cast_plan(
        cast_weights, (t // tm) * nb, lambda i, j: i * nb + j, cast_ranges)
    outs = pl.pallas_call(
        functools.partial(_conv_proj_kernel, cast_blocks),
        grid=(t // tm, nb),
        in_specs=[pl.BlockSpec((tm, d), lambda i, j: (i, 0)),
                  w_spec(0), w_spec(nb), w_spec(2 * nb)] + w_in_specs(),
        out_specs=[pl.BlockSpec((tm, tn), lambda i, j: (i, j))] * 2 + w_out_specs(),
        out_shape=[out, out] + w_shapes,
        compiler_params=_cparams("arbitrary", "arbitrary"),
        name="conv_proj",
    )(h, w_in_t, w_in_t, w_in_t, *cast_weights)
    return outs[0], outs[1], outs[2:]


def _gate_proj_kernel(cast_blocks, h_ref, w_ref, b_ref, *refs):
    n_cast = len(cast_blocks)
    w_refs, o_ref, wo_refs = refs[:n_cast], refs[n_cast], refs[n_cast + 1:]
    _cast_step(pl.program_id(0) * pl.num_programs(1) + pl.program_id(1), cast_blocks, w_refs, wo_refs)
    z = _dot_nt(h_ref[...], w_ref[...]) + b_ref[...]
    o_ref[...] = (1.0 / (1.0 + jnp.exp(-z))).astype(o_ref.dtype)


def _gate_proj(h, w_in_t, gate_row, b_gate, cast_weights, tm, tn):
    t, d = h.shape
    n = b_gate.shape[1]
    nb = n // tn
    cast_blocks, w_in_specs, w_out_specs, w_shapes = _cast_plan(
        cast_weights, (t // tm) * nb, lambda i, j: i * nb + j)
    outs = pl.pallas_call(
        functools.partial(_gate_proj_kernel, cast_blocks),
        grid=(t // tm, nb),
        in_specs=[pl.BlockSpec((tm, d), lambda i, j: (i, 0)),
                  pl.BlockSpec((pl.Element(tn), pl.Element(d)),
                               lambda i, j: (pl.multiple_of(gate_row + j * tn, math.gcd(gate_row, tn)), 0)),
                  pl.BlockSpec((1, tn), lambda i, j: (0, j))] + w_in_specs(),
        out_specs=[pl.BlockSpec((tm, tn), lambda i, j: (i, j))] + w_out_specs(),
        out_shape=[jax.ShapeDtypeStruct((t, n), BF16)] + w_shapes,
        compiler_params=_cparams("arbitrary", "arbitrary"),
        name="gate_proj",
    )(h, w_in_t, b_gate, *cast_weights)
    return outs[0], outs[1:]


def _rope_tables(pos_ref, invf_ref, sgn_ref):
    ang = pos_ref[...].astype(F32) * invf_ref[...]
    return jnp.cos(ang), jnp.sin(ang) * sgn_ref[...]


def _q_proj_kernel(n_heads, h_ref, pos_ref, invf_ref, sgn_ref, wqa_ref, gq_ref, wq_ref, q_ref):
    qn = _rms(_dot_nt(h_ref[...], wqa_ref[...]), gq_ref[...]).astype(BF16)
    z = _dot(qn, wq_ref[...])
    rope0 = n_heads * QK_NOPE_DIM
    swap0 = rope0 + n_heads * QK_ROPE_DIM
    cos, sin = _rope_tables(pos_ref, invf_ref, sgn_ref)
    cos, sin = cos * Q_SCALE, sin * Q_SCALE
    first = lax.broadcasted_iota(jnp.int32, (1, LANE), 1) < QK_ROPE_DIM
    for pair in range(n_heads // 2):
        g = pair * LANE
        rot = z[:, rope0 + g:rope0 + g + LANE] * cos + z[:, swap0 + g:swap0 + g + LANE] * sin
        for k, r in enumerate((rot, pltpu.roll(rot, QK_ROPE_DIM, axis=1))):
            hd = 2 * pair + k
            a = hd * HEAD_PAD
            nope = z[:, hd * QK_NOPE_DIM:(hd + 1) * QK_NOPE_DIM] * Q_SCALE
            q_ref[:, a:a + LANE] = nope.astype(q_ref.dtype)
            q_ref[:, a + LANE:a + HEAD_PAD] = jnp.where(first, r, 0.0).astype(q_ref.dtype)


def _q_proj(h, pos, invf, sgn, w_in_t, qa_col, g_q, wq, n_heads, tm):
    t, d = h.shape
    q_rank = g_q.shape[1]
    assert qa_col % q_rank == 0 and n_heads % 2 == 0
    const = lambda i: (0, 0)
    return pl.pallas_call(
        functools.partial(_q_proj_kernel, n_heads),
        grid=(t // tm,),
        in_specs=[pl.BlockSpec((tm, d), lambda i: (i, 0)),
                  pl.BlockSpec((tm, 1), lambda i: (i, 0)),
                  _resident((1, LANE), const),
                  _resident((1, LANE), const),
                  _resident((q_rank, d), lambda i: (qa_col // q_rank, 0)),
                  _resident(g_q.shape, const),
                  _resident(wq.shape, const)],
        out_specs=pl.BlockSpec((tm, n_heads * HEAD_PAD), lambda i: (i, 0)),
        out_shape=jax.ShapeDtypeStruct((t, n_heads * HEAD_PAD), BF16),
        compiler_params=_cparams("parallel"),
        name="q_proj",
    )(h, pos, invf, sgn, w_in_t, g_q, wq)


def _kv_proj_kernel(n_heads, x_ref, gm_ref, pos_ref, invf_ref, sgn_ref, wkvx_ref, gkv_ref,
                    wkt_ref, wv_ref, h_ref, kt_ref, v_ref):
    h = _rms(x_ref[...], gm_ref[...]).astype(h_ref.dtype)
    h_ref[...] = h
    kv_rank = gkv_ref.shape[1]
    z = _dot_nt(h, wkvx_ref[...])
    kvn = _rms(z[:, :kv_rank], gkv_ref[...]).astype(BF16)
    v_ref[...] = _dot(kvn, wv_ref[...]).astype(v_ref.dtype)
    knt = lax.dot_general(wkt_ref[...], kvn, (((1,), (1,)), ((), ())),
                          preferred_element_type=F32)
    cos, sin = _rope_tables(pos_ref, invf_ref, sgn_ref)
    krot = z[:, kv_rank:kv_rank + LANE] * cos + z[:, kv_rank + LANE:] * sin
    krt = krot.T.astype(kt_ref.dtype)
    for hd in range(n_heads):
        kt_ref[0, hd, 0:LANE, :] = knt[hd * LANE:(hd + 1) * LANE, :].astype(kt_ref.dtype)
        kt_ref[0, hd, LANE:HEAD_PAD, :] = krt


def _kv_proj(x, g_mix, pos, invf, sgn, wkvx, g_kv, wkt, wv, n_heads, batch, seq, tm):
    t, d = x.shape
    spt = seq // tm
    const = lambda i: (0, 0)
    return pl.pallas_call(
        functools.partial(_kv_proj_kernel, n_heads),
        grid=(t // tm,),
        in_specs=[pl.BlockSpec((tm, d), lambda i: (i, 0)),
                  _resident(g_mix.shape, const),
                  pl.BlockSpec((tm, 1), lambda i: (i, 0)),
                  _resident((1, LANE), const),
                  _resident((1, LANE), const),
                  _resident(wkvx.shape, const),
                  _resident(g_kv.shape, const),
                  _resident(wkt.shape, const),
                  _resident(wv.shape, const)],
        out_specs=[pl.BlockSpec((tm, d), lambda i: (i, 0)),
                   pl.BlockSpec((1, n_heads, HEAD_PAD, tm), lambda i: (i // spt, 0, 0, i % spt)),
                   pl.BlockSpec((tm, n_heads * V_HEAD_DIM), lambda i: (i, 0))],
        out_shape=[jax.ShapeDtypeStruct((t, d), BF16),
                   jax.ShapeDtypeStruct((batch, n_heads, HEAD_PAD, seq), BF16),
                   jax.ShapeDtypeStruct((t, n_heads * V_HEAD_DIM), BF16)],
        compiler_params=_cparams("parallel"),
        name="kv_proj",
    )(x, g_mix, pos, invf, sgn, wkvx, g_kv, wkt, wv)


def _attn_kernel(kc, cast_blocks, q_ref, kt_ref, v_ref, *refs):
    n_cast = len(cast_blocks)
    w_refs, o_ref = refs[:n_cast], refs[n_cast]
    wo_refs, vx_ref = refs[n_cast + 1:2 * n_cast + 1], refs[2 * n_cast + 1]
    step = (pl.program_id(0) * pl.num_programs(1) + pl.program_id(1)) * pl.num_programs(2) + pl.program_id(2)
    _cast_step(step, cast_blocks, w_refs, wo_refs)

    dv = o_ref.shape[-1]

    @pl.when(step == 0)
    def _():
        lane = lax.broadcasted_iota(jnp.int32, (vx_ref.shape[0], vx_ref.shape[1] - dv), 1)
        vx_ref[:, dv:] = jnp.where(lane == 0, 1.0, 0.0).astype(vx_ref.dtype)

    @pl.when(pl.program_id(2) == 0)
    def _():
        vx_ref[:, :dv] = v_ref[...]

    q = q_ref[...]
    tq = q.shape[0]
    seq = kt_ref.shape[-1]
    m = jnp.full((tq, 1), -jnp.inf, F32)
    acc = jnp.zeros((tq, vx_ref.shape[-1]), F32)
    for c0 in range(0, seq, kc):
        s = _dot(q, kt_ref[0, 0, :, c0:c0 + kc])
        m_new = jnp.maximum(m, jnp.max(s, axis=-1, keepdims=True))
        alpha = jnp.exp2(m - m_new)
        p = jnp.exp2(s - m_new).astype(BF16)
        acc = alpha * acc + _dot(p, vx_ref[c0:c0 + kc, :])
        m = m_new
    o_ref[...] = (acc[:, :dv] / acc[:, dv:dv + 1]).astype(o_ref.dtype)


def _attention(q, kt, v, cast_weights, n_heads, batch, seq, tq, kc):
    t = q.shape[0]
    qpt = seq // tq
    cast_blocks, w_in_specs, w_out_specs, w_shapes = _cast_plan(
        cast_weights, batch * n_heads * qpt, lambda b, hd, i: (b * n_heads + hd) * qpt + i)
    outs = pl.pallas_call(
        functools.partial(_attn_kernel, kc, cast_blocks),
        grid=(batch, n_heads, qpt),
        in_specs=[pl.BlockSpec((tq, HEAD_PAD), lambda b, hd, i: (b * qpt + i, hd)),
                  pl.BlockSpec((1, 1, HEAD_PAD, seq), lambda b, hd, i: (b, hd, 0, 0)),
                  pl.BlockSpec((seq, V_HEAD_DIM), lambda b, hd, i: (b, hd))] + w_in_specs(),
        out_specs=[pl.BlockSpec((tq, V_HEAD_DIM), lambda b, hd, i: (b * qpt + i, hd))] + w_out_specs(),
        out_shape=[jax.ShapeDtypeStruct((t, n_heads * V_HEAD_DIM), BF16)] + w_shapes,
        scratch_shapes=[pltpu.VMEM((seq, HEAD_PAD), BF16)],
        compiler_params=_cparams("arbitrary", "arbitrary", "arbitrary"),
        name="attention",
    )(q, kt, v, *cast_weights)
    return outs[0], outs[1:]


def _branch_kernel(tiles_per_seq, cchunk, cb_ref, u_ref, up_ref, un_ref, yb_ref, cw_ref,
                   ga_ref, gb_ref, w0_ref, w1_ref, o_ref, ya_ref):
    i = pl.program_id(0)
    j = pl.program_id(1)
    tm, c = u_ref.shape

    def project():
        pa = _dot(ya_ref[...], w0_ref[0])
        pb = _dot(yb_ref[...], w1_ref[0])
        o_ref[...] = (ga_ref[...].astype(F32) * pa + gb_ref[...].astype(F32) * pb).astype(o_ref.dtype)

    @pl.when(j == 0)
    def _():
        keep_prev = (i % tiles_per_seq != 0).astype(F32)
        keep_next = (i % tiles_per_seq != tiles_per_seq - 1).astype(F32)
        row = lax.broadcasted_iota(jnp.int32, (tm, 1), 0)
        for c0 in range(0, c, cchunk):
            cs = slice(c0, c0 + cchunk)
            u = u_ref[:, cs].astype(F32)
            prev_row = up_ref[HALO_ROWS - 1:HALO_ROWS, cs].astype(F32) * keep_prev
            next_row = un_ref[0:1, cs].astype(F32) * keep_next
            u_dn = jnp.where(row == 0, prev_row, pltpu.roll(u, 1, axis=0))
            u_up = jnp.where(row == tm - 1, next_row, pltpu.roll(u, tm - 1, axis=0))
            conv = u_dn * cw_ref[0:1, cs] + u * cw_ref[1:2, cs] + u_up * cw_ref[2:3, cs]
            ya_ref[:, cs] = (cb_ref[:, cs].astype(F32) * conv).astype(ya_ref.dtype)
        project()

    @pl.when(j != 0)
    def _():
        project()


def _branch(cb, u, yb, conv_w, gates, w_br, seq, tm, tn):
    t, c = cb.shape
    d = w_br.shape[2]
    nb = d // tn
    hb = tm // HALO_ROWS
    last_hb = t // HALO_ROWS - 1
    row = lambda i, j: (i, 0)
    return pl.pallas_call(
        functools.partial(_branch_kernel, seq // tm, _pick(CONV_CHUNK, c)),
        grid=(t // tm, nb),
        in_specs=[pl.BlockSpec((tm, c), row),
                  pl.BlockSpec((tm, c), row),
                  pl.BlockSpec((HALO_ROWS, c), lambda i, j: (jnp.maximum(i * hb - 1, 0), 0)),
                  pl.BlockSpec((HALO_ROWS, c), lambda i, j: (jnp.minimum((i + 1) * hb, last_hb), 0)),
                  pl.BlockSpec((tm, c), row),
                  pl.BlockSpec(conv_w.shape, lambda i, j: (0, 0)),
                  pl.BlockSpec((tm, tn), lambda i, j: (i, j)),
                  pl.BlockSpec((tm, tn), lambda i, j: (i, j + nb)),
                  pl.BlockSpec((1, c, tn), lambda i, j: (0, 0, j)),
                  pl.BlockSpec((1, c, tn), lambda i, j: (1, 0, j))],
        out_specs=pl.BlockSpec((tm, tn), lambda i, j: (i, j)),
        out_shape=jax.ShapeDtypeStruct((t, d), BF16),
        scratch_shapes=[pltpu.VMEM((tm, c), BF16)],
        compiler_params=_cparams("parallel", "arbitrary"),
        name="branch",
    )(cb, u, u, u, yb, conv_w, gates, gates, w_br, w_br)


def _out_proj_kernel(m_ref, w_ref, x_ref, o_ref):
    o_ref[...] = x_ref[...] + _dot(m_ref[...], w_ref[...])


def _out_proj(m, w_out, x, tm, tn):
    t, d = m.shape
    n = w_out.shape[1]
    return pl.pallas_call(
        _out_proj_kernel,
        grid=(t // tm, n // tn),
        in_specs=[pl.BlockSpec((tm, d), lambda i, j: (i, 0)),
                  pl.BlockSpec((d, tn), lambda i, j: (0, j)),
                  pl.BlockSpec((tm, tn), lambda i, j: (i, j))],
        out_specs=pl.BlockSpec((tm, tn), lambda i, j: (i, j)),
        out_shape=jax.ShapeDtypeStruct((t, n), F32),
        compiler_params=_cparams("parallel", "arbitrary"),
        name="out_proj",
    )(m, w_out, x)


def _ffn_kernel(final_norm, x_ref, gf_ref, wg_ref, wu_ref, wd_ref, gl_ref, o_ref, h2_ref):
    j = pl.program_id(1)
    last = pl.num_programs(1) - 1
    tm = x_ref.shape[0]
    step = _pick(NORM_ROWS, tm)
    chunks = [slice(r0, r0 + step) for r0 in range(0, tm, step)]

    def accumulate():
        h2 = h2_ref[...]
        g = _dot(h2, wg_ref[...])
        a = (g * (1.0 / (1.0 + jnp.exp(-g)))) * _dot(h2, wu_ref[...])
        o_ref[...] += _dot(a.astype(BF16), wd_ref[...])

    @pl.when(j == 0)
    def _():
        for rs in chunks:
            x = x_ref[rs, :]
            h2_ref[rs, :] = _rms(x, gf_ref[...]).astype(h2_ref.dtype)
            o_ref[rs, :] = x
        accumulate()

    @pl.when(jnp.logical_and(j > 0, j < last))
    def _():
        accumulate()

    @pl.when(j == last)
    def _():
        accumulate()
        if final_norm:
            for rs in chunks:
                o_ref[rs, :] = _rms(o_ref[rs, :], gl_ref[...])


def _ffn(x1, g_ffn, w_g, w_u, w_d, g_final, final_norm, tm, tf):
    t, d = x1.shape
    f = w_g.shape[1]
    nf = f // tf
    assert f % tf == 0 and nf >= 2
    return pl.pallas_call(
        functools.partial(_ffn_kernel, final_norm),
        grid=(t // tm, nf),
        in_specs=[_resident((tm, d), lambda i, j: (i, 0)),
                  pl.BlockSpec((1, d), lambda i, j: (0, 0)),
                  pl.BlockSpec((d, tf), lambda i, j: (0, j)),
                  pl.BlockSpec((d, tf), lambda i, j: (0, j)),
                  pl.BlockSpec((tf, d), lambda i, j: (j, 0)),
                  pl.BlockSpec((1, d), lambda i, j: (0, 0))],
        out_specs=_resident((tm, d), lambda i, j: (i, 0)),
        out_shape=jax.ShapeDtypeStruct((t, d), F32),
        scratch_shapes=[pltpu.VMEM((tm, d), BF16)],
        compiler_params=_cparams("parallel", "arbitrary"),
        name="ffn",
    )(x1, g_ffn, w_g, w_u, w_d, g_final)


def _swap_halves(w):
    half = w.shape[-1] // 2
    return jnp.concatenate([w[..., half:], w[..., :half]], axis=-1)


def _q_weights(w_q_b, n_heads):
    r = w_q_b.shape[0]
    w = w_q_b.reshape(r, n_heads, QK_HEAD_DIM)
    nope, rope = w[..., :QK_NOPE_DIM], w[..., QK_NOPE_DIM:]
    parts = [nope.reshape(r, -1), rope.reshape(r, -1), _swap_halves(rope).reshape(r, -1)]
    return jnp.concatenate(parts, axis=1).astype(BF16)


def _pick(pref, n):
    if n <= pref:
        return n
    t = pref
    while n % t:
        t //= 2
    return t


class _Tiles(NamedTuple):
    kv_rows: int
    q_rows: int
    proj_rows: int
    conv_cols: int
    gate_cols: int
    attn_q: int
    attn_k: int
    branch_rows: int
    branch_cols: int
    ffn_rows: int
    ffn_cols: int


def _tiles(t, seq, d, conv_dim, d_ff):
    return _Tiles(kv_rows=_pick(256, seq), q_rows=_pick(512, seq), proj_rows=_pick(1024, t),
                  conv_cols=_pick(512, conv_dim), gate_cols=_pick(1024, d),
                  attn_q=_pick(1024, seq), attn_k=_pick(256, seq),
                  branch_rows=_pick(1024, seq), branch_cols=_pick(512, d),
                  ffn_rows=_pick(1024, t), ffn_cols=_pick(256, d_ff))


def kernel(x, positions, g_mix, w_in, b_gate, conv_w, g_q_a, w_q_b, g_kv_a, w_kv_b, w_branch,
           w_out, g_ffn, w_ffn_gate, w_ffn_up, w_ffn_down, g_final):
    batch, seq, d = x.shape
    depth = w_in.shape[0]
    t = batch * seq
    conv_dim = conv_w.shape[-1]
    q_rank = g_q_a.shape[-1]
    kv_rank = g_kv_a.shape[-1]
    n_heads = w_q_b.shape[-1] // QK_HEAD_DIM
    qa_col = 3 * conv_dim
    kva_col = qa_col + q_rank
    kr_col = kva_col + kv_rank
    gate_col = kr_col + QK_ROPE_DIM

    tl = _tiles(t, seq, d, conv_dim, w_ffn_gate.shape[-1])

    xf = x.reshape(t, d)
    pos = positions.reshape(t, 1)
    inv_freq = ROPE_THETA ** (-jnp.arange(0, QK_ROPE_DIM, 2, dtype=F32) / QK_ROPE_DIM)
    reps = 2 * LANE // QK_ROPE_DIM
    half = QK_ROPE_DIM // 2
    invf = jnp.tile(inv_freq, reps)[None, :]
    sgn = jnp.tile(jnp.concatenate([-jnp.ones((half,), F32), jnp.ones((half,), F32)]), reps // 2)[None, :]

    for l in range(depth):
        w_in_f32_t = jnp.swapaxes(w_in[l], 0, 1)
        w_in_t = _cast_head_rows(w_in_f32_t, kr_col, _pick(CAST_ROWS, kr_col))
        n_tail = w_in_f32_t.shape[0] - kr_col
        conv_steps = (t // tl.proj_rows) * (conv_dim // tl.conv_cols)
        tail_in_conv = _cast_rows(n_tail, conv_steps, kr_col) is not None
        w_kr_t = jnp.swapaxes(w_in[l][:, kr_col:gate_col], 0, 1)
        w_krs_t = jnp.concatenate([w_kr_t[half:], w_kr_t[:half]], axis=0)
        pad_t = jnp.zeros((LANE - QK_ROPE_DIM, d), F32)
        wkvx_t = jnp.concatenate([w_in_t[kva_col:kr_col],
                                  jnp.concatenate([w_kr_t, pad_t, w_krs_t, pad_t], axis=0).astype(BF16)],
                                 axis=0)
        wq = _q_weights(w_q_b[l], n_heads)
        w_kv = w_kv_b[l].reshape(kv_rank, n_heads, QK_NOPE_DIM + V_HEAD_DIM)
        wkt = w_kv[..., :QK_NOPE_DIM].reshape(kv_rank, n_heads * QK_NOPE_DIM).T.astype(BF16)
        wv = w_kv[..., QK_NOPE_DIM:].reshape(kv_rank, n_heads * V_HEAD_DIM).astype(BF16)

        h, kt, v = _kv_proj(xf, g_mix[l][None, :], pos, invf, sgn, wkvx_t, g_kv_a[l][None, :], wkt, wv,
                            n_heads, batch, seq, tl.kv_rows)
        if tail_in_conv:
            cb, u, (w_tail_t,) = _conv_proj(h, w_in_t, conv_dim, (w_in_f32_t,), [(kr_col, n_tail)],
                                            tl.proj_rows, tl.conv_cols)
        else:
            cb, u, _ = _conv_proj(h, w_in_t, conv_dim, (), [], tl.proj_rows, tl.conv_cols)
            w_tail_t = w_in_f32_t[kr_col:].astype(BF16)
        gates, (w_br, w_o) = _gate_proj(h, w_tail_t, gate_col - kr_col, b_gate[l][None, :],
                                        (w_branch[l].reshape(-1, d), w_out[l]), tl.proj_rows, tl.gate_cols)
        w_br = w_br.reshape(w_branch.shape[1:])
        q = _q_proj(h, pos, invf, sgn, w_in_t, qa_col, g_q_a[l][None, :], wq, n_heads, tl.q_rows)
        yb, (w_fg, w_fu, w_fd) = _attention(q, kt, v, (w_ffn_gate[l], w_ffn_up[l], w_ffn_down[l]),
                                            n_heads, batch, seq, tl.attn_q, tl.attn_k)
        m = _branch(cb, u, yb, conv_w[l], gates, w_br, seq, tl.branch_rows, tl.branch_cols)
        xf = _out_proj(m, w_o, xf, tl.proj_rows, tl.gate_cols)
        xf = _ffn(xf, g_ffn[l][None, :], w_fg, w_fu, w_fd, g_final[None, :], l == depth - 1,
                  tl.ffn_rows, tl.ffn_cols)
    return xf.reshape(batch, seq, d)
```

```python
import functools
import math
from typing import NamedTuple

import jax
import jax.numpy as jnp
from jax import lax
from jax.experimental import pallas as pl
from jax.experimental.pallas import tpu as pltpu

F32 = jnp.float32
BF16 = jnp.bfloat16

RMS_EPS = 1e-6
ROPE_THETA = 10000.0
QK_NOPE_DIM = 128
QK_ROPE_DIM = 64
V_HEAD_DIM = 128
QK_HEAD_DIM = QK_NOPE_DIM + QK_ROPE_DIM
Q_SCALE = math.log2(math.e) / math.sqrt(QK_HEAD_DIM)
HEAD_PAD = 256
LANE = 128
BF16_ROWS = 16
HALO_ROWS = BF16_ROWS
CONV_CHUNK = 512
NORM_ROWS = 256
CAST_ROWS = 512
VMEM_LIMIT = 63 * 1024 * 1024


def _cparams(*sem):
    return pltpu.CompilerParams(dimension_semantics=sem, vmem_limit_bytes=VMEM_LIMIT)


def _resident(block_shape, index_map):
    return pl.BlockSpec(block_shape, index_map, pipeline_mode=pl.Buffered(1))


def _dot(a, b):
    return jnp.dot(a, b, preferred_element_type=F32)


def _dot_nt(a, b):
    return lax.dot_general(a, b, (((1,), (1,)), ((), ())), preferred_element_type=F32)


def _rms(x, g):
    inv = lax.rsqrt(jnp.mean(x * x, axis=-1, keepdims=True) + RMS_EPS)
    return x * inv * g


def _cast_rows(rows, steps, first_row=0):
    for rb in range(BF16_ROWS, rows + 1, BF16_ROWS):
        if rows % rb == 0 and first_row % rb == 0 and rows // rb <= steps:
            return rb
    return None


def _cast_plan(weights, steps, step_of, row_ranges=None):
    ranges = row_ranges or [(0, w.shape[0]) for w in weights]
    rows = [_cast_rows(n, steps, r0) for r0, n in ranges]
    assert all(rows), "no block size fits; cast that weight outside"
    blocks = tuple(n // rb for (_, n), rb in zip(ranges, rows))

    def specs(with_offset):
        return [pl.BlockSpec((rb, w.shape[1]),
                             lambda *idx, nblk=nblk, off=(r0 // rb if with_offset else 0):
                                 (off + jnp.minimum(step_of(*idx), nblk - 1), 0))
                for w, rb, nblk, (r0, _) in zip(weights, rows, blocks, ranges)]
    shapes = [jax.ShapeDtypeStruct((n, w.shape[1]), BF16) for w, (_, n) in zip(weights, ranges)]
    return blocks, functools.partial(specs, True), functools.partial(specs, False), shapes


def _cast_step(step, blocks, w_refs, wo_refs):
    for w_ref, wo_ref, nblk in zip(w_refs, wo_refs, blocks):
        @pl.when(step < nblk)
        def _(w_ref=w_ref, wo_ref=wo_ref):
            wo_ref[...] = w_ref[...].astype(wo_ref.dtype)


def _cast_kernel(w_ref, o_ref):
    o_ref[...] = w_ref[...].astype(o_ref.dtype)


def _cast_head_rows(w, n_rows, rb):
    assert n_rows % rb == 0
    return pl.pallas_call(
        _cast_kernel,
        grid=(n_rows // rb,),
        in_specs=[pl.BlockSpec((rb, w.shape[1]), lambda i: (i, 0))],
        out_specs=pl.BlockSpec((rb, w.shape[1]), lambda i: (i, 0)),
        out_shape=jax.ShapeDtypeStruct((n_rows, w.shape[1]), BF16),
        compiler_params=_cparams("parallel"),
        name="cast_head",
    )(w)


def _conv_proj_kernel(cast_blocks, h_ref, wb_ref, wc_ref, wh_ref, *refs):
    n_cast = len(cast_blocks)
    w_refs, (cb_ref, u_ref), wo_refs = refs[:n_cast], refs[n_cast:n_cast + 2], refs[n_cast + 2:]
    _cast_step(pl.program_id(0) * pl.num_programs(1) + pl.program_id(1), cast_blocks, w_refs, wo_refs)
    h = h_ref[...]
    cb_ref[...] = _dot_nt(h, wb_ref[...]).astype(cb_ref.dtype)
    u_ref[...] = (_dot_nt(h, wc_ref[...]) * _dot_nt(h, wh_ref[...])).astype(u_ref.dtype)


def _conv_proj(h, w_in_t, conv_dim, cast_weights, cast_ranges, tm, tn):
    t, d = h.shape
    nb = conv_dim // tn
    w_spec = lambda off: pl.BlockSpec((tn, d), lambda i, j: (j + off, 0))
    out = jax.ShapeDtypeStruct((t, conv_dim), BF16)
    cast_blocks, w_in_specs, w_out_specs, w_shapes = _cast_plan(
        cast_weights, (t // tm) * nb, lambda i, j: i * nb + j, cast_ranges)
    outs = pl.pallas_call(
        functools.partial(_conv_proj_kernel, cast_blocks),
        grid=(t // tm, nb),
        in_specs=[pl.BlockSpec((tm, d), lambda i, j: (i, 0)),
                  w_spec(0), w_spec(nb), w_spec(2 * nb)] + w_in_specs(),
        out_specs=[pl.BlockSpec((tm, tn), lambda i, j: (i, j))] * 2 + w_out_specs(),
        out_shape=[out, out] + w_shapes,
        compiler_params=_cparams("arbitrary", "arbitrary"),
        name="conv_proj",
    )(h, w_in_t, w_in_t, w_in_t, *cast_weights)
    return outs[0], outs[1], outs[2:]


def _gate_proj_kernel(cast_blocks, h_ref, w_ref, b_ref, *refs):
    n_cast = len(cast_blocks)
    w_refs, o_ref, wo_refs = refs[:n_cast], refs[n_cast], refs[n_cast + 1:]
    _cast_step(pl.program_id(0) * pl.num_programs(1) + pl.program_id(1), cast_blocks, w_refs, wo_refs)
    z = _dot_nt(h_ref[...], w_ref[...]) + b_ref[...]
    o_ref[...] = (1.0 / (1.0 + jnp.exp(-z))).astype(o_ref.dtype)


def _gate_proj(h, w_in_t, gate_row, b_gate, cast_weights, tm, tn):
    t, d = h.shape
    n = b_gate.shape[1]
    nb = n // tn
    cast_blocks, w_in_specs, w_out_specs, w_shapes = _cast_plan(
        cast_weights, (t // tm) * nb, lambda i, j: i * nb + j)
    outs = pl.pallas_call(
        functools.partial(_gate_proj_kernel, cast_blocks),
        grid=(t // tm, nb),
        in_specs=[pl.BlockSpec((tm, d), lambda i, j: (i, 0)),
                  pl.BlockSpec((pl.Element(tn), pl.Element(d)),
                               lambda i, j: (pl.multiple_of(gate_row + j * tn, math.gcd(gate_row, tn)), 0)),
                  pl.BlockSpec((1, tn), lambda i, j: (0, j))] + w_in_specs(),
        out_specs=[pl.BlockSpec((tm, tn), lambda i, j: (i, j))] + w_out_specs(),
        out_shape=[jax.ShapeDtypeStruct((t, n), BF16)] + w_shapes,
        compiler_params=_cparams("arbitrary", "arbitrary"),
        name="gate_proj",
    )(h, w_in_t, b_gate, *cast_weights)
    return outs[0], outs[1:]


def _rope_tables(pos_ref, invf_ref, sgn_ref):
    ang = pos_ref[...].astype(F32) * invf_ref[...]
    return jnp.cos(ang), jnp.sin(ang) * sgn_ref[...]


def _q_proj_kernel(n_heads, h_ref, pos_ref, invf_ref, sgn_ref, wqa_ref, gq_ref, wq_ref, q_ref):
    qn = _rms(_dot_nt(h_ref[...], wqa_ref[...]), gq_ref[...]).astype(BF16)
    z = _dot(qn, wq_ref[...])
    rope0 = n_heads * QK_NOPE_DIM
    swap0 = rope0 + n_heads * QK_ROPE_DIM
    cos, sin = _rope_tables(pos_ref, invf_ref, sgn_ref)
    cos, sin = cos * Q_SCALE, sin * Q_SCALE
    first = lax.broadcasted_iota(jnp.int32, (1, LANE), 1) < QK_ROPE_DIM
    for pair in range(n_heads // 2):
        g = pair * LANE
        rot = z[:, rope0 + g:rope0 + g + LANE] * cos + z[:, swap0 + g:swap0 + g + LANE] * sin
        for k, r in enumerate((rot, pltpu.roll(rot, QK_ROPE_DIM, axis=1))):
            hd = 2 * pair + k
            a = hd * HEAD_PAD
            nope = z[:, hd * QK_NOPE_DIM:(hd + 1) * QK_NOPE_DIM] * Q_SCALE
            q_ref[:, a:a + LANE] = nope.astype(q_ref.dtype)
            q_ref[:, a + LANE:a + HEAD_PAD] = jnp.where(first, r, 0.0).astype(q_ref.dtype)


def _q_proj(h, pos, invf, sgn, w_in_t, qa_col, g_q, wq, n_heads, tm):
    t, d = h.shape
    q_rank = g_q.shape[1]
    assert qa_col % q_rank == 0 and n_heads % 2 == 0
    const = lambda i: (0, 0)
    return pl.pallas_call(
        functools.partial(_q_proj_kernel, n_heads),
        grid=(t // tm,),
        in_specs=[pl.BlockSpec((tm, d), lambda i: (i, 0)),
                  pl.BlockSpec((tm, 1), lambda i: (i, 0)),
                  _resident((1, LANE), const),
                  _resident((1, LANE), const),
                  _resident((q_rank, d), lambda i: (qa_col // q_rank, 0)),
                  _resident(g_q.shape, const),
                  _resident(wq.shape, const)],
        out_specs=pl.BlockSpec((tm, n_heads * HEAD_PAD), lambda i: (i, 0)),
        out_shape=jax.ShapeDtypeStruct((t, n_heads * HEAD_PAD), BF16),
        compiler_params=_cparams("parallel"),
        name="q_proj",
    )(h, pos, invf, sgn, w_in_t, g_q, wq)


def _kv_proj_kernel(n_heads, x_ref, gm_ref, pos_ref, invf_ref, sgn_ref, wkvx_ref, gkv_ref,
                    wkt_ref, wv_ref, h_ref, kt_ref, v_ref):
    h = _rms(x_ref[...], gm_ref[...]).astype(h_ref.dtype)
    h_ref[...] = h
    kv_rank = gkv_ref.shape[1]
    z = _dot_nt(h, wkvx_ref[...])
    kvn = _rms(z[:, :kv_rank], gkv_ref[...]).astype(BF16)
    v_ref[...] = _dot(kvn, wv_ref[...]).astype(v_ref.dtype)
    knt = lax.dot_general(wkt_ref[...], kvn, (((1,), (1,)), ((), ())),
                          preferred_element_type=F32)
    cos, sin = _rope_tables(pos_ref, invf_ref, sgn_ref)
    krot = z[:, kv_rank:kv_rank + LANE] * cos + z[:, kv_rank + LANE:] * sin
    krt = krot.T.astype(kt_ref.dtype)
    for hd in range(n_heads):
        kt_ref[0, hd, 0:LANE, :] = knt[hd * LANE:(hd + 1) * LANE, :].astype(kt_ref.dtype)
        kt_ref[0, hd, LANE:HEAD_PAD, :] = krt


def _kv_proj(x, g_mix, pos, invf, sgn, wkvx, g_kv, wkt, wv, n_heads, batch, seq, tm):
    t, d = x.shape
    spt = seq // tm
    const = lambda i: (0, 0)
    return pl.pallas_call(
        functools.partial(_kv_proj_kernel, n_heads),
        grid=(t // tm,),
        in_specs=[pl.BlockSpec((tm, d), lambda i: (i, 0)),
                  _resident(g_mix.shape, const),
                  pl.BlockSpec((tm, 1), lambda i: (i, 0)),
                  _resident((1, LANE), const),
                  _resident((1, LANE), const),
                  _resident(wkvx.shape, const),
                  _resident(g_kv.shape, const),
                  _resident(wkt.shape, const),
                  _resident(wv.shape, const)],
        out_specs=[pl.BlockSpec((tm, d), lambda i: (i, 0)),
                   pl.BlockSpec((1, n_heads, HEAD_PAD, tm), lambda i: (i // spt, 0, 0, i % spt)),
                   pl.BlockSpec((tm, n_heads * V_HEAD_DIM), lambda i: (i, 0))],
        out_shape=[jax.ShapeDtypeStruct((t, d), BF16),
                   jax.ShapeDtypeStruct((batch, n_heads, HEAD_PAD, seq), BF16),
                   jax.ShapeDtypeStruct((t, n_heads * V_HEAD_DIM), BF16)],
        compiler_params=_cparams("parallel"),
        name="kv_proj",
    )(x, g_mix, pos, invf, sgn, wkvx, g_kv, wkt, wv)


def _attn_kernel(kc, cast_blocks, q_ref, kt_ref, v_ref, *refs):
    n_cast = len(cast_blocks)
    w_refs, o_ref = refs[:n_cast], refs[n_cast]
    wo_refs, vx_ref = refs[n_cast + 1:2 * n_cast + 1], refs[2 * n_cast + 1]
    step = (pl.program_id(0) * pl.num_programs(1) + pl.program_id(1)) * pl.num_programs(2) + pl.program_id(2)
    _cast_step(step, cast_blocks, w_refs, wo_refs)

    dv = o_ref.shape[-1]

    @pl.when(step == 0)
    def _():
        lane = lax.broadcasted_iota(jnp.int32, (vx_ref.shape[0], vx_ref.shape[1] - dv), 1)
        vx_ref[:, dv:] = jnp.where(lane == 0, 1.0, 0.0).astype(vx_ref.dtype)

    @pl.when(pl.program_id(2) == 0)
    def _():
        vx_ref[:, :dv] = v_ref[...]

    q = q_ref[...]
    tq = q.shape[0]
    seq = kt_ref.shape[-1]
    m = jnp.full((tq, 1), -jnp.inf, F32)
    acc = jnp.zeros((tq, vx_ref.shape[-1]), F32)
    for c0 in range(0, seq, kc):
        s = _dot(q, kt_ref[0, 0, :, c0:c0 + kc])
        m_new = jnp.maximum(m, jnp.max(s, axis=-1, keepdims=True))
        alpha = jnp.exp2(m - m_new)
        p = jnp.exp2(s - m_new).astype(BF16)
        acc = alpha * acc + _dot(p, vx_ref[c0:c0 + kc, :])
        m = m_new
    o_ref[...] = (acc[:, :dv] / acc[:, dv:dv + 1]).astype(o_ref.dtype)


def _attention(q, kt, v, cast_weights, n_heads, batch, seq, tq, kc):
    t = q.shape[0]
    qpt = seq // tq
    cast_blocks, w_in_specs, w_out_specs, w_shapes = _cast_plan(
        cast_weights, batch * n_heads * qpt, lambda b, hd, i: (b * n_heads + hd) * qpt + i)
    outs = pl.pallas_call(
        functools.partial(_attn_kernel, kc, cast_blocks),
        grid=(batch, n_heads, qpt),
        in_specs=[pl.BlockSpec((tq, HEAD_PAD), lambda b, hd, i: (b * qpt + i, hd)),
                  pl.BlockSpec((1, 1, HEAD_PAD, seq), lambda b, hd, i: (b, hd, 0, 0)),
                  pl.BlockSpec((seq, V_HEAD_DIM), lambda b, hd, i: (b, hd))] + w_in_specs(),
        out_specs=[pl.BlockSpec((tq, V_HEAD_DIM), lambda b, hd, i: (b * qpt + i, hd))] + w_out_specs(),
        out_shape=[jax.ShapeDtypeStruct((t, n_heads * V_HEAD_DIM), BF16)] + w_shapes,
        scratch_shapes=[pltpu.VMEM((seq, HEAD_PAD), BF16)],
        compiler_params=_cparams("arbitrary", "arbitrary", "arbitrary"),
        name="attention",
    )(q, kt, v, *cast_weights)
    return outs[0], outs[1:]


def _branch_kernel(tiles_per_seq, cchunk, cb_ref, u_ref, up_ref, un_ref, yb_ref, cw_ref,
                   ga_ref, gb_ref, w0_ref, w1_ref, o_ref, ya_ref):
    i = pl.program_id(0)
    j = pl.program_id(1)
    tm, c = u_ref.shape

    def project():
        pa = _dot(ya_ref[...], w0_ref[0])
        pb = _dot(yb_ref[...], w1_ref[0])
        o_ref[...] = (ga_ref[...].astype(F32) * pa + gb_ref[...].astype(F32) * pb).astype(o_ref.dtype)

    @pl.when(j == 0)
    def _():
        keep_prev = (i % tiles_per_seq != 0).astype(F32)
        keep_next = (i % tiles_per_seq != tiles_per_seq - 1).astype(F32)
        row = lax.broadcasted_iota(jnp.int32, (tm, 1), 0)
        for c0 in range(0, c, cchunk):
            cs = slice(c0, c0 + cchunk)
            u = u_ref[:, cs].astype(F32)
            prev_row = up_ref[HALO_ROWS - 1:HALO_ROWS, cs].astype(F32) * keep_prev
            next_row = un_ref[0:1, cs].astype(F32) * keep_next
            u_dn = jnp.where(row == 0, prev_row, pltpu.roll(u, 1, axis=0))
            u_up = jnp.where(row == tm - 1, next_row, pltpu.roll(u, tm - 1, axis=0))
            conv = u_dn * cw_ref[0:1, cs] + u * cw_ref[1:2, cs] + u_up * cw_ref[2:3, cs]
            ya_ref[:, cs] = (cb_ref[:, cs].astype(F32) * conv).astype(ya_ref.dtype)
        project()

    @pl.when(j != 0)
    def _():
        project()


def _branch(cb, u, yb, conv_w, gates, w_br, seq, tm, tn):
    t, c = cb.shape
    d = w_br.shape[2]
    nb = d // tn
    hb = tm // HALO_ROWS
    last_hb = t // HALO_ROWS - 1
    row = lambda i, j: (i, 0)
    return pl.pallas_call(
        functools.partial(_branch_kernel, seq // tm, _pick(CONV_CHUNK, c)),
        grid=(t // tm, nb),
        in_specs=[pl.BlockSpec((tm, c), row),
                  pl.BlockSpec((tm, c), row),
                  pl.BlockSpec((HALO_ROWS, c), lambda i, j: (jnp.maximum(i * hb - 1, 0), 0)),
                  pl.BlockSpec((HALO_ROWS, c), lambda i, j: (jnp.minimum((i + 1) * hb, last_hb), 0)),
                  pl.BlockSpec((tm, c), row),
                  pl.BlockSpec(conv_w.shape, lambda i, j: (0, 0)),
                  pl.BlockSpec((tm, tn), lambda i, j: (i, j)),
                  pl.BlockSpec((tm, tn), lambda i, j: (i, j + nb)),
                  pl.BlockSpec((1, c, tn), lambda i, j: (0, 0, j)),
                  pl.BlockSpec((1, c, tn), lambda i, j: (1, 0, j))],
        out_specs=pl.BlockSpec((tm, tn), lambda i, j: (i, j)),
        out_shape=jax.ShapeDtypeStruct((t, d), BF16),
        scratch_shapes=[pltpu.VMEM((tm, c), BF16)],
        compiler_params=_cparams("parallel", "arbitrary"),
        name="branch",
    )(cb, u, u, u, yb, conv_w, gates, gates, w_br, w_br)


def _out_proj_kernel(m_ref, w_ref, x_ref, o_ref):
    o_ref[...] = x_ref[...] + _dot(m_ref[...], w_ref[...])


def _out_proj(m, w_out, x, tm, tn):
    t, d = m.shape
    n = w_out.shape[1]
    return pl.pallas_call(
        _out_proj_kernel,
        grid=(t // tm, n // tn),
        in_specs=[pl.BlockSpec((tm, d), lambda i, j: (i, 0)),
                  pl.BlockSpec((d, tn), lambda i, j: (0, j)),
                  pl.BlockSpec((tm, tn), lambda i, j: (i, j))],
        out_specs=pl.BlockSpec((tm, tn), lambda i, j: (i, j)),
        out_shape=jax.ShapeDtypeStruct((t, n), F32),
        compiler_params=_cparams("parallel", "arbitrary"),
        name="out_proj",
    )(m, w_out, x)


def _ffn_kernel(final_norm, x_hbm, gf_ref, wg_ref, wu_ref, wd_ref, gl_ref, o_ref, h2_ref, x_ref, x_sem):
    i = pl.program_id(0)
    j = pl.program_id(1)
    last = pl.num_programs(1) - 1
    tm = x_ref.shape[0]

    def x_copy(tile):
        rows = pl.ds(pl.multiple_of(tile * tm, tm), tm)
        return pltpu.make_async_copy(x_hbm.at[rows, :], x_ref, x_sem)
    step = _pick(NORM_ROWS, tm)
    chunks = [slice(r0, r0 + step) for r0 in range(0, tm, step)]

    def accumulate():
        h2 = h2_ref[...]
        g = _dot(h2, wg_ref[...])
        a = (g * (1.0 / (1.0 + jnp.exp(-g)))) * _dot(h2, wu_ref[...])
        o_ref[...] += _dot(a.astype(BF16), wd_ref[...])

    @pl.when(jnp.logical_and(i == 0, j == 0))
    def _():
        x_copy(0).start()

    @pl.when(j == 0)
    def _():
        x_copy(i).wait()
        for rs in chunks:
            x = x_ref[rs, :]
            h2_ref[rs, :] = _rms(x, gf_ref[...]).astype(h2_ref.dtype)
            o_ref[rs, :] = x
        accumulate()

    @pl.when(jnp.logical_and(j > 0, j < last))
    def _():
        accumulate()

    @pl.when(jnp.logical_and(j == last, i + 1 < pl.num_programs(0)))
    def _():
        x_copy(i + 1).start()

    @pl.when(j == last)
    def _():
        accumulate()
        if final_norm:
            for rs in chunks:
                o_ref[rs, :] = _rms(o_ref[rs, :], gl_ref[...])


def _ffn(x1, g_ffn, w_g, w_u, w_d, g_final, final_norm, tm, tf):
    t, d = x1.shape
    f = w_g.shape[1]
    nf = f // tf
    assert f % tf == 0 and nf >= 2
    return pl.pallas_call(
        functools.partial(_ffn_kernel, final_norm),
        grid=(t // tm, nf),
        in_specs=[pl.BlockSpec(memory_space=pl.ANY),
                  pl.BlockSpec((1, d), lambda i, j: (0, 0)),
                  pl.BlockSpec((d, tf), lambda i, j: (0, j)),
                  pl.BlockSpec((d, tf), lambda i, j: (0, j)),
                  pl.BlockSpec((tf, d), lambda i, j: (j, 0)),
                  pl.BlockSpec((1, d), lambda i, j: (0, 0))],
        out_specs=_resident((tm, d), lambda i, j: (i, 0)),
        out_shape=jax.ShapeDtypeStruct((t, d), F32),
        scratch_shapes=[pltpu.VMEM((tm, d), BF16), pltpu.VMEM((tm, d), F32), pltpu.SemaphoreType.DMA(())],
        compiler_params=_cparams("arbitrary", "arbitrary"),
        name="ffn",
    )(x1, g_ffn, w_g, w_u, w_d, g_final)


def _swap_halves(w):
    half = w.shape[-1] // 2
    return jnp.concatenate([w[..., half:], w[..., :half]], axis=-1)


def _q_weights(w_q_b, n_heads):
    r = w_q_b.shape[0]
    w = w_q_b.reshape(r, n_heads, QK_HEAD_DIM)
    nope, rope = w[..., :QK_NOPE_DIM], w[..., QK_NOPE_DIM:]
    parts = [nope.reshape(r, -1), rope.reshape(r, -1), _swap_halves(rope).reshape(r, -1)]
    return jnp.concatenate(parts, axis=1).astype(BF16)


def _pick(pref, n):
    if n <= pref:
        return n
    t = pref
    while n % t:
        t //= 2
    return t


class _Tiles(NamedTuple):
    kv_rows: int
    q_rows: int
    proj_rows: int
    conv_cols: int
    gate_cols: int
    attn_q: int
    attn_k: int
    branch_rows: int
    branch_cols: int
    ffn_rows: int
    ffn_cols: int


def _tiles(t, seq, d, conv_dim, d_ff):
    return _Tiles(kv_rows=_pick(256, seq), q_rows=_pick(512, seq), proj_rows=_pick(1024, t),
                  conv_cols=_pick(512, conv_dim), gate_cols=_pick(1024, d),
                  attn_q=_pick(1024, seq), attn_k=_pick(256, seq),
                  branch_rows=_pick(1024, seq), branch_cols=_pick(512, d),
                  ffn_rows=_pick(1024, t), ffn_cols=_pick(256, d_ff))


def kernel(x, positions, g_mix, w_in, b_gate, conv_w, g_q_a, w_q_b, g_kv_a, w_kv_b, w_branch,
           w_out, g_ffn, w_ffn_gate, w_ffn_up, w_ffn_down, g_final):
    batch, seq, d = x.shape
    depth = w_in.shape[0]
    t = batch * seq
    conv_dim = conv_w.shape[-1]
    q_rank = g_q_a.shape[-1]
    kv_rank = g_kv_a.shape[-1]
    n_heads = w_q_b.shape[-1] // QK_HEAD_DIM
    qa_col = 3 * conv_dim
    kva_col = qa_col + q_rank
    kr_col = kva_col + kv_rank
    gate_col = kr_col + QK_ROPE_DIM

    tl = _tiles(t, seq, d, conv_dim, w_ffn_gate.shape[-1])

    xf = x.reshape(t, d)
    pos = positions.reshape(t, 1)
    inv_freq = ROPE_THETA ** (-jnp.arange(0, QK_ROPE_DIM, 2, dtype=F32) / QK_ROPE_DIM)
    reps = 2 * LANE // QK_ROPE_DIM
    half = QK_ROPE_DIM // 2
    invf = jnp.tile(inv_freq, reps)[None, :]
    sgn = jnp.tile(jnp.concatenate([-jnp.ones((half,), F32), jnp.ones((half,), F32)]), reps // 2)[None, :]

    for l in range(depth):
        w_in_f32_t = jnp.swapaxes(w_in[l], 0, 1)
        w_in_t = _cast_head_rows(w_in_f32_t, kr_col, _pick(CAST_ROWS, kr_col))
        n_tail = w_in_f32_t.shape[0] - kr_col
        conv_steps = (t // tl.proj_rows) * (conv_dim // tl.conv_cols)
        tail_in_conv = _cast_rows(n_tail, conv_steps, kr_col) is not None
        w_kr_t = jnp.swapaxes(w_in[l][:, kr_col:gate_col], 0, 1)
        w_krs_t = jnp.concatenate([w_kr_t[half:], w_kr_t[:half]], axis=0)
        pad_t = jnp.zeros((LANE - QK_ROPE_DIM, d), F32)
        wkvx_t = jnp.concatenate([w_in_t[kva_col:kr_col],
                                  jnp.concatenate([w_kr_t, pad_t, w_krs_t, pad_t], axis=0).astype(BF16)],
                                 axis=0)
        wq = _q_weights(w_q_b[l], n_heads)
        w_kv = w_kv_b[l].reshape(kv_rank, n_heads, QK_NOPE_DIM + V_HEAD_DIM)
        wkt = w_kv[..., :QK_NOPE_DIM].reshape(kv_rank, n_heads * QK_NOPE_DIM).T.astype(BF16)
        wv = w_kv[..., QK_NOPE_DIM:].reshape(kv_rank, n_heads * V_HEAD_DIM).astype(BF16)

        h, kt, v = _kv_proj(xf, g_mix[l][None, :], pos, invf, sgn, wkvx_t, g_kv_a[l][None, :], wkt, wv,
                            n_heads, batch, seq, tl.kv_rows)
        if tail_in_conv:
            cb, u, (w_tail_t,) = _conv_proj(h, w_in_t, conv_dim, (w_in_f32_t,), [(kr_col, n_tail)],
                                            tl.proj_rows, tl.conv_cols)
        else:
            cb, u, _ = _conv_proj(h, w_in_t, conv_dim, (), [], tl.proj_rows, tl.conv_cols)
            w_tail_t = w_in_f32_t[kr_col:].astype(BF16)
        gates, (w_br, w_o) = _gate_proj(h, w_tail_t, gate_col - kr_col, b_gate[l][None, :],
                                        (w_branch[l].reshape(-1, d), w_out[l]), tl.proj_rows, tl.gate_cols)
        w_br = w_br.reshape(w_branch.shape[1:])
        q = _q_proj(h, pos, invf, sgn, w_in_t, qa_col, g_q_a[l][None, :], wq, n_heads, tl.q_rows)
        yb, (w_fg, w_fu, w_fd) = _attention(q, kt, v, (w_ffn_gate[l], w_ffn_up[l], w_ffn_down[l]),
                                            n_heads, batch, seq, tl.attn_q, tl.attn_k)
        m = _branch(cb, u, yb, conv_w[l], gates, w_br, seq, tl.branch_rows, tl.branch_cols)
        xf = _out_proj(m, w_o, xf, tl.proj_rows, tl.gate_cols)
        xf = _ffn(xf, g_ffn[l][None, :], w_fg, w_fu, w_fd, g_final[None, :], l == depth - 1,
                  tl.ffn_rows, tl.ffn_cols)
    return xf.reshape(batch, seq, d)
```

```python
import functools
import math
from typing import NamedTuple

import jax
import jax.numpy as jnp
from jax import lax
from jax.experimental import pallas as pl
from jax.experimental.pallas import tpu as pltpu

F32 = jnp.float32
BF16 = jnp.bfloat16

RMS_EPS = 1e-6
ROPE_THETA = 10000.0
QK_NOPE_DIM = 128
QK_ROPE_DIM = 64
V_HEAD_DIM = 128
QK_HEAD_DIM = QK_NOPE_DIM + QK_ROPE_DIM
Q_SCALE = math.log2(math.e) / math.sqrt(QK_HEAD_DIM)
HEAD_PAD = 256
LANE = 128
BF16_ROWS = 16
HALO_ROWS = BF16_ROWS
CONV_CHUNK = 512
NORM_ROWS = 256
CAST_ROWS = 512
VMEM_LIMIT = 63 * 1024 * 1024


def _cparams(*sem):
    return pltpu.CompilerParams(dimension_semantics=sem, vmem_limit_bytes=VMEM_LIMIT)


def _resident(block_shape, index_map):
    return pl.BlockSpec(block_shape, index_map, pipeline_mode=pl.Buffered(1))


def _dot(a, b):
    return jnp.dot(a, b, preferred_element_type=F32)


def _dot_nt(a, b):
    return lax.dot_general(a, b, (((1,), (1,)), ((), ())), preferred_element_type=F32)


def _rms(x, g):
    inv = lax.rsqrt(jnp.mean(x * x, axis=-1, keepdims=True) + RMS_EPS)
    return x * inv * g


def _cast_rows(rows, steps, first_row=0):
    for rb in range(BF16_ROWS, rows + 1, BF16_ROWS):
        if rows % rb == 0 and first_row % rb == 0 and rows // rb <= steps:
            return rb
    return None


def _cast_plan(weights, steps, step_of, row_ranges=None):
    ranges = row_ranges or [(0, w.shape[0]) for w in weights]
    rows = [_cast_rows(n, steps, r0) for r0, n in ranges]
    assert all(rows), "no block size fits; cast that weight outside"
    blocks = tuple(n // rb for (_, n), rb in zip(ranges, rows))

    def specs(with_offset):
        return [pl.BlockSpec((rb, w.shape[1]),
                             lambda *idx, nblk=nblk, off=(r0 // rb if with_offset else 0):
                                 (off + jnp.minimum(step_of(*idx), nblk - 1), 0))
                for w, rb, nblk, (r0, _) in zip(weights, rows, blocks, ranges)]
    shapes = [jax.ShapeDtypeStruct((n, w.shape[1]), BF16) for w, (_, n) in zip(weights, ranges)]
    return blocks, functools.partial(specs, True), functools.partial(specs, False), shapes


def _cast_step(step, blocks, w_refs, wo_refs):
    for w_ref, wo_ref, nblk in zip(w_refs, wo_refs, blocks):
        @pl.when(step < nblk)
        def _(w_ref=w_ref, wo_ref=wo_ref):
            wo_ref[...] = w_ref[...].astype(wo_ref.dtype)


def _cast_kernel(w_ref, o_ref):
    o_ref[...] = w_ref[...].astype(o_ref.dtype)


def _cast_head_rows(w, n_rows, rb):
    assert n_rows % rb == 0
    return pl.pallas_call(
        _cast_kernel,
        grid=(n_rows // rb,),
        in_specs=[pl.BlockSpec((rb, w.shape[1]), lambda i: (i, 0))],
        out_specs=pl.BlockSpec((rb, w.shape[1]), lambda i: (i, 0)),
        out_shape=jax.ShapeDtypeStruct((n_rows, w.shape[1]), BF16),
        compiler_params=_cparams("parallel"),
        name="cast_head",
    )(w)


def _conv_proj_kernel(cast_blocks, h_ref, wb_ref, wc_ref, wh_ref, *refs):
    n_cast = len(cast_blocks)
    w_refs, (cb_ref, u_ref), wo_refs = refs[:n_cast], refs[n_cast:n_cast + 2], refs[n_cast + 2:]
    _cast_step(pl.program_id(0) * pl.num_programs(1) + pl.program_id(1), cast_blocks, w_refs, wo_refs)
    h = h_ref[...]
    cb_ref[...] = _dot_nt(h, wb_ref[...]).astype(cb_ref.dtype)
    u_ref[...] = (_dot_nt(h, wc_ref[...]) * _dot_nt(h, wh_ref[...])).astype(u_ref.dtype)


def _conv_proj(h, w_in_t, conv_dim, cast_weights, cast_ranges, tm, tn):
    t, d = h.shape
    nb = conv_dim // tn
    w_spec = lambda off: pl.BlockSpec((tn, d), lambda i, j: (j + off, 0))
    out = jax.ShapeDtypeStruct((t, conv_dim), BF16)
    cast_blocks, w_in_specs, w_out_specs, w_shapes = _cast_plan(
        cast_weights, (t // tm) * nb, lambda i, j: i * nb + j, cast_ranges)
    outs = pl.pallas_call(
        functools.partial(_conv_proj_kernel, cast_blocks),
        grid=(t // tm, nb),
        in_specs=[pl.BlockSpec((tm, d), lambda i, j: (i, 0)),
                  w_spec(0), w_spec(nb), w_spec(2 * nb)] + w_in_specs(),
        out_specs=[pl.BlockSpec((tm, tn), lambda i, j: (i, j))] * 2 + w_out_specs(),
        out_shape=[out, out] + w_shapes,
        compiler_params=_cparams("arbitrary", "arbitrary"),
        name="conv_proj",
    )(h, w_in_t, w_in_t, w_in_t, *cast_weights)
    return outs[0], outs[1], outs[2:]


def _gate_proj_kernel(cast_blocks, h_ref, w_ref, b_ref, *refs):
    n_cast = len(cast_blocks)
    w_refs, o_ref, wo_refs = refs[:n_cast], refs[n_cast], refs[n_cast + 1:]
    _cast_step(pl.program_id(0) * pl.num_programs(1) + pl.program_id(1), cast_blocks, w_refs, wo_refs)
    z = _dot_nt(h_ref[...], w_ref[...]) + b_ref[...]
    o_ref[...] = (1.0 / (1.0 + jnp.exp(-z))).astype(o_ref.dtype)


def _gate_proj(h, w_in_t, gate_row, b_gate, cast_weights, tm, tn):
    t, d = h.shape
    n = b_gate.shape[1]
    nb = n // tn
    cast_blocks, w_in_specs, w_out_specs, w_shapes = _cast_plan(
        cast_weights, (t // tm) * nb, lambda i, j: i * nb + j)
    outs = pl.pallas_call(
        functools.partial(_gate_proj_kernel, cast_blocks),
        grid=(t // tm, nb),
        in_specs=[pl.BlockSpec((tm, d), lambda i, j: (i, 0)),
                  pl.BlockSpec((pl.Element(tn), pl.Element(d)),
                               lambda i, j: (pl.multiple_of(gate_row + j * tn, math.gcd(gate_row, tn)), 0)),
                  pl.BlockSpec((1, tn), lambda i, j: (0, j))] + w_in_specs(),
        out_specs=[pl.BlockSpec((tm, tn), lambda i, j: (i, j))] + w_out_specs(),
        out_shape=[jax.ShapeDtypeStruct((t, n), BF16)] + w_shapes,
        compiler_params=_cparams("arbitrary", "arbitrary"),
        name="gate_proj",
    )(h, w_in_t, b_gate, *cast_weights)
    return outs[0], outs[1:]


def _rope_tables(pos_ref, invf_ref, sgn_ref):
    ang = pos_ref[...].astype(F32) * invf_ref[...]
    return jnp.cos(ang), jnp.sin(ang) * sgn_ref[...]


def _q_proj_kernel(n_heads, h_ref, pos_ref, invf_ref, sgn_ref, wqa_ref, gq_ref, wq_ref, q_ref):
    qn = _rms(_dot_nt(h_ref[...], wqa_ref[...]), gq_ref[...]).astype(BF16)
    z = _dot(qn, wq_ref[...])
    rope0 = n_heads * QK_NOPE_DIM
    swap0 = rope0 + n_heads * QK_ROPE_DIM
    cos, sin = _rope_tables(pos_ref, invf_ref, sgn_ref)
    cos, sin = cos * Q_SCALE, sin * Q_SCALE
    first = lax.broadcasted_iota(jnp.int32, (1, LANE), 1) < QK_ROPE_DIM
    for pair in range(n_heads // 2):
        g = pair * LANE
        rot = z[:, rope0 + g:rope0 + g + LANE] * cos + z[:, swap0 + g:swap0 + g + LANE] * sin
        for k, r in enumerate((rot, pltpu.roll(rot, QK_ROPE_DIM, axis=1))):
            hd = 2 * pair + k
            a = hd * HEAD_PAD
            nope = z[:, hd * QK_NOPE_DIM:(hd + 1) * QK_NOPE_DIM] * Q_SCALE
            q_ref[:, a:a + LANE] = nope.astype(q_ref.dtype)
            q_ref[:, a + LANE:a + HEAD_PAD] = jnp.where(first, r, 0.0).astype(q_ref.dtype)


def _q_proj(h, pos, invf, sgn, w_in_t, qa_col, g_q, wq, n_heads, tm):
    t, d = h.shape
    q_rank = g_q.shape[1]
    assert qa_col % q_rank == 0 and n_heads % 2 == 0
    const = lambda i: (0, 0)
    return pl.pallas_call(
        functools.partial(_q_proj_kernel, n_heads),
        grid=(t // tm,),
        in_specs=[pl.BlockSpec((tm, d), lambda i: (i, 0)),
                  pl.BlockSpec((tm, 1), lambda i: (i, 0)),
                  _resident((1, LANE), const),
                  _resident((1, LANE), const),
                  _resident((q_rank, d), lambda i: (qa_col // q_rank, 0)),
                  _resident(g_q.shape, const),
                  _resident(wq.shape, const)],
        out_specs=pl.BlockSpec((tm, n_heads * HEAD_PAD), lambda i: (i, 0)),
        out_shape=jax.ShapeDtypeStruct((t, n_heads * HEAD_PAD), BF16),
        compiler_params=_cparams("parallel"),
        name="q_proj",
    )(h, pos, invf, sgn, w_in_t, g_q, wq)


def _kv_proj_kernel(n_heads, x_ref, gm_ref, pos_ref, invf_ref, sgn_ref, wkvx_ref, gkv_ref,
                    wkt_ref, wv_ref, h_ref, kt_ref, v_ref):
    h = _rms(x_ref[...], gm_ref[...]).astype(h_ref.dtype)
    h_ref[...] = h
    kv_rank = gkv_ref.shape[1]
    z = _dot_nt(h, wkvx_ref[...])
    kvn = _rms(z[:, :kv_rank], gkv_ref[...]).astype(BF16)
    v_ref[...] = _dot(kvn, wv_ref[...]).astype(v_ref.dtype)
    knt = lax.dot_general(wkt_ref[...], kvn, (((1,), (1,)), ((), ())),
                          preferred_element_type=F32)
    cos, sin = _rope_tables(pos_ref, invf_ref, sgn_ref)
    krot = z[:, kv_rank:kv_rank + LANE] * cos + z[:, kv_rank + LANE:] * sin
    krt = krot.T.astype(kt_ref.dtype)
    for hd in range(n_heads):
        kt_ref[0, hd, 0:LANE, :] = knt[hd * LANE:(hd + 1) * LANE, :].astype(kt_ref.dtype)
        kt_ref[0, hd, LANE:HEAD_PAD, :] = krt


def _kv_proj(x, g_mix, pos, invf, sgn, wkvx, g_kv, wkt, wv, n_heads, batch, seq, tm):
    t, d = x.shape
    spt = seq // tm
    const = lambda i: (0, 0)
    return pl.pallas_call(
        functools.partial(_kv_proj_kernel, n_heads),
        grid=(t // tm,),
        in_specs=[pl.BlockSpec((tm, d), lambda i: (i, 0)),
                  _resident(g_mix.shape, const),
                  pl.BlockSpec((tm, 1), lambda i: (i, 0)),
                  _resident((1, LANE), const),
                  _resident((1, LANE), const),
                  _resident(wkvx.shape, const),
                  _resident(g_kv.shape, const),
                  _resident(wkt.shape, const),
                  _resident(wv.shape, const)],
        out_specs=[pl.BlockSpec((tm, d), lambda i: (i, 0)),
                   pl.BlockSpec((1, n_heads, HEAD_PAD, tm), lambda i: (i // spt, 0, 0, i % spt)),
                   pl.BlockSpec((tm, n_heads * V_HEAD_DIM), lambda i: (i, 0))],
        out_shape=[jax.ShapeDtypeStruct((t, d), BF16),
                   jax.ShapeDtypeStruct((batch, n_heads, HEAD_PAD, seq), BF16),
                   jax.ShapeDtypeStruct((t, n_heads * V_HEAD_DIM), BF16)],
        compiler_params=_cparams("parallel"),
        name="kv_proj",
    )(x, g_mix, pos, invf, sgn, wkvx, g_kv, wkt, wv)


def _attn_kernel(kc, cast_blocks, q_ref, kt_ref, v_ref, *refs):
    n_cast = len(cast_blocks)
    w_refs, o_ref = refs[:n_cast], refs[n_cast]
    wo_refs, vx_ref = refs[n_cast + 1:2 * n_cast + 1], refs[2 * n_cast + 1]
    step = (pl.program_id(0) * pl.num_programs(1) + pl.program_id(1)) * pl.num_programs(2) + pl.program_id(2)
    _cast_step(step, cast_blocks, w_refs, wo_refs)

    dv = o_ref.shape[-1]

    @pl.when(step == 0)
    def _():
        lane = lax.broadcasted_iota(jnp.int32, (vx_ref.shape[0], vx_ref.shape[1] - dv), 1)
        vx_ref[:, dv:] = jnp.where(lane == 0, 1.0, 0.0).astype(vx_ref.dtype)

    @pl.when(pl.program_id(2) == 0)
    def _():
        vx_ref[:, :dv] = v_ref[...]

    q = q_ref[...]
    tq = q.shape[0]
    seq = kt_ref.shape[-1]
    m = jnp.full((tq, 1), -jnp.inf, F32)
    acc = jnp.zeros((tq, vx_ref.shape[-1]), F32)
    for c0 in range(0, seq, kc):
        s = _dot(q, kt_ref[0, 0, :, c0:c0 + kc])
        m_new = jnp.maximum(m, jnp.max(s, axis=-1, keepdims=True))
        alpha = jnp.exp2(m - m_new)
        p = jnp.exp2(s - m_new).astype(BF16)
        acc = alpha * acc + _dot(p, vx_ref[c0:c0 + kc, :])
        m = m_new
    o_ref[...] = (acc[:, :dv] / acc[:, dv:dv + 1]).astype(o_ref.dtype)


def _attention(q, kt, v, cast_weights, n_heads, batch, seq, tq, kc):
    t = q.shape[0]
    qpt = seq // tq
    cast_blocks, w_in_specs, w_out_specs, w_shapes = _cast_plan(
        cast_weights, batch * n_heads * qpt, lambda b, hd, i: (b * n_heads + hd) * qpt + i)
    outs = pl.pallas_call(
        functools.partial(_attn_kernel, kc, cast_blocks),
        grid=(batch, n_heads, qpt),
        in_specs=[pl.BlockSpec((tq, HEAD_PAD), lambda b, hd, i: (b * qpt + i, hd)),
                  pl.BlockSpec((1, 1, HEAD_PAD, seq), lambda b, hd, i: (b, hd, 0, 0)),
                  pl.BlockSpec((seq, V_HEAD_DIM), lambda b, hd, i: (b, hd))] + w_in_specs(),
        out_specs=[pl.BlockSpec((tq, V_HEAD_DIM), lambda b, hd, i: (b * qpt + i, hd))] + w_out_specs(),
        out_shape=[jax.ShapeDtypeStruct((t, n_heads * V_HEAD_DIM), BF16)] + w_shapes,
        scratch_shapes=[pltpu.VMEM((seq, HEAD_PAD), BF16)],
        compiler_params=_cparams("arbitrary", "arbitrary", "arbitrary"),
        name="attention",
    )(q, kt, v, *cast_weights)
    return outs[0], outs[1:]


def _branch_kernel(tiles_per_seq, cchunk, cb_ref, u_ref, up_ref, un_ref, yb_ref, cw_ref,
                   ga_ref, gb_ref, w0_ref, w1_ref, o_ref, ya_ref):
    i = pl.program_id(0)
    j = pl.program_id(1)
    tm, c = u_ref.shape

    def project():
        pa = _dot(ya_ref[...], w0_ref[0])
        pb = _dot(yb_ref[...], w1_ref[0])
        o_ref[...] = (ga_ref[...].astype(F32) * pa + gb_ref[...].astype(F32) * pb).astype(o_ref.dtype)

    @pl.when(j == 0)
    def _():
        keep_prev = (i % tiles_per_seq != 0).astype(F32)
        keep_next = (i % tiles_per_seq != tiles_per_seq - 1).astype(F32)
        row = lax.broadcasted_iota(jnp.int32, (tm, 1), 0)
        for c0 in range(0, c, cchunk):
            cs = slice(c0, c0 + cchunk)
            u = u_ref[:, cs].astype(F32)
            prev_row = up_ref[HALO_ROWS - 1:HALO_ROWS, cs].astype(F32) * keep_prev
            next_row = un_ref[0:1, cs].astype(F32) * keep_next
            u_dn = jnp.where(row == 0, prev_row, pltpu.roll(u, 1, axis=0))
            u_up = jnp.where(row == tm - 1, next_row, pltpu.roll(u, tm - 1, axis=0))
            conv = u_dn * cw_ref[0:1, cs] + u * cw_ref[1:2, cs] + u_up * cw_ref[2:3, cs]
            ya_ref[:, cs] = (cb_ref[:, cs].astype(F32) * conv).astype(ya_ref.dtype)
        project()

    @pl.when(j != 0)
    def _():
        project()


def _branch(cb, u, yb, conv_w, gates, w_br, seq, tm, tn):
    t, c = cb.shape
    d = w_br.shape[2]
    nb = d // tn
    hb = tm // HALO_ROWS
    last_hb = t // HALO_ROWS - 1
    row = lambda i, j: (i, 0)
    return pl.pallas_call(
        functools.partial(_branch_kernel, seq // tm, _pick(CONV_CHUNK, c)),
        grid=(t // tm, nb),
        in_specs=[pl.BlockSpec((tm, c), row),
                  pl.BlockSpec((tm, c), row),
                  pl.BlockSpec((HALO_ROWS, c), lambda i, j: (jnp.maximum(i * hb - 1, 0), 0)),
                  pl.BlockSpec((HALO_ROWS, c), lambda i, j: (jnp.minimum((i + 1) * hb, last_hb), 0)),
                  pl.BlockSpec((tm, c), row),
                  pl.BlockSpec(conv_w.shape, lambda i, j: (0, 0)),
                  pl.BlockSpec((tm, tn), lambda i, j: (i, j)),
                  pl.BlockSpec((tm, tn), lambda i, j: (i, j + nb)),
                  pl.BlockSpec((1, c, tn), lambda i, j: (0, 0, j)),
                  pl.BlockSpec((1, c, tn), lambda i, j: (1, 0, j))],
        out_specs=pl.BlockSpec((tm, tn), lambda i, j: (i, j)),
        out_shape=jax.ShapeDtypeStruct((t, d), BF16),
        scratch_shapes=[pltpu.VMEM((tm, c), BF16)],
        compiler_params=_cparams("parallel", "arbitrary"),
        name="branch",
    )(cb, u, u, u, yb, conv_w, gates, gates, w_br, w_br)


def _out_proj_kernel(m_ref, w_ref, x_ref, o_ref):
    o_ref[...] = x_ref[...] + _dot(m_ref[...], w_ref[...])


def _out_proj(m, w_out, x, tm, tn):
    t, d = m.shape
    n = w_out.shape[1]
    return pl.pallas_call(
        _out_proj_kernel,
        grid=(t // tm, n // tn),
        in_specs=[pl.BlockSpec((tm, d), lambda i, j: (i, 0)),
                  pl.BlockSpec((d, tn), lambda i, j: (0, j)),
                  pl.BlockSpec((tm, tn), lambda i, j: (i, j))],
        out_specs=pl.BlockSpec((tm, tn), lambda i, j: (i, j)),
        out_shape=jax.ShapeDtypeStruct((t, n), F32),
        compiler_params=_cparams("parallel", "arbitrary"),
        name="out_proj",
    )(m, w_out, x)


def _ffn_kernel(final_norm, x_hbm, gf_ref, wg_ref, wu_ref, wd_ref, gl_ref, o_ref, h2_ref, x_ref, x_sem):
    i = pl.program_id(0)
    j = pl.program_id(1)
    last = pl.num_programs(1) - 1
    tm = x_ref.shape[0]

    def x_copy(tile):
        rows = pl.ds(pl.multiple_of(tile * tm, tm), tm)
        return pltpu.make_async_copy(x_hbm.at[rows, :], x_ref, x_sem)
    step = _pick(NORM_ROWS, tm)
    chunks = [slice(r0, r0 + step) for r0 in range(0, tm, step)]

    def accumulate():
        h2 = h2_ref[...]
        g = _dot(h2, wg_ref[...])
        a = (g * (1.0 / (1.0 + jnp.exp(-g)))) * _dot(h2, wu_ref[...])
        o_ref[...] += _dot(a.astype(BF16), wd_ref[...])

    @pl.when(jnp.logical_and(i == 0, j == 0))
    def _():
        x_copy(0).start()

    @pl.when(j == 0)
    def _():
        x_copy(i).wait()
        for rs in chunks:
            x = x_ref[rs, :]
            h2_ref[rs, :] = _rms(x, gf_ref[...]).astype(h2_ref.dtype)
            o_ref[rs, :] = x
        accumulate()

    @pl.when(jnp.logical_and(j > 0, j < last))
    def _():
        accumulate()

    @pl.when(jnp.logical_and(j == last, i + 1 < pl.num_programs(0)))
    def _():
        x_copy(i + 1).start()

    @pl.when(j == last)
    def _():
        accumulate()
        if final_norm:
            for rs in chunks:
                o_ref[rs, :] = _rms(o_ref[rs, :], gl_ref[...])


def _ffn(x1, g_ffn, w_g, w_u, w_d, g_final, final_norm, tm, tf):
    t, d = x1.shape
    f = w_g.shape[1]
    nf = f // tf
    assert f % tf == 0 and nf >= 2
    return pl.pallas_call(
        functools.partial(_ffn_kernel, final_norm),
        grid=(t // tm, nf),
        in_specs=[pl.BlockSpec(memory_space=pl.ANY),
                  pl.BlockSpec((1, d), lambda i, j: (0, 0)),
                  pl.BlockSpec((d, tf), lambda i, j: (0, j)),
                  pl.BlockSpec((d, tf), lambda i, j: (0, j)),
                  pl.BlockSpec((tf, d), lambda i, j: (j, 0)),
                  pl.BlockSpec((1, d), lambda i, j: (0, 0))],
        out_specs=_resident((tm, d), lambda i, j: (i, 0)),
        out_shape=jax.ShapeDtypeStruct((t, d), F32),
        scratch_shapes=[pltpu.VMEM((tm, d), BF16), pltpu.VMEM((tm, d), F32), pltpu.SemaphoreType.DMA(())],
        compiler_params=_cparams("arbitrary", "arbitrary"),
        name="ffn",
    )(x1, g_ffn, w_g, w_u, w_d, g_final)


def _swap_halves(w):
    half = w.shape[-1] // 2
    return jnp.concatenate([w[..., half:], w[..., :half]], axis=-1)


def _q_weights(w_q_b, n_heads):
    r = w_q_b.shape[0]
    w = w_q_b.reshape(r, n_heads, QK_HEAD_DIM)
    nope, rope = w[..., :QK_NOPE_DIM], w[..., QK_NOPE_DIM:]
    parts = [nope.reshape(r, -1), rope.reshape(r, -1), _swap_halves(rope).reshape(r, -1)]
    return jnp.concatenate(parts, axis=1).astype(BF16)


def _pick(pref, n):
    if n <= pref:
        return n
    t = pref
    while n % t:
        t //= 2
    return t


class _Tiles(NamedTuple):
    kv_rows: int
    q_rows: int
    proj_rows: int
    conv_cols: int
    gate_cols: int
    attn_q: int
    attn_k: int
    branch_rows: int
    branch_cols: int
    ffn_rows: int
    ffn_cols: int


def _tiles(t, seq, d, conv_dim, d_ff):
    return _Tiles(kv_rows=_pick(512, seq), q_rows=_pick(512, seq), proj_rows=_pick(1024, t),
                  conv_cols=_pick(512, conv_dim), gate_cols=_pick(1024, d),
                  attn_q=_pick(1024, seq), attn_k=_pick(256, seq),
                  branch_rows=_pick(1024, seq), branch_cols=_pick(512, d),
                  ffn_rows=_pick(1024, t), ffn_cols=_pick(256, d_ff))


def kernel(x, positions, g_mix, w_in, b_gate, conv_w, g_q_a, w_q_b, g_kv_a, w_kv_b, w_branch,
           w_out, g_ffn, w_ffn_gate, w_ffn_up, w_ffn_down, g_final):
    batch, seq, d = x.shape
    depth = w_in.shape[0]
    t = batch * seq
    conv_dim = conv_w.shape[-1]
    q_rank = g_q_a.shape[-1]
    kv_rank = g_kv_a.shape[-1]
    n_heads = w_q_b.shape[-1] // QK_HEAD_DIM
    qa_col = 3 * conv_dim
    kva_col = qa_col + q_rank
    kr_col = kva_col + kv_rank
    gate_col = kr_col + QK_ROPE_DIM

    tl = _tiles(t, seq, d, conv_dim, w_ffn_gate.shape[-1])

    xf = x.reshape(t, d)
    pos = positions.reshape(t, 1)
    inv_freq = ROPE_THETA ** (-jnp.arange(0, QK_ROPE_DIM, 2, dtype=F32) / QK_ROPE_DIM)
    reps = 2 * LANE // QK_ROPE_DIM
    half = QK_ROPE_DIM // 2
    invf = jnp.tile(inv_freq, reps)[None, :]
    sgn = jnp.tile(jnp.concatenate([-jnp.ones((half,), F32), jnp.ones((half,), F32)]), reps // 2)[None, :]

    for l in range(depth):
        w_in_f32_t = jnp.swapaxes(w_in[l], 0, 1)
        w_in_t = _cast_head_rows(w_in_f32_t, kr_col, _pick(CAST_ROWS, kr_col))
        n_tail = w_in_f32_t.shape[0] - kr_col
        conv_steps = (t // tl.proj_rows) * (conv_dim // tl.conv_cols)
        tail_in_conv = _cast_rows(n_tail, conv_steps, kr_col) is not None
        w_kr_t = jnp.swapaxes(w_in[l][:, kr_col:gate_col], 0, 1)
        w_krs_t = jnp.concatenate([w_kr_t[half:], w_kr_t[:half]], axis=0)
        pad_t = jnp.zeros((LANE - QK_ROPE_DIM, d), F32)
        wkvx_t = jnp.concatenate([w_in_t[kva_col:kr_col],
                                  jnp.concatenate([w_kr_t, pad_t, w_krs_t, pad_t], axis=0).astype(BF16)],
                                 axis=0)
        wq = _q_weights(w_q_b[l], n_heads)
        w_kv = w_kv_b[l].reshape(kv_rank, n_heads, QK_NOPE_DIM + V_HEAD_DIM)
        wkt = w_kv[..., :QK_NOPE_DIM].reshape(kv_rank, n_heads * QK_NOPE_DIM).T.astype(BF16)
        wv = w_kv[..., QK_NOPE_DIM:].reshape(kv_rank, n_heads * V_HEAD_DIM).astype(BF16)

        h, kt, v = _kv_proj(xf, g_mix[l][None, :], pos, invf, sgn, wkvx_t, g_kv_a[l][None, :], wkt, wv,
                            n_heads, batch, seq, tl.kv_rows)
        if tail_in_conv:
            cb, u, (w_tail_t,) = _conv_proj(h, w_in_t, conv_dim, (w_in_f32_t,), [(kr_col, n_tail)],
                                            tl.proj_rows, tl.conv_cols)
        else:
            cb, u, _ = _conv_proj(h, w_in_t, conv_dim, (), [], tl.proj_rows, tl.conv_cols)
            w_tail_t = w_in_f32_t[kr_col:].astype(BF16)
        gates, (w_br, w_o) = _gate_proj(h, w_tail_t, gate_col - kr_col, b_gate[l][None, :],
                                        (w_branch[l].reshape(-1, d), w_out[l]), tl.proj_rows, tl.gate_cols)
        w_br = w_br.reshape(w_branch.shape[1:])
        q = _q_proj(h, pos, invf, sgn, w_in_t, qa_col, g_q_a[l][None, :], wq, n_heads, tl.q_rows)
        yb, (w_fg, w_fu, w_fd) = _attention(q, kt, v, (w_ffn_gate[l], w_ffn_up[l], w_ffn_down[l]),
                                            n_heads, batch, seq, tl.attn_q, tl.attn_k)
        m = _branch(cb, u, yb, conv_w[l], gates, w_br, seq, tl.branch_rows, tl.branch_cols)
        xf = _out_proj(m, w_o, xf, tl.proj_rows, tl.gate_cols)
        xf = _ffn(xf, g_ffn[l][None, :], w_fg, w_fu, w_fd, g_final[None, :], l == depth - 1,
                  tl.ffn_rows, tl.ffn_cols)
    return xf.reshape(batch, seq, d)
```
